```python
import math
import jax
import jax.numpy as jnp
from jax import lax
import numpy as np

D_MODEL = 4096
BATCH = 4
SEQ = 4096
DEPTH = 2

GRID_W = 64
CTX_LEN = 256
CHUNK = 64
CONV_SIZE = 3
EPS = 1e-6
BRANCH_W = D_MODEL // 2
HGRN_DK = 128
HGRN_HEADS = BRANCH_W // HGRN_DK
MLSTM_HEADS = 8
MLSTM_DV = BRANCH_W // MLSTM_HEADS
MLSTM_DK = MLSTM_DV // 2
SSD_HEADDIM = 64
SSD_HEADS = BRANCH_W // SSD_HEADDIM
SSD_GROUPS = 8
SSD_STATE = 128
GATE_RANK = D_MODEL // 16
N_BRANCH = 3
D_FF = 2 * D_MODEL
N_EXPERTS = 8
TOP_K = 2
D_EXPERT = 3 * D_MODEL // 4
N_DENSE = (DEPTH + 1) // 2
N_MOE = DEPTH // 2

PIECE_NAMES = ('ml_q', 'ml_k', 'ss_x', 'ss_B', 'ss_C',
               'hg_q', 'hg_f_fwd', 'hg_f_bwd', 'hg_i', 'hg_g',
               'ml_v', 'ml_z', 'ml_gates', 'ss_z', 'ss_dt', 'merge')
PIECE_SIZES = (MLSTM_HEADS * MLSTM_DK, MLSTM_HEADS * MLSTM_DK, BRANCH_W, SSD_GROUPS * SSD_STATE, SSD_GROUPS * SSD_STATE,
               BRANCH_W, BRANCH_W, BRANCH_W, BRANCH_W, BRANCH_W,
               BRANCH_W, BRANCH_W, 4 * MLSTM_HEADS, BRANCH_W, 2 * SSD_HEADS, GATE_RANK)
N_CONV_PIECES = 5
N_CONV = sum(PIECE_SIZES[:N_CONV_PIECES])
N_IN = sum(PIECE_SIZES)

kernel_name = 'hybrid_hgrn2_mlstm_ssd_moe_dit'


def rmsnorm(x, w):
    xf = x.astype(jnp.float32)
    y = xf * lax.rsqrt(jnp.mean(xf * xf, axis=-1, keepdims=True) + EPS)
    return (y * w.astype(jnp.float32)).astype(x.dtype)


def to_heads(u, n_heads):
    b, l, w = u.shape
    return u.reshape(b, l, n_heads, w // n_heads).transpose(0, 2, 1, 3)


def merge_heads(o):
    b, h, l, d = o.shape
    return o.transpose(0, 2, 1, 3).reshape(b, l, h * d)


def merge_heads_rmsnorm(o, w):
    o = o.astype(jnp.float32).transpose(0, 2, 1, 3)
    o = o * lax.rsqrt(jnp.mean(o * o, axis=-1, keepdims=True) + EPS)
    return o.reshape(o.shape[:2] + (-1,)) * w


def _cuts(sizes):
    return [int(s) for s in np.cumsum(sizes)[:-1]]


def to_chunks(a):
    a = a.reshape(a.shape[:2] + (a.shape[2] // CHUNK, CHUNK) + a.shape[3:])
    return jnp.moveaxis(a, 2, 0)


def from_chunks(a):
    a = jnp.moveaxis(a, 0, 2)
    return a.reshape(a.shape[:2] + (a.shape[2] * a.shape[3],) + a.shape[4:])


def grid_conv(u, w, b, rows):
    bsz, n, ch = u.shape
    y = lax.conv_general_dilated(u.reshape(bsz, rows, n // rows, ch), w[:, :, None, :].astype(u.dtype),
                                 window_strides=(1, 1), padding='SAME',
                                 dimension_numbers=('NHWC', 'HWIO', 'NHWC'), feature_group_count=ch)
    return y.reshape(bsz, n, ch) + b


def gla_scan(q, k, v, log_a, state):
    q, k, v, log_a = (a.astype(jnp.float32) for a in (q, k, v, log_a))
    per_dim = log_a.shape[-1] > 1
    mask = jnp.tril(jnp.ones((CHUNK, CHUNK), dtype=bool))

    def step(s, blk):
        qc, kc, vc, ac = blk
        b = jnp.cumsum(ac, axis=2)
        b_end = b[:, :, -1:]
        rel = b[:, :, :, None] - b[:, :, None, :]
        dec = jnp.exp(jnp.where(mask[:, :, None], rel, -jnp.inf))
        if per_dim:
            scores = jnp.einsum('bhid,bhjd,bhijd->bhij', qc, kc, dec)
        else:
            scores = jnp.einsum('bhid,bhjd->bhij', qc, kc) * dec[..., 0]
        o = (jnp.einsum('bhij,bhje->bhie', scores, vc)
             + jnp.einsum('bhcd,bhde->bhce', qc * jnp.exp(b), s))
        s = (jnp.exp(jnp.swapaxes(b_end, 2, 3)) * s
             + jnp.einsum('bhcd,bhce->bhde', kc * jnp.exp(b_end - b), vc))
        return s, o

    state, o = lax.scan(step, state, tuple(to_chunks(a) for a in (q, k, v, log_a)))
    return from_chunks(o), state


def mlstm_scan(q, k, v, log_f, i_pre, state):
    q, k, v, log_f, i_pre = (a.astype(jnp.float32) for a in (q, k, v, log_f, i_pre))
    mask = jnp.tril(jnp.ones((CHUNK, CHUNK), dtype=bool))

    def step(carry, blk):
        cmem, m = carry
        qc, kc, vc, fc, ic = blk
        b = jnp.cumsum(fc, axis=-1)
        w_in = jnp.where(mask, b[..., :, None] - b[..., None, :] + ic[..., None, :], -jnp.inf)
        w_st = b + m[..., None]
        m_row = jnp.maximum(jnp.max(w_in, axis=-1), w_st)
        p = jnp.exp(w_in - m_row[..., None]) * jnp.einsum('bhid,bhjd->bhij', qc, kc)
        num = (jnp.einsum('bhij,bhje->bhie', p, vc)
               + jnp.exp(w_st - m_row)[..., None] * jnp.einsum('bhcd,bhde->bhce', qc, cmem))
        h = num[..., :-1] / jnp.maximum(jnp.abs(num[..., -1:]), jnp.exp(-m_row)[..., None])
        w_end = b[..., -1:] - b + ic
        m_new = jnp.maximum(b[..., -1] + m, jnp.max(w_end, axis=-1))
        cmem = (jnp.exp(b[..., -1] + m - m_new)[..., None, None] * cmem
                + jnp.einsum('bhcd,bhce->bhde', kc * jnp.exp(w_end - m_new[..., None])[..., None], vc))
        return (cmem, m_new), h

    state, h = lax.scan(step, state, tuple(to_chunks(a) for a in (q, k, v, log_f, i_pre)))
    return from_chunks(h), state


def _flip(args):
    return tuple(jnp.flip(a, axis=2) for a in args)


def bidirectional_scan(scan, lat_fwd, ctx_fwd, lat_bwd, ctx_bwd, state0):
    o_ctx_f, s_f = scan(*ctx_fwd, state0)
    o_lat_f, _ = scan(*lat_fwd, s_f)
    o_ctx_b, s_b = scan(*_flip(ctx_bwd), state0)
    o_lat_b, _ = scan(*_flip(lat_bwd), s_b)
    return o_lat_f + jnp.flip(o_lat_b, axis=2), o_ctx_f + jnp.flip(o_ctx_b, axis=2)


def project(h, in_w, conv_w, conv_b, rows):
    p = h @ in_w
    conv = jax.nn.silu(grid_conv(p[..., :N_CONV], conv_w, conv_b, rows))
    parts = (jnp.split(conv, _cuts(PIECE_SIZES[:N_CONV_PIECES]), axis=-1)
             + jnp.split(p[..., N_CONV:], _cuts(PIECE_SIZES[N_CONV_PIECES:]), axis=-1))
    return dict(zip(PIECE_NAMES, parts))


def hgrn2_inputs(p, f_name, lb):
    q = to_heads(jax.nn.silu(p['hg_q']), HGRN_HEADS) * HGRN_DK ** -0.5
    u = to_heads(p[f_name], HGRN_HEADS).astype(jnp.float32)
    lbh = lb.astype(jnp.float32).reshape(HGRN_HEADS, 1, HGRN_DK)
    log_f = jnp.logaddexp(jnp.log(lbh), jnp.log1p(-lbh) + jax.nn.log_sigmoid(u))
    k = (1.0 - lbh) * jax.nn.sigmoid(-u)
    return (q, k, to_heads(p['hg_i'], HGRN_HEADS), log_f)


def mlstm_inputs(p, d, ib, fb):
    q = to_heads(p['ml_q'], MLSTM_HEADS)
    k = to_heads(p['ml_k'], MLSTM_HEADS) * MLSTM_DK ** -0.5
    v = to_heads(p['ml_v'], MLSTM_HEADS)
    v = jnp.concatenate([v, jnp.ones_like(v[..., :1])], axis=-1)
    g = p['ml_gates'].astype(jnp.float32)
    g = g.reshape(g.shape[:2] + (4, MLSTM_HEADS))
    i_pre = jnp.swapaxes(g[:, :, 2 * d] + ib[d], 1, 2)
    log_f = jnp.swapaxes(jax.nn.log_sigmoid(g[:, :, 2 * d + 1] + fb[d]), 1, 2)
    return (q, k, v, log_f, i_pre)


def ssd_inputs(p, d, a_log, dt_bias):
    x = to_heads(p['ss_x'], SSD_HEADS)
    rep = SSD_HEADS // SSD_GROUPS
    bm = jnp.repeat(to_heads(p['ss_B'], SSD_GROUPS), rep, axis=1)
    cm = jnp.repeat(to_heads(p['ss_C'], SSD_GROUPS), rep, axis=1)
    dt_raw = p['ss_dt'][..., d * SSD_HEADS:(d + 1) * SSD_HEADS].astype(jnp.float32)
    dt = jnp.swapaxes(jax.nn.softplus(dt_raw + dt_bias[d]), 1, 2)[..., None]
    log_a = -jnp.exp(a_log[d].astype(jnp.float32))[:, None, None] * dt
    return (cm, bm, x * dt, log_a)


def token_mixer(h, hc, rows, lb, in_w, conv_w, conv_b, hgrn_norm_w, ml_ib, ml_fb, ml_norm_w,
                ss_a_log, ss_dt_bias, ss_d, ss_norm_w, gate_w, gate_b, branch_w, out_w, with_ctx):
    pl = project(h, in_w, conv_w, conv_b, rows)
    pc = project(hc, in_w, conv_w, conv_b, 1)
    bsz = h.shape[0]

    hg0 = jnp.zeros((bsz, HGRN_HEADS, HGRN_DK, HGRN_DK), jnp.float32)
    hg_lat, hg_ctx = bidirectional_scan(
        gla_scan,
        hgrn2_inputs(pl, 'hg_f_fwd', lb), hgrn2_inputs(pc, 'hg_f_fwd', lb),
        hgrn2_inputs(pl, 'hg_f_bwd', lb), hgrn2_inputs(pc, 'hg_f_bwd', lb), hg0)

    ml0 = (jnp.zeros((bsz, MLSTM_HEADS, MLSTM_DK, MLSTM_DV + 1), jnp.float32),
           jnp.zeros((bsz, MLSTM_HEADS), jnp.float32))
    ml_lat, ml_ctx = bidirectional_scan(
        mlstm_scan,
        mlstm_inputs(pl, 0, ml_ib, ml_fb), mlstm_inputs(pc, 0, ml_ib, ml_fb),
        mlstm_inputs(pl, 1, ml_ib, ml_fb), mlstm_inputs(pc, 1, ml_ib, ml_fb), ml0)

    ss0 = jnp.zeros((bsz, SSD_HEADS, SSD_STATE, SSD_HEADDIM), jnp.float32)
    ss_lat, ss_ctx = bidirectional_scan(
        gla_scan,
        ssd_inputs(pl, 0, ss_a_log, ss_dt_bias), ssd_inputs(pc, 0, ss_a_log, ss_dt_bias),
        ssd_inputs(pl, 1, ss_a_log, ss_dt_bias), ssd_inputs(pc, 1, ss_a_log, ss_dt_bias), ss0)

    def finish(p, hg, ml, ss):
        y_hg = merge_heads_rmsnorm(hg, hgrn_norm_w) * jax.nn.silu(p['hg_g'])
        y_ml = merge_heads_rmsnorm(ml, ml_norm_w) * jax.nn.silu(p['ml_z'])
        y_ss = merge_heads(ss + ss_d[:, None, None] * to_heads(p['ss_x'], SSD_HEADS))
        y_ss = rmsnorm(y_ss * jax.nn.silu(p['ss_z']), ss_norm_w)
        g = jnp.split(jax.nn.sigmoid((p['merge'] @ gate_w + gate_b).astype(jnp.float32)), N_BRANCH, axis=-1)
        y = g[0] * (y_hg @ branch_w[0]) + g[1] * (y_ml @ branch_w[1]) + g[2] * (y_ss @ branch_w[2])
        return y @ out_w

    y_lat = finish(pl, hg_lat, ml_lat, ss_lat)
    y_ctx = finish(pc, hg_ctx, ml_ctx, ss_ctx) if with_ctx else None
    return y_lat, y_ctx


def swiglu(t, w1, w3, w2):
    return (jax.nn.silu(t @ w1) * (t @ w3)) @ w2


def moe_swiglu(t, router_w, w1, w3, w2):
    probs = jax.nn.softmax((t @ router_w).astype(jnp.float32), axis=-1)
    top_p, top_i = lax.top_k(probs, TOP_K)
    top_p = top_p / jnp.sum(top_p, axis=-1, keepdims=True)
    combine = jnp.sum(jax.nn.one_hot(top_i, N_EXPERTS, dtype=jnp.float32) * top_p[..., None], axis=-2)
    combine = combine.astype(t.dtype)
    out = jnp.zeros_like(t)
    for e in range(N_EXPERTS):
        out = out + combine[..., e:e + 1] * swiglu(t, w1[e], w3[e], w2[e])
    return out


def channel_mix(t, l, ffn_w1, ffn_w3, ffn_w2, router_w, moe_w1, moe_w3, moe_w2):
    i = l // 2
    if l % 2 == 0:
        return swiglu(t, ffn_w1[i], ffn_w3[i], ffn_w2[i])
    return moe_swiglu(t, router_w[i], moe_w1[i], moe_w3[i], moe_w2[i])


def setup_inputs(seed: int = 0) -> dict:
    key = jax.random.key(seed)
    ks = jax.random.split(key, 32)
    f32 = jnp.float32

    def nrm(k, shape):
        return jax.random.normal(k, shape, f32)

    def w(k, shape, fan_in):
        return nrm(k, shape) * fan_in ** -0.5

    def gain(k, shape):
        return 1.0 + 0.02 * nrm(k, shape)

    dt0 = jnp.exp(jax.random.uniform(ks[17], (DEPTH, 2, SSD_HEADS), f32, math.log(1e-3), math.log(1e-1)))
    return {
        'x': nrm(ks[0], (BATCH, SEQ, D_MODEL)),
        'c': nrm(ks[1], (BATCH, D_MODEL)),
        'ctx': nrm(ks[2], (BATCH, CTX_LEN, D_MODEL)),
        'c_ctx': nrm(ks[3], (D_MODEL,)),
        'mod_w': w(ks[4], (DEPTH, D_MODEL, 6 * D_MODEL), D_MODEL),
        'mod_b': 0.02 * nrm(ks[5], (DEPTH, 6 * D_MODEL)),
        'norm1_w': gain(ks[6], (DEPTH, D_MODEL)),
        'norm2_w': gain(ks[7], (DEPTH, D_MODEL)),
        'in_w': w(ks[8], (DEPTH, D_MODEL, N_IN), D_MODEL),
        'conv_w': w(ks[9], (DEPTH, CONV_SIZE, CONV_SIZE, N_CONV), CONV_SIZE * CONV_SIZE),
        'conv_b': 0.02 * nrm(ks[10], (DEPTH, N_CONV)),
        'hgrn_lb': 0.1 * nrm(ks[11], (DEPTH, BRANCH_W)),
        'hgrn_norm_w': gain(ks[12], (DEPTH, BRANCH_W)),
        'mlstm_igate_b': 0.1 * nrm(ks[13], (DEPTH, 2, MLSTM_HEADS)),
        'mlstm_fgate_b': jnp.linspace(3.0, 6.0, MLSTM_HEADS, dtype=f32) + 0.1 * nrm(ks[14], (DEPTH, 2, MLSTM_HEADS)),
        'mlstm_norm_w': gain(ks[15], (DEPTH, BRANCH_W)),
        'ssd_a_log': jnp.log(jax.random.uniform(ks[16], (DEPTH, 2, SSD_HEADS), f32, 1.0, 16.0)),
        'ssd_dt_bias': dt0 + jnp.log(-jnp.expm1(-dt0)),
        'ssd_d': 1.0 + 0.1 * nrm(ks[18], (DEPTH, SSD_HEADS)),
        'ssd_norm_w': gain(ks[19], (DEPTH, BRANCH_W)),
        'gate_w': w(ks[20], (DEPTH, GATE_RANK, N_BRANCH * D_MODEL), GATE_RANK),
        'gate_b': 0.02 * nrm(ks[21], (DEPTH, N_BRANCH * D_MODEL)),
        'branch_w': w(ks[22], (DEPTH, N_BRANCH, BRANCH_W, D_MODEL), BRANCH_W),
        'out_w': w(ks[23], (DEPTH, D_MODEL, D_MODEL), D_MODEL),
        'ffn_w1': w(ks[24], (N_DENSE, D_MODEL, D_FF), D_MODEL),
        'ffn_w3': w(ks[25], (N_DENSE, D_MODEL, D_FF), D_MODEL),
        'ffn_w2': w(ks[26], (N_DENSE, D_FF, D_MODEL), D_FF),
        'router_w': w(ks[27], (N_MOE, D_MODEL, N_EXPERTS), D_MODEL),
        'moe_w1': w(ks[28], (N_MOE, N_EXPERTS, D_MODEL, D_EXPERT), D_MODEL),
        'moe_w3': w(ks[29], (N_MOE, N_EXPERTS, D_MODEL, D_EXPERT), D_MODEL),
        'moe_w2': w(ks[30], (N_MOE, N_EXPERTS, D_EXPERT, D_MODEL), D_EXPERT),
        'final_norm_w': gain(ks[31], (D_MODEL,)),
    }


def reference(x, c, ctx, c_ctx, mod_w, mod_b, norm1_w, norm2_w, in_w, conv_w, conv_b, hgrn_lb, hgrn_norm_w,
              mlstm_igate_b, mlstm_fgate_b, mlstm_norm_w, ssd_a_log, ssd_dt_bias, ssd_d, ssd_norm_w,
              gate_w, gate_b, branch_w, out_w, ffn_w1, ffn_w3, ffn_w2, router_w, moe_w1, moe_w3, moe_w2,
              final_norm_w):
    rows = x.shape[1] // GRID_W
    lb_cum = jnp.cumsum(jax.nn.softmax(hgrn_lb.astype(jnp.float32), axis=0), axis=0)
    lower_bounds = lb_cum - lb_cum[0]
    x_lat, x_ctx = x, ctx
    for l in range(DEPTH):
        last = l == DEPTH - 1
        mod = jax.nn.silu(c) @ mod_w[l] + mod_b[l]
        mod_c = jax.nn.silu(c_ctx) @ mod_w[l] + mod_b[l]
        sh1, sc1, g1, sh2, sc2, g2 = jnp.split(mod[:, None, :], 6, axis=-1)
        sh1c, sc1c, g1c, sh2c, sc2c, g2c = jnp.split(mod_c, 6)

        h = rmsnorm(x_lat, norm1_w[l]) * (1.0 + sc1) + sh1
        hc = rmsnorm(x_ctx, norm1_w[l]) * (1.0 + sc1c) + sh1c
        y, yc = token_mixer(h, hc, rows, lower_bounds[l], in_w[l], conv_w[l], conv_b[l], hgrn_norm_w[l],
                            mlstm_igate_b[l], mlstm_fgate_b[l], mlstm_norm_w[l],
                            ssd_a_log[l], ssd_dt_bias[l], ssd_d[l], ssd_norm_w[l],
                            gate_w[l], gate_b[l], branch_w[l], out_w[l], not last)
        x_lat = x_lat + g1 * y
        h = rmsnorm(x_lat, norm2_w[l]) * (1.0 + sc2) + sh2
        x_lat = x_lat + g2 * channel_mix(h, l, ffn_w1, ffn_w3, ffn_w2, router_w, moe_w1, moe_w3, moe_w2)
        if not last:
            x_ctx = x_ctx + g1c * yc
            hc = rmsnorm(x_ctx, norm2_w[l]) * (1.0 + sc2c) + sh2c
            x_ctx = x_ctx + g2c * channel_mix(hc, l, ffn_w1, ffn_w3, ffn_w2, router_w, moe_w1, moe_w3, moe_w2)
    return rmsnorm(x_lat, final_norm_w).astype(x.dtype)
```

```python
import functools
import math

import jax
import jax.numpy as jnp
from jax import lax
from jax.experimental import pallas as pl
from jax.experimental.pallas import tpu as pltpu

F32 = jnp.float32
BF16 = jnp.bfloat16

GRID_W = 64
EPS = 1e-6
HGRN_DK = 128
MLSTM_HEADS = 8
SSD_HEADDIM = 64
SSD_GROUPS = 8
SSD_STATE = 128
N_EXPERTS = 8
SUB = 16
LANES = 128
MOD_ROWS = 8
VMEM_LIMIT = 56 * 1024 * 1024


def _cparams(sem):
    return pltpu.CompilerParams(dimension_semantics=sem, vmem_limit_bytes=VMEM_LIMIT)


def _tile(n, pref):
    t = min(n, pref)
    while n % t:
        t //= 2
    return t


def _sigmoid(x):
    return 1.0 / (1.0 + jnp.exp(-x))


def _silu(x):
    return x * _sigmoid(x)


def _log_sigmoid(x):
    return jnp.minimum(x, 0.0) - jnp.log1p(jnp.exp(-jnp.abs(x)))


def _softplus(x):
    return jnp.maximum(x, 0.0) + jnp.log1p(jnp.exp(-jnp.abs(x)))


def _dot(a, b):
    return jnp.dot(a, b, preferred_element_type=F32)


def _dot_nt(a, b):
    return lax.dot_general(a, b, (((1,), (1,)), ((), ())), preferred_element_type=F32)


def _split3(x):
    x1 = x.astype(BF16)
    r = x - x1.astype(F32)
    x2 = r.astype(BF16)
    x3 = (r - x2.astype(F32)).astype(BF16)
    return x1, x2, x3


def _dot_exact(m01, x):
    x1, x2, x3 = _split3(x)
    return _dot(m01, x1) + _dot(m01, x2) + _dot(m01, x3)


def _mod_row(tile, n_ctx_tiles, tiles_per_batch, ctx_row):
    return jnp.where(tile < n_ctx_tiles, ctx_row, (tile - n_ctx_tiles) // tiles_per_batch)


def _mm_kernel(*refs, nk, a_silu, has_bias, resid, seg):
    it = iter(refs)
    a_ref, w_ref = next(it), next(it)
    bias_ref = next(it) if has_bias else None
    x_ref = next(it) if resid else None
    mod_ref = next(it) if resid else None
    out_ref, acc_ref = next(it), next(it)
    k = pl.program_id(2)
    row = _mod_row(pl.program_id(0), *seg) if resid else None

    @pl.when(k == 0)
    def _():
        acc_ref[...] = jnp.zeros_like(acc_ref)

    a = a_ref[...]
    if a_silu:
        a = _silu(a.astype(F32))
    acc_ref[...] += _dot(a.astype(BF16), w_ref[...].astype(BF16))

    @pl.when(k == nk - 1)
    def _():
        r = acc_ref[...]
        if has_bias:
            r = r + bias_ref[...]
        if resid:
            r = x_ref[...] + mod_ref[pl.ds(row, 1), :] * r
        out_ref[...] = r.astype(out_ref.dtype)


def _mm(a, w, *, out_dtype, tm, tn, tk, w_lead=None, a_silu=False, bias=None,
        resid=None, name):
    m, kdim = a.shape
    n = w.shape[-1]
    tm, tn, tk = _tile(m, tm), _tile(n, tn), _tile(kdim, tk)
    nk = kdim // tk
    if w_lead is None:
        w_spec = pl.BlockSpec((tk, tn), lambda i, j, k: (k, j))
    else:
        w_spec = pl.BlockSpec((None, tk, tn), lambda i, j, k: (w_lead, k, j))
    in_specs = [pl.BlockSpec((tm, tk), lambda i, j, k: (i, k)), w_spec]
    args = [a, w]
    if bias is not None:
        in_specs.append(pl.BlockSpec((1, tn), lambda i, j, k: (0, j)))
        args.append(bias)
    seg = None
    if resid is not None:
        x, mod, gate_off, seg_fn = resid
        seg = seg_fn(tm)
        gblk = gate_off // tn
        in_specs.append(pl.BlockSpec((tm, tn), lambda i, j, k: (i, j)))
        in_specs.append(pl.BlockSpec((MOD_ROWS, tn), lambda i, j, k: (0, gblk + j)))
        args += [x, mod]
    return pl.pallas_call(
        functools.partial(_mm_kernel, nk=nk, a_silu=a_silu, has_bias=bias is not None,
                          resid=resid is not None, seg=seg),
        grid=(m // tm, n // tn, nk),
        in_specs=in_specs,
        out_specs=pl.BlockSpec((tm, tn), lambda i, j, k: (i, j)),
        out_shape=jax.ShapeDtypeStruct((m, n), out_dtype),
        scratch_shapes=[pltpu.VMEM((tm, tn), F32)],
        compiler_params=_cparams(("parallel", "parallel", "arbitrary")),
        name=name,
    )(*args)


def _norm_mod_kernel(*refs, d, sh_off, sc_off, seg, router):
    if router:
        x_ref, nw_ref, mod_ref, rw_ref, out_ref, comb_ref = refs
    else:
        x_ref, nw_ref, mod_ref, out_ref = refs
    row = _mod_row(pl.program_id(0), *seg)
    x = x_ref[...]
    y = x * lax.rsqrt(jnp.mean(x * x, axis=-1, keepdims=True) + EPS) * nw_ref[...]
    sc = mod_ref[pl.ds(row, 1), sc_off:sc_off + d]
    sh = mod_ref[pl.ds(row, 1), sh_off:sh_off + d]
    h = y * (1.0 + sc) + sh
    out_ref[...] = h.astype(BF16)
    if router:
        h1, h2, h3 = _split3(h)
        r1, r2, r3 = _split3(rw_ref[...])
        logits = (_dot(h1, r1) + _dot(h1, r2) + _dot(h2, r1)
                  + _dot(h2, r2) + _dot(h1, r3) + _dot(h3, r1))
        lane = lax.broadcasted_iota(jnp.int32, logits.shape, 1).astype(F32)
        valid = lane < N_EXPERTS
        logits = jnp.where(valid, logits, -jnp.inf)
        mx = jnp.max(logits, axis=-1, keepdims=True)
        e = jnp.exp(logits - mx)
        probs = e / jnp.sum(e, axis=-1, keepdims=True)
        p1 = jnp.max(probs, axis=-1, keepdims=True)
        i1 = jnp.min(jnp.where((probs == p1) & valid, lane, float(LANES)), axis=-1, keepdims=True)
        rest = jnp.where((lane == i1) | (lane >= N_EXPERTS), -1.0, probs)
        p2 = jnp.max(rest, axis=-1, keepdims=True)
        i2 = jnp.min(jnp.where(rest == p2, lane, float(LANES)), axis=-1, keepdims=True)
        tot = p1 + p2
        comb_ref[...] = jnp.where(lane == i1, p1 / tot, jnp.where(lane == i2, p2 / tot, 0.0))


def _norm_mod(x, nw, mod, *, sh_off, sc_off, seg_fn, tr, router_w=None, name):
    t, d = x.shape
    tr = _tile(t, tr)
    router = router_w is not None
    in_specs = [pl.BlockSpec((tr, d), lambda i: (i, 0)),
                pl.BlockSpec((1, d), lambda i: (0, 0)),
                pl.BlockSpec(mod.shape, lambda i: (0, 0))]
    args = [x, nw, mod]
    out_specs = [pl.BlockSpec((tr, d), lambda i: (i, 0))]
    out_shape = [jax.ShapeDtypeStruct((t, d), BF16)]
    if router:
        in_specs.append(pl.BlockSpec(router_w.shape, lambda i: (0, 0)))
        args.append(router_w)
        out_specs.append(pl.BlockSpec((tr, LANES), lambda i: (i, 0)))
        out_shape.append(jax.ShapeDtypeStruct((t, LANES), F32))
    res = pl.pallas_call(
        functools.partial(_norm_mod_kernel, d=d, sh_off=sh_off, sc_off=sc_off, seg=seg_fn(tr),
                          router=router),
        grid=(t // tr,), in_specs=in_specs, out_specs=out_specs, out_shape=out_shape,
        compiler_params=_cparams(("parallel",)), name=name,
    )(*args)
    return res if router else res[0]


def _final_norm_kernel(x_ref, w_ref, out_ref):
    x = x_ref[...]
    out_ref[...] = x * lax.rsqrt(jnp.mean(x * x, axis=-1, keepdims=True) + EPS) * w_ref[...]


def _final_norm(x, w, *, row0, rows, tr):
    d = x.shape[1]
    tr = _tile(math.gcd(row0, rows), tr)
    off = row0 // tr
    return pl.pallas_call(
        _final_norm_kernel, grid=(rows // tr,),
        in_specs=[pl.BlockSpec((tr, d), lambda i: (i + off, 0)),
                  pl.BlockSpec((1, d), lambda i: (0, 0))],
        out_specs=pl.BlockSpec((tr, d), lambda i: (i, 0)),
        out_shape=jax.ShapeDtypeStruct((rows, d), F32),
        compiler_params=_cparams(("parallel",)), name="final_norm",
    )(x, w)


def _conv_kernel(main_ref, prev_ref, next_ref, w_ref, b_ref, out_ref, *,
                 tc, ctx_len, n_ctx_tiles, tiles_per_img):
    i = pl.program_id(0)
    is_ctx = i < n_ctx_tiles
    li = i - n_ctx_tiles
    first = (li % tiles_per_img) == 0
    last = (li % tiles_per_img) == tiles_per_img - 1
    main = main_ref[...]
    prev = jnp.where(is_ctx | first, 0.0, prev_ref[...])
    nxt = jnp.where(is_ctx | last, 0.0, next_ref[...])
    z = jnp.concatenate([prev, main, nxt], axis=0)
    n = tc + 2 * GRID_W
    pos = lax.broadcasted_iota(jnp.int32, (n, 1), 0)
    col_lat = pos % GRID_W
    col_ctx = (pos + (ctx_len - GRID_W)) % ctx_len
    col = jnp.where(is_ctx, col_ctx, col_lat)
    width = jnp.where(is_ctx, ctx_len, GRID_W)
    zl = jnp.where(col != 0, pltpu.roll(z, 1, 0), 0.0)
    zr = jnp.where(col != width - 1, pltpu.roll(z, n - 1, 0), 0.0)
    taps = (zl, z, zr)
    w = w_ref[...]
    acc = b_ref[...] + jnp.zeros((tc, main.shape[1]), F32)
    for dr in range(3):
        for dc in range(3):
            wt = w[3 * dr + dc:3 * dr + dc + 1, :]
            if dr != 1:
                wt = jnp.where(is_ctx, 0.0, wt)
            acc = acc + wt * taps[dc][GRID_W * dr:GRID_W * dr + tc, :]
    out_ref[...] = _silu(acc)


def _conv(p_main, conv_w9, conv_b, *, n_conv, tc, ctx_len, n_ctx_tiles, tiles_per_img, cb):
    t = p_main.shape[0]
    cb = _tile(n_conv, cb)
    rpt = tc // GRID_W
    nrow = t // GRID_W
    return pl.pallas_call(
        functools.partial(_conv_kernel, tc=tc, ctx_len=ctx_len, n_ctx_tiles=n_ctx_tiles,
                          tiles_per_img=tiles_per_img),
        grid=(t // tc, n_conv // cb),
        in_specs=[pl.BlockSpec((tc, cb), lambda i, j: (i, j)),
                  pl.BlockSpec((GRID_W, cb), lambda i, j: (jnp.maximum(i * rpt - 1, 0), j)),
                  pl.BlockSpec((GRID_W, cb), lambda i, j: (jnp.minimum((i + 1) * rpt, nrow - 1), j)),
                  pl.BlockSpec((9, cb), lambda i, j: (0, j)),
                  pl.BlockSpec((1, cb), lambda i, j: (0, j))],
        out_specs=pl.BlockSpec((tc, cb), lambda i, j: (i, j)),
        out_shape=jax.ShapeDtypeStruct((t, n_conv), F32),
        compiler_params=_cparams(("parallel", "parallel")), name="conv_silu",
    )(p_main, p_main, p_main, conv_w9, conv_b)


def _rowblk(b, s, rev, ncb, nlb, nb):
    if rev:
        ctx = b * ncb + (ncb - 1 - s)
        lat = nb * ncb + b * nlb + (nlb - 1 - (s - ncb))
    else:
        ctx = b * ncb + s
        lat = nb * ncb + b * nlb + (s - ncb)
    return jnp.where(s < ncb, ctx, lat)


def _tri(n, rev):
    r = lax.broadcasted_iota(jnp.int32, (n, n), 0)
    c = lax.broadcasted_iota(jnp.int32, (n, n), 1)
    return (c >= r) if rev else (c <= r)


def _sel_col(x, c):
    lane = lax.broadcasted_iota(jnp.int32, x.shape, 1)
    return jnp.sum(jnp.where(lane == c, x, 0.0), axis=1, keepdims=True)


def _sel_row(xt, c):
    sub = lax.broadcasted_iota(jnp.int32, xt.shape, 0)
    return jnp.sum(jnp.where(sub == c, xt, 0.0), axis=0, keepdims=True)


def _hgrn_dir(q_ref, u_ref, v_ref, lb_ref, o_ref, st_ref, d, rev, tb):
    dk = HGRN_DK
    lbv = lb_ref[...]
    qraw = q_ref[...]
    u = u_ref[...]
    v = v_ref[...]
    q = _silu(qraw) * dk ** -0.5
    la = jnp.log(lbv)
    lc = jnp.log1p(-lbv) + _log_sigmoid(u)
    mx = jnp.maximum(la, lc)
    mn = jnp.minimum(la, lc)
    logf = mx + jnp.log1p(jnp.exp(mn - mx))
    k = (1.0 - lbv) * _sigmoid(-u)

    r = lax.broadcasted_iota(jnp.int32, (tb, tb), 0)
    c = lax.broadcasted_iota(jnp.int32, (tb, tb), 1)
    same = (r // SUB) == (c // SUB)
    tri = (c >= r) if rev else (c <= r)
    m_cum = jnp.where(same & tri, 1.0, 0.0).astype(BF16)
    m_all = jnp.where(same, 1.0, 0.0).astype(BF16)
    bl = _dot_exact(m_cum, logf)
    be = _dot_exact(m_all, logf)
    qt = (q * jnp.exp(bl)).astype(BF16)
    kt = (k * jnp.exp(be - bl)).astype(BF16)
    ee = jnp.exp(be)
    vt = v.T.astype(BF16)
    vb = v.astype(BF16)

    si = lax.broadcasted_iota(jnp.int32, (SUB, SUB * SUB), 0)
    sp = lax.broadcasted_iota(jnp.int32, (SUB, SUB * SUB), 1)
    pi, pj = sp // SUB, sp % SUB
    keep = (pj >= pi) if rev else (pj <= pi)
    sel = jnp.where((si == pi) & keep, 1.0, 0.0).astype(BF16)
    ones = jnp.ones((dk, dk), BF16)

    st = st_ref[d]
    order = range(tb // SUB - 1, -1, -1) if rev else range(tb // SUB)
    for s in order:
        lo = s * SUB
        o_inter = _dot_nt(qt[lo:lo + SUB], st.astype(BF16))
        st = st * ee[lo:lo + 1, :] + _dot(vt[:, lo:lo + SUB], kt[lo:lo + SUB])
        qs, ks, bs, vs = q[lo:lo + SUB], k[lo:lo + SUB], bl[lo:lo + SUB], v[lo:lo + SUB]
        diff = bs[:, None, :] - bs[None, :, :]
        zz = qs[:, None, :] * jnp.exp(jnp.minimum(diff, 0.0)) * ks[None, :, :]
        rs = _dot(zz.reshape(SUB * SUB, dk).astype(BF16), ones)
        wv = rs.reshape(SUB, SUB, dk) * vs[None, :, :]
        o_intra = _dot(sel, wv.reshape(SUB * SUB, dk).astype(BF16))
        o_ref[lo:lo + SUB, :] = o_inter + o_intra
    st_ref[d] = st


def _hgrn_kernel(qf, uf, vf, qb, ub, vb, lb_ref, of, ob, st_ref, *, tb):
    @pl.when(pl.program_id(2) == 0)
    def _():
        st_ref[...] = jnp.zeros_like(st_ref)
    _hgrn_dir(qf, uf, vf, lb_ref, of, st_ref, 0, False, tb)
    _hgrn_dir(qb, ub, vb, lb_ref, ob, st_ref, 1, True, tb)


def _hgrn_scan(p_main, lb_row, *, nb, ncb, nlb, tb, heads, off_q, off_ff, off_fb, off_i):
    t = p_main.shape[0]
    dk = HGRN_DK

    def spec(off, rev):
        return pl.BlockSpec((tb, dk), lambda b, h, s: (_rowblk(b, s, rev, ncb, nlb, nb), off // dk + h))

    def ospec(rev):
        return pl.BlockSpec((tb, dk), lambda b, h, s: (_rowblk(b, s, rev, ncb, nlb, nb), h))

    out = jax.ShapeDtypeStruct((t, heads * dk), F32)
    return pl.pallas_call(
        functools.partial(_hgrn_kernel, tb=tb),
        grid=(nb, heads, ncb + nlb),
        in_specs=[spec(off_q, False), spec(off_ff, False), spec(off_i, False),
                  spec(off_q, True), spec(off_fb, True), spec(off_i, True),
                  pl.BlockSpec((1, dk), lambda b, h, s: (0, h))],
        out_specs=[ospec(False), ospec(True)],
        out_shape=[out, out],
        scratch_shapes=[pltpu.VMEM((2, dk, dk), F32)],
        compiler_params=_cparams(("parallel", "parallel", "arbitrary")), name="hgrn_scan",
    )(p_main, p_main, p_main, p_main, p_main, p_main, lb_row)


def _mlstm_dir(q_ref, k_ref, v_ref, g_ref, bias_ref, o_ref, c_ref, n_ref, m_ref, d, rev, tb, dk):
    h = pl.program_id(1)
    ci = 2 * MLSTM_HEADS * d + h
    cf = ci + MLSTM_HEADS
    q = q_ref[...]
    ks = k_ref[...] * dk ** -0.5
    v = v_ref[...]
    g = g_ref[...] + bias_ref[...]
    lf = _log_sigmoid(g)
    keep = _tri(tb, rev)
    b = _dot_exact(jnp.where(keep, 1.0, 0.0).astype(BF16), lf)
    b_col, ic_col = _sel_col(b, cf), _sel_col(g, ci)
    b_row, ic_row = _sel_row(b.T, cf), _sel_row(g.T, ci)
    b_end = b_col[0:1] if rev else b_col[tb - 1:tb]
    m_prev = m_ref[d][:, 0:1]
    w_in = jnp.where(keep, b_col - b_row + ic_row, -jnp.inf)
    w_st = b_col + m_prev
    m_row = jnp.maximum(jnp.max(w_in, axis=1, keepdims=True), w_st)
    qb = q.astype(BF16)
    vb = v.astype(BF16)
    p = jnp.exp(w_in - m_row) * _dot_nt(qb, ks.astype(BF16))
    e_st = jnp.exp(w_st - m_row)
    num = _dot(p.astype(BF16), vb) + e_st * _dot(qb, c_ref[d].astype(BF16))
    nrm = (jnp.sum(p, axis=1, keepdims=True)
           + e_st * jnp.sum(q * n_ref[d], axis=1, keepdims=True))
    o_ref[...] = num / jnp.maximum(jnp.abs(nrm), jnp.exp(-m_row))
    w_end = b_end - b_col + ic_col
    m_new = jnp.maximum(b_end + m_prev, jnp.max(w_end, axis=0, keepdims=True))
    s_old = jnp.exp(b_end + m_prev - m_new)
    kt = ks * jnp.exp(w_end - m_new)
    c_ref[d] = s_old * c_ref[d] + _dot(kt.T.astype(BF16), vb)
    n_ref[d] = s_old * n_ref[d] + jnp.sum(kt, axis=0, keepdims=True)
    m_ref[d] = jnp.broadcast_to(m_new, m_ref.shape[1:])


def _mlstm_kernel(qf, kf, vf, gf, qb, kb, vb, gb, bias_ref, of, ob, c_ref, n_ref, m_ref, *, tb, dk):
    @pl.when(pl.program_id(2) == 0)
    def _():
        c_ref[...] = jnp.zeros_like(c_ref)
        n_ref[...] = jnp.zeros_like(n_ref)
        m_ref[...] = jnp.zeros_like(m_ref)
    _mlstm_dir(qf, kf, vf, gf, bias_ref, of, c_ref, n_ref, m_ref, 0, False, tb, dk)
    _mlstm_dir(qb, kb, vb, gb, bias_ref, ob, c_ref, n_ref, m_ref, 1, True, tb, dk)


def _mlstm_scan(conv, p_main, p_tail, bias_row, *, nb, ncb, nlb, tb, dk, dv,
                off_q, off_k, off_v, off_small):
    t = conv.shape[0]

    def spec(width, off, rev):
        return pl.BlockSpec((tb, width),
                            lambda b, h, s: (_rowblk(b, s, rev, ncb, nlb, nb), off // width + h))

    def gspec(rev):
        return pl.BlockSpec((tb, LANES),
                            lambda b, h, s: (_rowblk(b, s, rev, ncb, nlb, nb), off_small // LANES))

    def ospec(rev):
        return pl.BlockSpec((tb, dv), lambda b, h, s: (_rowblk(b, s, rev, ncb, nlb, nb), h))

    out = jax.ShapeDtypeStruct((t, MLSTM_HEADS * dv), F32)
    return pl.pallas_call(
        functools.partial(_mlstm_kernel, tb=tb, dk=dk),
        grid=(nb, MLSTM_HEADS, ncb + nlb),
        in_specs=[spec(dk, off_q, False), spec(dk, off_k, False), spec(dv, off_v, False), gspec(False),
                  spec(dk, off_q, True), spec(dk, off_k, True), spec(dv, off_v, True), gspec(True),
                  pl.BlockSpec((1, LANES), lambda b, h, s: (0, 0))],
        out_specs=[ospec(False), ospec(True)],
        out_shape=[out, out],
        scratch_shapes=[pltpu.VMEM((2, dk, dv), F32), pltpu.VMEM((2, 1, dk), F32),
                        pltpu.VMEM((2, 1, LANES), F32)],
        compiler_params=_cparams(("parallel", "parallel", "arbitrary")), name="mlstm_scan",
    )(conv, conv, p_main, p_tail, conv, conv, p_main, p_tail, bias_row)


def _ssd_dir(c_ref, bm_ref, x_ref, g_ref, dtb_ref, alog_ref, o_ref, s_ref, d, rev, tb, heads, hpg):
    grp = pl.program_id(1)
    lane0 = MLSTM_HEADS * 4 + heads * d + hpg * grp
    cm = c_ref[...].astype(BF16)
    bm = bm_ref[...]
    x = x_ref[...]
    dt = _softplus(g_ref[...] + dtb_ref[...])
    la = -jnp.exp(alog_ref[...]) * dt
    keep = _tri(tb, rev)
    b = _dot_exact(jnp.where(keep, 1.0, 0.0).astype(BF16), la)
    bt = b.T
    width = hpg * SSD_HEADDIM
    hl = lax.broadcasted_iota(jnp.int32, (1, width), 1) // SSD_HEADDIM
    b_cols = [_sel_col(b, lane0 + i) for i in range(hpg)]
    b_all = jnp.zeros((tb, width), F32)
    dt_all = jnp.zeros((tb, width), F32)
    for i in range(hpg):
        b_all = jnp.where(hl == i, b_cols[i], b_all)
        dt_all = jnp.where(hl == i, _sel_col(dt, lane0 + i), dt_all)
    b_end = b_all[0:1] if rev else b_all[tb - 1:tb]
    v_all = x * dt_all
    v_bf = v_all.astype(BF16)
    gmat = _dot_nt(cm, bm.astype(BF16))
    s_old = s_ref[d]
    inter = jnp.exp(b_all) * _dot(cm, s_old.astype(BF16))
    outs = []
    for i in range(hpg):
        dec = jnp.where(keep, jnp.exp(jnp.minimum(b_cols[i] - _sel_row(bt, lane0 + i), 0.0)), 0.0)
        outs.append(_dot((gmat * dec).astype(BF16), v_bf[:, i * SSD_HEADDIM:(i + 1) * SSD_HEADDIM]))
    o_ref[...] = jnp.concatenate(outs, axis=1) + inter
    sv = (v_all * jnp.exp(b_end - b_all)).astype(BF16)
    s_ref[d] = jnp.exp(b_end) * s_old + _dot(bm.T.astype(BF16), sv)


def _ssd_kernel(cf, bf, xf, gf, cb, bb, xb, gb, dtb_ref, alog_ref, of, ob, s_ref, *, tb, heads, hpg):
    @pl.when(pl.program_id(2) == 0)
    def _():
        s_ref[...] = jnp.zeros_like(s_ref)
    _ssd_dir(cf, bf, xf, gf, dtb_ref, alog_ref, of, s_ref, 0, False, tb, heads, hpg)
    _ssd_dir(cb, bb, xb, gb, dtb_ref, alog_ref, ob, s_ref, 1, True, tb, heads, hpg)


def _ssd_scan(conv, p_tail, dtb_row, alog_row, *, nb, ncb, nlb, tb, heads, off_x, off_b, off_c, off_small):
    t = conv.shape[0]
    hpg = heads // SSD_GROUPS
    width = hpg * SSD_HEADDIM

    def spec(w, off, rev):
        return pl.BlockSpec((tb, w), lambda b, g, s: (_rowblk(b, s, rev, ncb, nlb, nb), off // w + g))

    def gspec(rev):
        return pl.BlockSpec((tb, LANES),
                            lambda b, g, s: (_rowblk(b, s, rev, ncb, nlb, nb), off_small // LANES))

    def ospec(rev):
        return pl.BlockSpec((tb, width), lambda b, g, s: (_rowblk(b, s, rev, ncb, nlb, nb), g))

    row = pl.BlockSpec((1, LANES), lambda b, g, s: (0, 0))
    out = jax.ShapeDtypeStruct((t, heads * SSD_HEADDIM), F32)
    return pl.pallas_call(
        functools.partial(_ssd_kernel, tb=tb, heads=heads, hpg=hpg),
        grid=(nb, SSD_GROUPS, ncb + nlb),
        in_specs=[spec(SSD_STATE, off_c, False), spec(SSD_STATE, off_b, False),
                  spec(width, off_x, False), gspec(False),
                  spec(SSD_STATE, off_c, True), spec(SSD_STATE, off_b, True),
                  spec(width, off_x, True), gspec(True), row, row],
        out_specs=[ospec(False), ospec(True)],
        out_shape=[out, out],
        scratch_shapes=[pltpu.VMEM((2, SSD_STATE, width), F32)],
        compiler_params=_cparams(("parallel", "parallel", "arbitrary")), name="ssd_scan",
    )(conv, conv, conv, p_tail, conv, conv, conv, p_tail, dtb_row, alog_row)


def _headnorm_kernel(of_ref, ob_ref, g_ref, w_ref, out_ref, *, hd):
    cb = of_ref.shape[1]
    for i in range(cb // hd):
        sl = slice(i * hd, (i + 1) * hd)
        o = of_ref[:, sl] + ob_ref[:, sl]
        y = o * lax.rsqrt(jnp.mean(o * o, axis=-1, keepdims=True) + EPS) * w_ref[:, sl]
        out_ref[:, sl] = (y * _silu(g_ref[:, sl])).astype(BF16)


def _headnorm(of, ob, gsrc, goff, w, *, hd, tr, cb, name):
    t, wd = of.shape
    tr, cb = _tile(t, tr), _tile(wd, cb)
    gblk = goff // cb
    return pl.pallas_call(
        functools.partial(_headnorm_kernel, hd=hd), grid=(t // tr, wd // cb),
        in_specs=[pl.BlockSpec((tr, cb), lambda i, j: (i, j)),
                  pl.BlockSpec((tr, cb), lambda i, j: (i, j)),
                  pl.BlockSpec((tr, cb), lambda i, j: (i, gblk + j)),
                  pl.BlockSpec((1, cb), lambda i, j: (0, j))],
        out_specs=pl.BlockSpec((tr, cb), lambda i, j: (i, j)),
        out_shape=jax.ShapeDtypeStruct((t, wd), BF16),
        compiler_params=_cparams(("parallel", "parallel")), name=name,
    )(of, ob, gsrc, w)


def _ssd_finish_kernel(of_ref, ob_ref, x_ref, z_ref, d_ref, w_ref, out_ref):
    y = of_ref[...] + ob_ref[...] + d_ref[...] * x_ref[...]
    tt = y * _silu(z_ref[...])
    out_ref[...] = (tt * lax.rsqrt(jnp.mean(tt * tt, axis=-1, keepdims=True) + EPS)
                    * w_ref[...]).astype(BF16)


def _ssd_finish(of, ob, conv, off_x, p_tail, off_z, d_row, w, *, tr):
    t, wd = of.shape
    tr = _tile(t, tr)
    row = pl.BlockSpec((1, wd), lambda i: (0, 0))
    return pl.pallas_call(
        _ssd_finish_kernel, grid=(t // tr,),
        in_specs=[pl.BlockSpec((tr, wd), lambda i: (i, 0)),
                  pl.BlockSpec((tr, wd), lambda i: (i, 0)),
                  pl.BlockSpec((tr, wd), lambda i: (i, off_x // wd)),
                  pl.BlockSpec((tr, wd), lambda i: (i, off_z // wd)), row, row],
        out_specs=pl.BlockSpec((tr, wd), lambda i: (i, 0)),
        out_shape=jax.ShapeDtypeStruct((t, wd), BF16),
        compiler_params=_cparams(("parallel",)), name="ssd_finish",
    )(of, ob, conv, p_tail, d_row, w)


def _merge_kernel(y0, y1, y2, bw_ref, mg_ref, gw_ref, gb_ref, out_ref, acc_ref):
    k = pl.program_id(2)
    gate = _sigmoid(_dot(mg_ref[...].astype(BF16), gw_ref[...]) + gb_ref[...])
    for idx, y_ref in enumerate((y0, y1, y2)):
        @pl.when(k == idx)
        def _(y_ref=y_ref, idx=idx):
            r = gate * _dot(y_ref[...], bw_ref[...])
            if idx == 0:
                acc_ref[...] = r
            elif idx == 1:
                acc_ref[...] += r
            else:
                out_ref[...] = (acc_ref[...] + r).astype(BF16)


def _merge(ys, bw, p_tail, off_merge, rank, gw, gb, *, tm, tn):
    t, bwid = ys[0].shape
    d = bw.shape[-1]
    tm, tn = _tile(t, tm), _tile(d, tn)
    nj = d // tn
    yspec = pl.BlockSpec((tm, bwid), lambda i, j, k: (i, 0))
    return pl.pallas_call(
        _merge_kernel, grid=(t // tm, nj, 3),
        in_specs=[yspec, yspec, yspec,
                  pl.BlockSpec((None, bwid, tn), lambda i, j, k: (k, 0, j)),
                  pl.BlockSpec((tm, rank), lambda i, j, k: (i, off_merge // rank)),
                  pl.BlockSpec((rank, tn), lambda i, j, k: (0, k * nj + j)),
                  pl.BlockSpec((1, tn), lambda i, j, k: (0, k * nj + j))],
        out_specs=pl.BlockSpec((tm, tn), lambda i, j, k: (i, j)),
        out_shape=jax.ShapeDtypeStruct((t, d), BF16),
        scratch_shapes=[pltpu.VMEM((tm, tn), F32)],
        compiler_params=_cparams(("parallel", "parallel", "arbitrary")), name="branch_merge",
    )(ys[0], ys[1], ys[2], bw, p_tail, gw, gb)


def _swiglu_kernel(*refs, nk, scaled):
    if scaled:
        a_ref, w1_ref, w3_ref, comb_ref, out_ref, acc1, acc3 = refs
    else:
        a_ref, w1_ref, w3_ref, out_ref, acc1, acc3 = refs
    k = pl.program_id(3)
    expert = pl.program_id(1)

    @pl.when(k == 0)
    def _():
        acc1[...] = jnp.zeros_like(acc1)
        acc3[...] = jnp.zeros_like(acc3)

    a = a_ref[...]
    acc1[...] += _dot(a, w1_ref[...])
    acc3[...] += _dot(a, w3_ref[...])

    @pl.when(k == nk - 1)
    def _():
        r = _silu(acc1[...]) * acc3[...]
        if scaled:
            r = r * _sel_col(comb_ref[...], expert)
        out_ref[...] = r.astype(BF16)


def _swiglu_up(a, w1, w3, comb=None, *, tm, tn, tk, name):
    t, kdim = a.shape
    ne, _, n = w1.shape
    tm, tn, tk = _tile(t, tm), _tile(n, tn), _tile(kdim, tk)
    nj, nk = n // tn, kdim // tk
    wspec = pl.BlockSpec((None, tk, tn), lambda i, e, j, k: (e, k, j))
    in_specs = [pl.BlockSpec((tm, tk), lambda i, e, j, k: (i, k)), wspec, wspec]
    args = [a, w1, w3]
    if comb is not None:
        in_specs.append(pl.BlockSpec((tm, LANES), lambda i, e, j, k: (i, 0)))
        args.append(comb)
    return pl.pallas_call(
        functools.partial(_swiglu_kernel, nk=nk, scaled=comb is not None),
        grid=(t // tm, ne, nj, nk),
        in_specs=in_specs,
        out_specs=pl.BlockSpec((tm, tn), lambda i, e, j, k: (i, e * nj + j)),
        out_shape=jax.ShapeDtypeStruct((t, ne * n), BF16),
        scratch_shapes=[pltpu.VMEM((tm, tn), F32), pltpu.VMEM((tm, tn), F32)],
        compiler_params=_cparams(("parallel", "parallel", "parallel", "arbitrary")), name=name,
    )(*args)


def kernel(x, c, ctx, c_ctx, mod_w, mod_b, norm1_w, norm2_w, in_w, conv_w, conv_b, hgrn_lb, hgrn_norm_w,
           mlstm_igate_b, mlstm_fgate_b, mlstm_norm_w, ssd_a_log, ssd_dt_bias, ssd_d, ssd_norm_w,
           gate_w, gate_b, branch_w, out_w, ffn_w1, ffn_w3, ffn_w2, router_w, moe_w1, moe_w3, moe_w2,
           final_norm_w):
    nb, seq, d = x.shape
    ctx_len = ctx.shape[1]
    depth = mod_w.shape[0]
    bw = d // 2
    hg_heads = bw // HGRN_DK
    ml_dv = bw // MLSTM_HEADS
    ml_dk = ml_dv // 2
    ss_heads = bw // SSD_HEADDIM
    rank = gate_w.shape[1]
    n_ml_qk = MLSTM_HEADS * ml_dk
    n_ss_bc = SSD_GROUPS * SSD_STATE
    off = {}
    pos = 0
    for nm, sz in (('ml_q', n_ml_qk), ('ml_k', n_ml_qk), ('ss_x', bw), ('ss_B', n_ss_bc), ('ss_C', n_ss_bc),
                   ('hg_q', bw), ('hg_f_fwd', bw), ('hg_f_bwd', bw), ('hg_i', bw), ('hg_g', bw),
                   ('ml_v', bw), ('ml_z', bw), ('ml_gates', 4 * MLSTM_HEADS), ('ss_z', bw),
                   ('ss_dt', 2 * ss_heads), ('merge', rank)):
        off[nm] = (pos, sz)
        pos += sz
    n_conv = off['hg_q'][0]
    n_main = off['ml_gates'][0]
    n_small = 4 * MLSTM_HEADS + 2 * ss_heads
    assert n_small <= LANES and ctx_len % GRID_W == 0 and seq % ctx_len == 0
    t_off_z, t_off_merge, t_off_small = 0, bw, bw + rank
    n_tail_raw = bw + rank + LANES
    n_tail = -(-n_tail_raw // 512) * 512

    tb = ctx_len
    ncb, nlb = 1, seq // tb
    n_ctx_rows = nb * ctx_len
    t = n_ctx_rows + nb * seq
    ctx_row = nb

    def seg_fn(tile):
        assert n_ctx_rows % tile == 0 and seq % tile == 0
        return (n_ctx_rows // tile, seq // tile, ctx_row)

    tm_big = _tile(math.gcd(n_ctx_rows, seq), 1024)

    xs = jnp.concatenate([ctx.reshape(n_ctx_rows, d), x.reshape(nb * seq, d)], axis=0)
    c_all = jnp.zeros((MOD_ROWS, d), F32).at[:nb].set(c).at[ctx_row].set(c_ctx)

    lb_cum = jnp.cumsum(jax.nn.softmax(hgrn_lb.astype(F32), axis=0), axis=0)
    lower_bounds = lb_cum - lb_cum[0]

    for l in range(depth):
        mod = _mm(c_all, mod_w, w_lead=l, out_dtype=F32, tm=MOD_ROWS, tn=1024, tk=2048,
                  a_silu=True, bias=mod_b[l][None, :], name="mod")
        h = _norm_mod(xs, norm1_w[l][None, :], mod, sh_off=0, sc_off=d, seg_fn=seg_fn, tr=tb, name="norm1")
        w_main = in_w[l, :, :n_main].astype(BF16)
        wl = in_w[l]
        w_tail = jnp.concatenate(
            [wl[:, off['ss_z'][0]:off['ss_z'][0] + bw],
             wl[:, off['merge'][0]:off['merge'][0] + rank],
             wl[:, off['ml_gates'][0]:off['ml_gates'][0] + 4 * MLSTM_HEADS],
             wl[:, off['ss_dt'][0]:off['ss_dt'][0] + 2 * ss_heads],
             jnp.zeros((d, n_tail - n_tail_raw + LANES - n_small), F32)], axis=1).astype(BF16)
        p_main = _mm(h, w_main, out_dtype=F32, tm=tm_big, tn=2048, tk=512, name="in_proj_main")
        p_tail = _mm(h, w_tail, out_dtype=F32, tm=tm_big, tn=n_tail, tk=512, name="in_proj_tail")
        conv = _conv(p_main, conv_w[l].reshape(9, n_conv), conv_b[l][None, :], n_conv=n_conv, tc=tb,
                     ctx_len=ctx_len, n_ctx_tiles=n_ctx_rows // tb, tiles_per_img=seq // tb, cb=512)

        hgf, hgb = _hgrn_scan(p_main, lower_bounds[l][None, :], nb=nb, ncb=ncb, nlb=nlb, tb=tb,
                              heads=hg_heads, off_q=off['hg_q'][0], off_ff=off['hg_f_fwd'][0],
                              off_fb=off['hg_f_bwd'][0], off_i=off['hg_i'][0])
        ml_bias = jnp.zeros((1, LANES), F32).at[0, :4 * MLSTM_HEADS].set(
            jnp.stack([mlstm_igate_b[l, 0], mlstm_fgate_b[l, 0],
                       mlstm_igate_b[l, 1], mlstm_fgate_b[l, 1]]).reshape(-1))
        mlf, mlb = _mlstm_scan(conv, p_main, p_tail, ml_bias, nb=nb, ncb=ncb, nlb=nlb, tb=tb,
                               dk=ml_dk, dv=ml_dv, off_q=off['ml_q'][0], off_k=off['ml_k'][0],
                               off_v=off['ml_v'][0], off_small=t_off_small)
        lo = 4 * MLSTM_HEADS
        dtb_row = jnp.zeros((1, LANES), F32).at[0, lo:lo + 2 * ss_heads].set(ssd_dt_bias[l].reshape(-1))
        alog_row = jnp.zeros((1, LANES), F32).at[0, lo:lo + 2 * ss_heads].set(ssd_a_log[l].reshape(-1))
        ssf, ssb = _ssd_scan(conv, p_tail, dtb_row, alog_row, nb=nb, ncb=ncb, nlb=nlb, tb=tb,
                             heads=ss_heads, off_x=off['ss_x'][0], off_b=off['ss_B'][0],
                             off_c=off['ss_C'][0], off_small=t_off_small)

        y_hg = _headnorm(hgf, hgb, p_main, off['hg_g'][0], hgrn_norm_w[l][None, :], hd=HGRN_DK,
                         tr=tb, cb=512, name="hgrn_finish")
        y_ml = _headnorm(mlf, mlb, p_main, off['ml_z'][0], mlstm_norm_w[l][None, :], hd=ml_dv,
                         tr=tb, cb=512, name="mlstm_finish")
        d_row = jnp.repeat(ssd_d[l], SSD_HEADDIM)[None, :]
        y_ss = _ssd_finish(ssf, ssb, conv, off['ss_x'][0], p_tail, t_off_z, d_row,
                           ssd_norm_w[l][None, :], tr=tb)
        ym = _merge((y_hg, y_ml, y_ss), branch_w[l].astype(BF16), p_tail, t_off_merge, rank,
                    gate_w[l].astype(BF16), gate_b[l][None, :], tm=tm_big, tn=1024)
        xs = _mm(ym, out_w[l].astype(BF16), out_dtype=F32, tm=tm_big, tn=1024, tk=1024,
                 resid=(xs, mod, 2 * d, seg_fn), name="out_proj")

        i = l // 2
        if l % 2 == 0:
            h2 = _norm_mod(xs, norm2_w[l][None, :], mod, sh_off=3 * d, sc_off=4 * d, seg_fn=seg_fn,
                           tr=tb, name="norm2")
            mid = _swiglu_up(h2, ffn_w1[i][None].astype(BF16), ffn_w3[i][None].astype(BF16),
                             tm=tm_big, tn=1024, tk=1024, name="ffn_up")
            w2 = ffn_w2[i].astype(BF16)
        else:
            rw = jnp.zeros((d, LANES), F32).at[:, :N_EXPERTS].set(router_w[i])
            h2, comb = _norm_mod(xs, norm2_w[l][None, :], mod, sh_off=3 * d, sc_off=4 * d, seg_fn=seg_fn,
                                 tr=tb, router_w=rw, name="norm2_router")
            mid = _swiglu_up(h2, moe_w1[i].astype(BF16), moe_w3[i].astype(BF16), comb,
                             tm=tm_big, tn=1024, tk=1024, name="moe_up")
            w2 = moe_w2[i].reshape(-1, d).astype(BF16)
        xs = _mm(mid, w2, out_dtype=F32, tm=tm_big, tn=1024, tk=1024,
                 resid=(xs, mod, 5 * d, seg_fn), name="ffn_down")

    out = _final_norm(xs, final_norm_w[None, :], row0=n_ctx_rows, rows=nb * seq, tr=tb)
    return out.reshape(nb, seq, d)
```

```python
import functools
import math

import jax
import jax.numpy as jnp
from jax import lax
from jax.experimental import pallas as pl
from jax.experimental.pallas import tpu as pltpu

F32 = jnp.float32
BF16 = jnp.bfloat16

GRID_W = 64
EPS = 1e-6
HGRN_DK = 128
MLSTM_HEADS = 8
SSD_HEADDIM = 64
SSD_GROUPS = 8
SSD_STATE = 128
N_EXPERTS = 8
LANES = 128
SUBLANES = 8
MOD_ROWS = 8
VMEM_LIMIT = 56 * 1024 * 1024


def _cparams(sem):
    return pltpu.CompilerParams(dimension_semantics=sem, vmem_limit_bytes=VMEM_LIMIT)


def _tile(n, pref):
    t = min(n, pref)
    while n % t:
        t //= 2
    return t


def _sigmoid(x):
    return 1.0 / (1.0 + jnp.exp(-x))


def _silu(x):
    return x * _sigmoid(x)


def _log_sigmoid(x):
    return jnp.minimum(x, 0.0) - jnp.log1p(jnp.exp(-jnp.abs(x)))


def _softplus(x):
    return jnp.maximum(x, 0.0) + jnp.log1p(jnp.exp(-jnp.abs(x)))


def _dot(a, b):
    return jnp.dot(a, b, preferred_element_type=F32)


def _dot_nt(a, b):
    return lax.dot_general(a, b, (((1,), (1,)), ((), ())), preferred_element_type=F32)


def _split3(x):
    x1 = x.astype(BF16)
    r = x - x1.astype(F32)
    x2 = r.astype(BF16)
    x3 = (r - x2.astype(F32)).astype(BF16)
    return x1, x2, x3


def _dot_exact(m01, x):
    x1, x2, x3 = _split3(x)
    return _dot(m01, x1) + _dot(m01, x2) + _dot(m01, x3)


def _mod_row(tile, n_ctx_tiles, tiles_per_batch, ctx_row):
    return jnp.where(tile < n_ctx_tiles, ctx_row, (tile - n_ctx_tiles) // tiles_per_batch)


def _mm_kernel(*refs, nk, a_silu, has_bias, resid, seg):
    it = iter(refs)
    a_ref, w_ref = next(it), next(it)
    bias_ref = next(it) if has_bias else None
    x_ref = next(it) if resid else None
    mod_ref = next(it) if resid else None
    out_ref, acc_ref = next(it), next(it)
    k = pl.program_id(2)
    row = _mod_row(pl.program_id(0), *seg) if resid else None

    @pl.when(k == 0)
    def _():
        acc_ref[...] = jnp.zeros_like(acc_ref)

    a = a_ref[...]
    if a_silu:
        a = _silu(a.astype(F32))
    acc_ref[...] += _dot(a.astype(BF16), w_ref[...].astype(BF16))

    @pl.when(k == nk - 1)
    def _():
        r = acc_ref[...]
        if has_bias:
            r = r + bias_ref[...]
        if resid:
            r = x_ref[...] + mod_ref[pl.ds(row, 1), :] * r
        out_ref[...] = r.astype(out_ref.dtype)


def _mm(a, w, *, out_dtype, tm, tn, tk, w_lead=None, a_silu=False, bias=None,
        resid=None, name):
    m, kdim = a.shape
    n = w.shape[-1]
    tm, tn, tk = _tile(m, tm), _tile(n, tn), _tile(kdim, tk)
    nk = kdim // tk
    if w_lead is None:
        w_spec = pl.BlockSpec((tk, tn), lambda i, j, k: (k, j))
    else:
        w_spec = pl.BlockSpec((None, tk, tn), lambda i, j, k: (w_lead, k, j))
    in_specs = [pl.BlockSpec((tm, tk), lambda i, j, k: (i, k)), w_spec]
    args = [a, w]
    if bias is not None:
        in_specs.append(pl.BlockSpec((1, tn), lambda i, j, k: (0, j)))
        args.append(bias)
    seg = None
    if resid is not None:
        x, mod, gate_off, seg_fn = resid
        seg = seg_fn(tm)
        gblk = gate_off // tn
        in_specs.append(pl.BlockSpec((tm, tn), lambda i, j, k: (i, j)))
        in_specs.append(pl.BlockSpec((MOD_ROWS, tn), lambda i, j, k: (0, gblk + j)))
        args += [x, mod]
    return pl.pallas_call(
        functools.partial(_mm_kernel, nk=nk, a_silu=a_silu, has_bias=bias is not None,
                          resid=resid is not None, seg=seg),
        grid=(m // tm, n // tn, nk),
        in_specs=in_specs,
        out_specs=pl.BlockSpec((tm, tn), lambda i, j, k: (i, j)),
        out_shape=jax.ShapeDtypeStruct((m, n), out_dtype),
        scratch_shapes=[pltpu.VMEM((tm, tn), F32)],
        compiler_params=_cparams(("parallel", "parallel", "arbitrary")),
        name=name,
    )(*args)


def _norm_mod_kernel(*refs, d, sh_off, sc_off, seg, router):
    if router:
        x_ref, nw_ref, mod_ref, rw_ref, out_ref, comb_ref, cnt_ref = refs
    else:
        x_ref, nw_ref, mod_ref, out_ref = refs
    row = _mod_row(pl.program_id(0), *seg)
    x = x_ref[...]
    y = x * lax.rsqrt(jnp.mean(x * x, axis=-1, keepdims=True) + EPS) * nw_ref[...]
    sc = mod_ref[pl.ds(row, 1), sc_off:sc_off + d]
    sh = mod_ref[pl.ds(row, 1), sh_off:sh_off + d]
    h = y * (1.0 + sc) + sh
    out_ref[...] = h.astype(BF16)
    if router:
        h1, h2, h3 = _split3(h)
        r1, r2, r3 = _split3(rw_ref[...])
        logits = (_dot(h1, r1) + _dot(h1, r2) + _dot(h2, r1)
                  + _dot(h2, r2) + _dot(h1, r3) + _dot(h3, r1))
        lane = lax.broadcasted_iota(jnp.int32, logits.shape, 1).astype(F32)
        valid = lane < N_EXPERTS
        logits = jnp.where(valid, logits, -jnp.inf)
        mx = jnp.max(logits, axis=-1, keepdims=True)
        e = jnp.exp(logits - mx)
        probs = e / jnp.sum(e, axis=-1, keepdims=True)
        p1 = jnp.max(probs, axis=-1, keepdims=True)
        i1 = jnp.min(jnp.where((probs == p1) & valid, lane, float(LANES)), axis=-1, keepdims=True)
        rest = jnp.where((lane == i1) | (lane >= N_EXPERTS), -1.0, probs)
        p2 = jnp.max(rest, axis=-1, keepdims=True)
        i2 = jnp.min(jnp.where(rest == p2, lane, float(LANES)), axis=-1, keepdims=True)
        tot = p1 + p2
        comb = jnp.where(lane == i1, p1 / tot, jnp.where(lane == i2, p2 / tot, 0.0))
        comb_ref[...] = comb
        cnt = jnp.sum(jnp.where(comb > 0.0, 1.0, 0.0), axis=0, keepdims=True)
        cnt_ref[...] = jnp.broadcast_to(cnt, cnt_ref.shape)


def _norm_mod(x, nw, mod, *, sh_off, sc_off, seg_fn, tr, router_w=None, name):
    t, d = x.shape
    tr = _tile(t, tr)
    router = router_w is not None
    in_specs = [pl.BlockSpec((tr, d), lambda i: (i, 0)),
                pl.BlockSpec((1, d), lambda i: (0, 0)),
                pl.BlockSpec(mod.shape, lambda i: (0, 0))]
    args = [x, nw, mod]
    out_specs = [pl.BlockSpec((tr, d), lambda i: (i, 0))]
    out_shape = [jax.ShapeDtypeStruct((t, d), BF16)]
    if router:
        in_specs.append(pl.BlockSpec(router_w.shape, lambda i: (0, 0)))
        args.append(router_w)
        out_specs.append(pl.BlockSpec((tr, LANES), lambda i: (i, 0)))
        out_shape.append(jax.ShapeDtypeStruct((t, LANES), F32))
        out_specs.append(pl.BlockSpec((SUBLANES, LANES), lambda i: (i, 0)))
        out_shape.append(jax.ShapeDtypeStruct((t // tr * SUBLANES, LANES), F32))
    res = pl.pallas_call(
        functools.partial(_norm_mod_kernel, d=d, sh_off=sh_off, sc_off=sc_off, seg=seg_fn(tr),
                          router=router),
        grid=(t // tr,), in_specs=in_specs, out_specs=out_specs, out_shape=out_shape,
        compiler_params=_cparams(("parallel",)), name=name,
    )(*args)
    return res if router else res[0]


def _final_norm_kernel(x_ref, w_ref, out_ref):
    x = x_ref[...]
    out_ref[...] = x * lax.rsqrt(jnp.mean(x * x, axis=-1, keepdims=True) + EPS) * w_ref[...]


def _final_norm(x, w, *, row0, rows, tr):
    d = x.shape[1]
    tr = _tile(math.gcd(row0, rows), tr)
    off = row0 // tr
    return pl.pallas_call(
        _final_norm_kernel, grid=(rows // tr,),
        in_specs=[pl.BlockSpec((tr, d), lambda i: (i + off, 0)),
                  pl.BlockSpec((1, d), lambda i: (0, 0))],
        out_specs=pl.BlockSpec((tr, d), lambda i: (i, 0)),
        out_shape=jax.ShapeDtypeStruct((rows, d), F32),
        compiler_params=_cparams(("parallel",)), name="final_norm",
    )(x, w)


def _conv_kernel(main_ref, prev_ref, next_ref, w_ref, b_ref, out_ref, *,
                 tc, ctx_len, n_ctx_tiles, tiles_per_img):
    i = pl.program_id(0)
    is_ctx = i < n_ctx_tiles
    li = i - n_ctx_tiles
    first = (li % tiles_per_img) == 0
    last = (li % tiles_per_img) == tiles_per_img - 1
    main = main_ref[...]
    prev = jnp.where(is_ctx | first, 0.0, prev_ref[...])
    nxt = jnp.where(is_ctx | last, 0.0, next_ref[...])
    z = jnp.concatenate([prev, main, nxt], axis=0)
    n = tc + 2 * GRID_W
    pos = lax.broadcasted_iota(jnp.int32, (n, 1), 0)
    col_lat = pos % GRID_W
    col_ctx = (pos + (ctx_len - GRID_W)) % ctx_len
    col = jnp.where(is_ctx, col_ctx, col_lat)
    width = jnp.where(is_ctx, ctx_len, GRID_W)
    zl = jnp.where(col != 0, pltpu.roll(z, 1, 0), 0.0)
    zr = jnp.where(col != width - 1, pltpu.roll(z, n - 1, 0), 0.0)
    taps = (zl, z, zr)
    w = w_ref[...]
    acc = b_ref[...] + jnp.zeros((tc, main.shape[1]), F32)
    for dr in range(3):
        for dc in range(3):
            wt = w[3 * dr + dc:3 * dr + dc + 1, :]
            if dr != 1:
                wt = jnp.where(is_ctx, 0.0, wt)
            acc = acc + wt * taps[dc][GRID_W * dr:GRID_W * dr + tc, :]
    out_ref[...] = _silu(acc)


def _conv(p_main, conv_w9, conv_b, *, n_conv, tc, ctx_len, n_ctx_tiles, tiles_per_img, cb):
    t = p_main.shape[0]
    cb = _tile(n_conv, cb)
    rpt = tc // GRID_W
    nrow = t // GRID_W
    return pl.pallas_call(
        functools.partial(_conv_kernel, tc=tc, ctx_len=ctx_len, n_ctx_tiles=n_ctx_tiles,
                          tiles_per_img=tiles_per_img),
        grid=(t // tc, n_conv // cb),
        in_specs=[pl.BlockSpec((tc, cb), lambda i, j: (i, j)),
                  pl.BlockSpec((GRID_W, cb), lambda i, j: (jnp.maximum(i * rpt - 1, 0), j)),
                  pl.BlockSpec((GRID_W, cb), lambda i, j: (jnp.minimum((i + 1) * rpt, nrow - 1), j)),
                  pl.BlockSpec((9, cb), lambda i, j: (0, j)),
                  pl.BlockSpec((1, cb), lambda i, j: (0, j))],
        out_specs=pl.BlockSpec((tc, cb), lambda i, j: (i, j)),
        out_shape=jax.ShapeDtypeStruct((t, n_conv), F32),
        compiler_params=_cparams(("parallel", "parallel")), name="conv_silu",
    )(p_main, p_main, p_main, conv_w9, conv_b)


def _rowblk(b, s, rev, ncb, nlb, nb):
    if rev:
        ctx = b * ncb + (ncb - 1 - s)
        lat = nb * ncb + b * nlb + (nlb - 1 - (s - ncb))
    else:
        ctx = b * ncb + s
        lat = nb * ncb + b * nlb + (s - ncb)
    return jnp.where(s < ncb, ctx, lat)


def _tri(n, rev):
    r = lax.broadcasted_iota(jnp.int32, (n, n), 0)
    c = lax.broadcasted_iota(jnp.int32, (n, n), 1)
    return (c >= r) if rev else (c <= r)


def _sel_col(x, c):
    lane = lax.broadcasted_iota(jnp.int32, x.shape, 1)
    return jnp.sum(jnp.where(lane == c, x, 0.0), axis=1, keepdims=True)


def _sel_row(xt, c):
    sub = lax.broadcasted_iota(jnp.int32, xt.shape, 0)
    return jnp.sum(jnp.where(sub == c, xt, 0.0), axis=0, keepdims=True)


def _level_ref(bl, half, rev):
    tb, dk = bl.shape
    blk = 2 * half
    idx = half if rev else half - 1
    if blk >= SUBLANES:
        b3 = bl.reshape(tb // blk, blk, dk)
        return jnp.broadcast_to(b3[:, idx:idx + 1, :], b3.shape).reshape(tb, dk)
    tmod = lax.broadcasted_iota(jnp.int32, (tb, 1), 0) % blk
    r = bl
    for m in range(blk):
        if idx != m:
            r = jnp.where(tmod == m, pltpu.roll(bl, (m - idx) % tb, 0), r)
    return r


def _hgrn_dir(q_ref, u_ref, v_ref, lb_ref, o_ref, st_ref, d, rev, tb):
    dk = HGRN_DK
    lbv = lb_ref[...]
    qraw = q_ref[...]
    u = u_ref[...]
    v = v_ref[...]
    q = _silu(qraw) * dk ** -0.5
    la = jnp.log(lbv)
    lc = jnp.log1p(-lbv) + _log_sigmoid(u)
    mx = jnp.maximum(la, lc)
    mn = jnp.minimum(la, lc)
    logf = mx + jnp.log1p(jnp.exp(mn - mx))
    k = (1.0 - lbv) * _sigmoid(-u)

    ri = lax.broadcasted_iota(jnp.int32, (tb, tb), 0)
    cj = lax.broadcasted_iota(jnp.int32, (tb, tb), 1)
    keep = (cj >= ri) if rev else (cj <= ri)
    bl = _dot_exact(jnp.where(keep, 1.0, 0.0).astype(BF16), logf)
    b_end = bl[0:1] if rev else bl[tb - 1:tb]
    level = jnp.where((cj > ri) if rev else (cj < ri), 31 - lax.clz(ri ^ cj), -1)
    scores = jnp.zeros((tb, tb), F32)
    for lv in range(tb.bit_length() - 1):
        dq = bl - _level_ref(bl, 1 << lv, rev)
        qt = (q * jnp.exp(dq)).astype(BF16)
        kt = (k * jnp.exp(-dq)).astype(BF16)
        scores = jnp.where(level == lv, _dot_nt(qt, kt), scores)
    vb = v.astype(BF16)
    st = st_ref[d]
    o_ref[...] = (_dot(scores.astype(BF16), vb) + jnp.sum(q * k, axis=-1, keepdims=True) * v
                  + _dot_nt((q * jnp.exp(bl)).astype(BF16), st.astype(BF16)))
    st_ref[d] = st * jnp.exp(b_end) + _dot(v.T.astype(BF16), (k * jnp.exp(b_end - bl)).astype(BF16))


def _hgrn_kernel(qf, uf, vf, qb, ub, vb, lb_ref, of, ob, st_ref, *, tb):
    @pl.when(pl.program_id(2) == 0)
    def _():
        st_ref[...] = jnp.zeros_like(st_ref)
    _hgrn_dir(qf, uf, vf, lb_ref, of, st_ref, 0, False, tb)
    _hgrn_dir(qb, ub, vb, lb_ref, ob, st_ref, 1, True, tb)


def _hgrn_scan(p_main, lb_row, *, nb, ncb, nlb, tb, heads, off_q, off_ff, off_fb, off_i):
    t = p_main.shape[0]
    dk = HGRN_DK

    def spec(off, rev):
        return pl.BlockSpec((tb, dk), lambda b, h, s: (_rowblk(b, s, rev, ncb, nlb, nb), off // dk + h))

    def ospec(rev):
        return pl.BlockSpec((tb, dk), lambda b, h, s: (_rowblk(b, s, rev, ncb, nlb, nb), h))

    out = jax.ShapeDtypeStruct((t, heads * dk), F32)
    return pl.pallas_call(
        functools.partial(_hgrn_kernel, tb=tb),
        grid=(nb, heads, ncb + nlb),
        in_specs=[spec(off_q, False), spec(off_ff, False), spec(off_i, False),
                  spec(off_q, True), spec(off_fb, True), spec(off_i, True),
                  pl.BlockSpec((1, dk), lambda b, h, s: (0, h))],
        out_specs=[ospec(False), ospec(True)],
        out_shape=[out, out],
        scratch_shapes=[pltpu.VMEM((2, dk, dk), F32)],
        compiler_params=_cparams(("parallel", "parallel", "arbitrary")), name="hgrn_scan",
    )(p_main, p_main, p_main, p_main, p_main, p_main, lb_row)


def _mlstm_dir(q_ref, k_ref, v_ref, g_ref, bias_ref, o_ref, c_ref, n_ref, m_ref, d, rev, tb, dk):
    h = pl.program_id(1)
    ci = 2 * MLSTM_HEADS * d + h
    cf = ci + MLSTM_HEADS
    q = q_ref[...]
    ks = k_ref[...] * dk ** -0.5
    v = v_ref[...]
    g = g_ref[...] + bias_ref[...]
    lf = _log_sigmoid(g)
    keep = _tri(tb, rev)
    b = _dot_exact(jnp.where(keep, 1.0, 0.0).astype(BF16), lf)
    b_col, ic_col = _sel_col(b, cf), _sel_col(g, ci)
    b_row, ic_row = _sel_row(b.T, cf), _sel_row(g.T, ci)
    b_end = b_col[0:1] if rev else b_col[tb - 1:tb]
    m_prev = m_ref[d][:, 0:1]
    w_in = jnp.where(keep, b_col - b_row + ic_row, -jnp.inf)
    w_st = b_col + m_prev
    m_row = jnp.maximum(jnp.max(w_in, axis=1, keepdims=True), w_st)
    qb = q.astype(BF16)
    vb = v.astype(BF16)
    p = jnp.exp(w_in - m_row) * _dot_nt(qb, ks.astype(BF16))
    e_st = jnp.exp(w_st - m_row)
    num = _dot(p.astype(BF16), vb) + e_st * _dot(qb, c_ref[d].astype(BF16))
    nrm = (jnp.sum(p, axis=1, keepdims=True)
           + e_st * jnp.sum(q * n_ref[d], axis=1, keepdims=True))
    o_ref[...] = num / jnp.maximum(jnp.abs(nrm), jnp.exp(-m_row))
    w_end = b_end - b_col + ic_col
    m_new = jnp.maximum(b_end + m_prev, jnp.max(w_end, axis=0, keepdims=True))
    s_old = jnp.exp(b_end + m_prev - m_new)
    kt = ks * jnp.exp(w_end - m_new)
    c_ref[d] = s_old * c_ref[d] + _dot(kt.T.astype(BF16), vb)
    n_ref[d] = s_old * n_ref[d] + jnp.sum(kt, axis=0, keepdims=True)
    m_ref[d] = jnp.broadcast_to(m_new, m_ref.shape[1:])


def _mlstm_kernel(qf, kf, vf, gf, qb, kb, vb, gb, bias_ref, of, ob, c_ref, n_ref, m_ref, *, tb, dk):
    @pl.when(pl.program_id(2) == 0)
    def _():
        c_ref[...] = jnp.zeros_like(c_ref)
        n_ref[...] = jnp.zeros_like(n_ref)
        m_ref[...] = jnp.zeros_like(m_ref)
    _mlstm_dir(qf, kf, vf, gf, bias_ref, of, c_ref, n_ref, m_ref, 0, False, tb, dk)
    _mlstm_dir(qb, kb, vb, gb, bias_ref, ob, c_ref, n_ref, m_ref, 1, True, tb, dk)


def _mlstm_scan(conv, p_main, p_tail, bias_row, *, nb, ncb, nlb, tb, dk, dv,
                off_q, off_k, off_v, off_small):
    t = conv.shape[0]

    def spec(width, off, rev):
        return pl.BlockSpec((tb, width),
                            lambda b, h, s: (_rowblk(b, s, rev, ncb, nlb, nb), off // width + h))

    def gspec(rev):
        return pl.BlockSpec((tb, LANES),
                            lambda b, h, s: (_rowblk(b, s, rev, ncb, nlb, nb), off_small // LANES))

    def ospec(rev):
        return pl.BlockSpec((tb, dv), lambda b, h, s: (_rowblk(b, s, rev, ncb, nlb, nb), h))

    out = jax.ShapeDtypeStruct((t, MLSTM_HEADS * dv), F32)
    return pl.pallas_call(
        functools.partial(_mlstm_kernel, tb=tb, dk=dk),
        grid=(nb, MLSTM_HEADS, ncb + nlb),
        in_specs=[spec(dk, off_q, False), spec(dk, off_k, False), spec(dv, off_v, False), gspec(False),
                  spec(dk, off_q, True), spec(dk, off_k, True), spec(dv, off_v, True), gspec(True),
                  pl.BlockSpec((1, LANES), lambda b, h, s: (0, 0))],
        out_specs=[ospec(False), ospec(True)],
        out_shape=[out, out],
        scratch_shapes=[pltpu.VMEM((2, dk, dv), F32), pltpu.VMEM((2, 1, dk), F32),
                        pltpu.VMEM((2, 1, LANES), F32)],
        compiler_params=_cparams(("parallel", "parallel", "arbitrary")), name="mlstm_scan",
    )(conv, conv, p_main, p_tail, conv, conv, p_main, p_tail, bias_row)


def _ssd_dir(c_ref, bm_ref, x_ref, g_ref, dtb_ref, alog_ref, o_ref, s_ref, d, rev, tb, heads, hpg):
    grp = pl.program_id(1)
    lane0 = MLSTM_HEADS * 4 + heads * d + hpg * grp
    cm = c_ref[...].astype(BF16)
    bm = bm_ref[...]
    x = x_ref[...]
    dt = _softplus(g_ref[...] + dtb_ref[...])
    la = -jnp.exp(alog_ref[...]) * dt
    keep = _tri(tb, rev)
    b = _dot_exact(jnp.where(keep, 1.0, 0.0).astype(BF16), la)
    bt = b.T
    width = hpg * SSD_HEADDIM
    hl = lax.broadcasted_iota(jnp.int32, (1, width), 1) // SSD_HEADDIM
    b_cols = [_sel_col(b, lane0 + i) for i in range(hpg)]
    b_all = jnp.zeros((tb, width), F32)
    dt_all = jnp.zeros((tb, width), F32)
    for i in range(hpg):
        b_all = jnp.where(hl == i, b_cols[i], b_all)
        dt_all = jnp.where(hl == i, _sel_col(dt, lane0 + i), dt_all)
    b_end = b_all[0:1] if rev else b_all[tb - 1:tb]
    v_all = x * dt_all
    v_bf = v_all.astype(BF16)
    gmat = _dot_nt(cm, bm.astype(BF16))
    s_old = s_ref[d]
    inter = jnp.exp(b_all) * _dot(cm, s_old.astype(BF16))
    outs = []
    for i in range(hpg):
        dec = jnp.where(keep, jnp.exp(jnp.minimum(b_cols[i] - _sel_row(bt, lane0 + i), 0.0)), 0.0)
        outs.append(_dot((gmat * dec).astype(BF16), v_bf[:, i * SSD_HEADDIM:(i + 1) * SSD_HEADDIM]))
    o_ref[...] = jnp.concatenate(outs, axis=1) + inter
    sv = (v_all * jnp.exp(b_end - b_all)).astype(BF16)
    s_ref[d] = jnp.exp(b_end) * s_old + _dot(bm.T.astype(BF16), sv)


def _ssd_kernel(cf, bf, xf, gf, cb, bb, xb, gb, dtb_ref, alog_ref, of, ob, s_ref, *, tb, heads, hpg):
    @pl.when(pl.program_id(2) == 0)
    def _():
        s_ref[...] = jnp.zeros_like(s_ref)
    _ssd_dir(cf, bf, xf, gf, dtb_ref, alog_ref, of, s_ref, 0, False, tb, heads, hpg)
    _ssd_dir(cb, bb, xb, gb, dtb_ref, alog_ref, ob, s_ref, 1, True, tb, heads, hpg)


def _ssd_scan(conv, p_tail, dtb_row, alog_row, *, nb, ncb, nlb, tb, heads, off_x, off_b, off_c, off_small):
    t = conv.shape[0]
    hpg = heads // SSD_GROUPS
    width = hpg * SSD_HEADDIM

    def spec(w, off, rev):
        return pl.BlockSpec((tb, w), lambda b, g, s: (_rowblk(b, s, rev, ncb, nlb, nb), off // w + g))

    def gspec(rev):
        return pl.BlockSpec((tb, LANES),
                            lambda b, g, s: (_rowblk(b, s, rev, ncb, nlb, nb), off_small // LANES))

    def ospec(rev):
        return pl.BlockSpec((tb, width), lambda b, g, s: (_rowblk(b, s, rev, ncb, nlb, nb), g))

    row = pl.BlockSpec((1, LANES), lambda b, g, s: (0, 0))
    out = jax.ShapeDtypeStruct((t, heads * SSD_HEADDIM), F32)
    return pl.pallas_call(
        functools.partial(_ssd_kernel, tb=tb, heads=heads, hpg=hpg),
        grid=(nb, SSD_GROUPS, ncb + nlb),
        in_specs=[spec(SSD_STATE, off_c, False), spec(SSD_STATE, off_b, False),
                  spec(width, off_x, False), gspec(False),
                  spec(SSD_STATE, off_c, True), spec(SSD_STATE, off_b, True),
                  spec(width, off_x, True), gspec(True), row, row],
        out_specs=[ospec(False), ospec(True)],
        out_shape=[out, out],
        scratch_shapes=[pltpu.VMEM((2, SSD_STATE, width), F32)],
        compiler_params=_cparams(("parallel", "parallel", "arbitrary")), name="ssd_scan",
    )(conv, conv, conv, p_tail, conv, conv, conv, p_tail, dtb_row, alog_row)


def _headnorm_kernel(of_ref, ob_ref, g_ref, w_ref, out_ref, *, hd):
    cb = of_ref.shape[1]
    for i in range(cb // hd):
        sl = slice(i * hd, (i + 1) * hd)
        o = of_ref[:, sl] + ob_ref[:, sl]
        y = o * lax.rsqrt(jnp.mean(o * o, axis=-1, keepdims=True) + EPS) * w_ref[:, sl]
        out_ref[:, sl] = (y * _silu(g_ref[:, sl])).astype(BF16)


def _headnorm(of, ob, gsrc, goff, w, *, hd, tr, cb, name):
    t, wd = of.shape
    tr, cb = _tile(t, tr), _tile(wd, cb)
    gblk = goff // cb
    return pl.pallas_call(
        functools.partial(_headnorm_kernel, hd=hd), grid=(t // tr, wd // cb),
        in_specs=[pl.BlockSpec((tr, cb), lambda i, j: (i, j)),
                  pl.BlockSpec((tr, cb), lambda i, j: (i, j)),
                  pl.BlockSpec((tr, cb), lambda i, j: (i, gblk + j)),
                  pl.BlockSpec((1, cb), lambda i, j: (0, j))],
        out_specs=pl.BlockSpec((tr, cb), lambda i, j: (i, j)),
        out_shape=jax.ShapeDtypeStruct((t, wd), BF16),
        compiler_params=_cparams(("parallel", "parallel")), name=name,
    )(of, ob, gsrc, w)


def _ssd_finish_kernel(of_ref, ob_ref, x_ref, z_ref, d_ref, w_ref, out_ref):
    y = of_ref[...] + ob_ref[...] + d_ref[...] * x_ref[...]
    tt = y * _silu(z_ref[...])
    out_ref[...] = (tt * lax.rsqrt(jnp.mean(tt * tt, axis=-1, keepdims=True) + EPS)
                    * w_ref[...]).astype(BF16)


def _ssd_finish(of, ob, conv, off_x, p_tail, off_z, d_row, w, *, tr):
    t, wd = of.shape
    tr = _tile(t, tr)
    row = pl.BlockSpec((1, wd), lambda i: (0, 0))
    return pl.pallas_call(
        _ssd_finish_kernel, grid=(t // tr,),
        in_specs=[pl.BlockSpec((tr, wd), lambda i: (i, 0)),
                  pl.BlockSpec((tr, wd), lambda i: (i, 0)),
                  pl.BlockSpec((tr, wd), lambda i: (i, off_x // wd)),
                  pl.BlockSpec((tr, wd), lambda i: (i, off_z // wd)), row, row],
        out_specs=pl.BlockSpec((tr, wd), lambda i: (i, 0)),
        out_shape=jax.ShapeDtypeStruct((t, wd), BF16),
        compiler_params=_cparams(("parallel",)), name="ssd_finish",
    )(of, ob, conv, p_tail, d_row, w)


def _merge_kernel(y0, y1, y2, bw_ref, mg_ref, gw_ref, gb_ref, out_ref, acc_ref):
    k = pl.program_id(2)
    gate = _sigmoid(_dot(mg_ref[...].astype(BF16), gw_ref[...]) + gb_ref[...])
    for idx, y_ref in enumerate((y0, y1, y2)):
        @pl.when(k == idx)
        def _(y_ref=y_ref, idx=idx):
            r = gate * _dot(y_ref[...], bw_ref[...])
            if idx == 0:
                acc_ref[...] = r
            elif idx == 1:
                acc_ref[...] += r
            else:
                out_ref[...] = (acc_ref[...] + r).astype(BF16)


def _merge(ys, bw, p_tail, off_merge, rank, gw, gb, *, tm, tn):
    t, bwid = ys[0].shape
    d = bw.shape[-1]
    tm, tn = _tile(t, tm), _tile(d, tn)
    nj = d // tn
    yspec = pl.BlockSpec((tm, bwid), lambda i, j, k: (i, 0))
    return pl.pallas_call(
        _merge_kernel, grid=(t // tm, nj, 3),
        in_specs=[yspec, yspec, yspec,
                  pl.BlockSpec((None, bwid, tn), lambda i, j, k: (k, 0, j)),
                  pl.BlockSpec((tm, rank), lambda i, j, k: (i, off_merge // rank)),
                  pl.BlockSpec((rank, tn), lambda i, j, k: (0, k * nj + j)),
                  pl.BlockSpec((1, tn), lambda i, j, k: (0, k * nj + j))],
        out_specs=pl.BlockSpec((tm, tn), lambda i, j, k: (i, j)),
        out_shape=jax.ShapeDtypeStruct((t, d), BF16),
        scratch_shapes=[pltpu.VMEM((tm, tn), F32)],
        compiler_params=_cparams(("parallel", "parallel", "arbitrary")), name="branch_merge",
    )(ys[0], ys[1], ys[2], bw, p_tail, gw, gb)


def _swiglu_kernel(a_ref, w1_ref, w3_ref, out_ref, acc1, acc3, *, nk):
    k = pl.program_id(2)

    @pl.when(k == 0)
    def _():
        acc1[...] = jnp.zeros_like(acc1)
        acc3[...] = jnp.zeros_like(acc3)

    a = a_ref[...]
    acc1[...] += _dot(a, w1_ref[...])
    acc3[...] += _dot(a, w3_ref[...])

    @pl.when(k == nk - 1)
    def _():
        out_ref[...] = (_silu(acc1[...]) * acc3[...]).astype(BF16)


def _swiglu_up(a, w1, w3, *, tm, tn, tk, name):
    t, kdim = a.shape
    n = w1.shape[1]
    tm, tn, tk = _tile(t, tm), _tile(n, tn), _tile(kdim, tk)
    nk = kdim // tk
    wspec = pl.BlockSpec((tk, tn), lambda i, j, k: (k, j))
    return pl.pallas_call(
        functools.partial(_swiglu_kernel, nk=nk),
        grid=(t // tm, n // tn, nk),
        in_specs=[pl.BlockSpec((tm, tk), lambda i, j, k: (i, k)), wspec, wspec],
        out_specs=pl.BlockSpec((tm, tn), lambda i, j, k: (i, j)),
        out_shape=jax.ShapeDtypeStruct((t, n), BF16),
        scratch_shapes=[pltpu.VMEM((tm, tn), F32), pltpu.VMEM((tm, tn), F32)],
        compiler_params=_cparams(("parallel", "parallel", "arbitrary")), name=name,
    )(a, w1, w3)


MOE_RB = LANES
MOE_MB = 512


def _moe_tables(cnt_tile, tm):
    nt, ne = cnt_tile.shape
    rb, bpm = MOE_RB, MOE_MB // MOE_RB
    ni = 2 * tm // rb + ne
    nblk = -(-(nt * ni + ne * (bpm - 1)) // bpm) * bpm
    nbk = (cnt_tile + (rb - 1)) // rb
    reg_e = (jnp.sum(nbk, axis=0) + (bpm - 1)) // bpm * bpm
    end_e = jnp.cumsum(reg_e)
    start_e = end_e - reg_e
    pre_ie = jnp.cumsum(nbk, axis=0) - nbk
    ends_ie = jnp.cumsum(nbk, axis=1)
    off_ie = ends_ie - nbk
    n_items = ends_ie[:, -1]
    it = jnp.arange(ni, dtype=jnp.int32)[None, :]
    itc = jnp.minimum(it, n_items[:, None] - 1)
    e_idx = jnp.sum((itc[:, :, None] >= ends_ie[:, None, :]).astype(jnp.int32), axis=-1)
    e_idx = jnp.minimum(e_idx, ne - 1)
    chunk = itc - jnp.take_along_axis(off_ie, e_idx, axis=1)
    gblk = start_e[e_idx] + jnp.take_along_axis(pre_ie, e_idx, axis=1) + chunk
    iout = jnp.where(it < n_items[:, None], gblk, nblk)
    m = jnp.arange(nblk // bpm, dtype=jnp.int32)
    mexp = jnp.sum((m[:, None] * bpm >= end_e[None, :]).astype(jnp.int32), axis=-1)
    off_row = jnp.zeros((nt, 1, LANES), F32).at[:, 0, :ne].set((off_ie * rb).astype(F32))
    return dict(ni=ni, nblk=nblk, iout=iout.reshape(-1).astype(jnp.int32),
                iblk=gblk.reshape(-1).astype(jnp.int32), n_items=n_items.astype(jnp.int32),
                mexp=jnp.minimum(mexp, ne - 1).astype(jnp.int32),
                nvalid=(end_e[-1:] // bpm).astype(jnp.int32), off_row=off_row)


def _moe_gather_kernel(iout_ref, h_ref, comb_ref, off_ref, za_ref, zw_ref,
                       a_ref, w_ref, l0_ref, l1_ref, lt_ref, wt_ref, *, tm, rb):
    del iout_ref, za_ref, zw_ref
    it = pl.program_id(1)

    @pl.when(it == 0)
    def _():
        comb = comb_ref[...]
        pick = comb > 0.0
        r = lax.broadcasted_iota(jnp.int32, (tm, tm), 0)
        c = lax.broadcasted_iota(jnp.int32, (tm, tm), 1)
        rank = _dot(jnp.where(c < r, 1.0, 0.0).astype(BF16), jnp.where(pick, 1.0, 0.0).astype(BF16))
        loc = off_ref[...] + rank
        lane = lax.broadcasted_iota(jnp.int32, comb.shape, 1).astype(F32)
        m1 = jnp.min(jnp.where(pick, lane, float(LANES)), axis=1, keepdims=True)
        first = pick & (lane == m1)
        second = pick & (lane != m1)
        l0 = jnp.sum(jnp.where(first, loc, 0.0), axis=1, keepdims=True)
        l1 = jnp.sum(jnp.where(second, loc + 1.0, 0.0), axis=1, keepdims=True) - 1.0
        l0b = jnp.broadcast_to(l0, comb.shape)
        l1b = jnp.broadcast_to(l1, comb.shape)
        l0_ref[...] = l0b
        l1_ref[...] = l1b
        lt_ref[0] = l0b.T
        lt_ref[1] = l1b.T
        wt_ref[0] = jnp.broadcast_to(jnp.sum(jnp.where(first, comb, 0.0), axis=1, keepdims=True), comb.shape)
        wt_ref[1] = jnp.broadcast_to(jnp.sum(jnp.where(second, comb, 0.0), axis=1, keepdims=True), comb.shape)

    s = (it * rb + lax.broadcasted_iota(jnp.int32, (rb, 1), 0)).astype(F32)
    p0 = jnp.where(lt_ref[0, 0:1, :] == s, 1.0, 0.0).astype(BF16)
    p1 = jnp.where(lt_ref[1, 0:1, :] == s, 1.0, 0.0).astype(BF16)
    a_ref[...] = _dot(p0 + p1, h_ref[...]).astype(BF16)
    w_ref[...] = _dot_exact(p0, wt_ref[0]) + _dot_exact(p1, wt_ref[1])


def _moe_gather(h, comb, tab, *, tm):
    t, d = h.shape
    rb, ni = MOE_RB, tab['ni']
    nrows = tab['nblk'] * rb + MOE_MB
    grid_spec = pltpu.PrefetchScalarGridSpec(
        num_scalar_prefetch=1, grid=(t // tm, ni),
        in_specs=[pl.BlockSpec((tm, d), lambda i, it, io: (i, 0)),
                  pl.BlockSpec((tm, LANES), lambda i, it, io: (i, 0)),
                  pl.BlockSpec((None, 1, LANES), lambda i, it, io: (i, 0, 0)),
                  pl.BlockSpec(memory_space=pl.ANY), pl.BlockSpec(memory_space=pl.ANY)],
        out_specs=[pl.BlockSpec((rb, d), lambda i, it, io: (io[i * ni + it], 0)),
                   pl.BlockSpec((rb, LANES), lambda i, it, io: (io[i * ni + it], 0)),
                   pl.BlockSpec((tm, LANES), lambda i, it, io: (i, 0)),
                   pl.BlockSpec((tm, LANES), lambda i, it, io: (i, 0))],
        scratch_shapes=[pltpu.VMEM((2, LANES, tm), F32), pltpu.VMEM((2, tm, LANES), F32)])
    return pl.pallas_call(
        functools.partial(_moe_gather_kernel, tm=tm, rb=rb), grid_spec=grid_spec,
        out_shape=[jax.ShapeDtypeStruct((nrows, d), BF16), jax.ShapeDtypeStruct((nrows, LANES), F32),
                   jax.ShapeDtypeStruct((t, LANES), F32), jax.ShapeDtypeStruct((t, LANES), F32)],
        input_output_aliases={4: 0, 5: 1},
        compiler_params=_cparams(("parallel", "arbitrary")), name="moe_gather",
    )(tab['iout'], h, comb, tab['off_row'], jnp.zeros((nrows, d), BF16), jnp.zeros((nrows, LANES), F32))


def _moe_grouped_kernel(*refs, nk, dual):
    if dual:
        mexp_ref, nv_ref, a_ref, w1_ref, w3_ref, ws_ref, out_ref, acc1, acc3 = refs
    else:
        mexp_ref, nv_ref, a_ref, w1_ref, out_ref, acc1 = refs
    del mexp_ref
    k = pl.program_id(2)

    @pl.when(pl.program_id(0) < nv_ref[0])
    def _():
        @pl.when(k == 0)
        def _():
            acc1[...] = jnp.zeros_like(acc1)
            if dual:
                acc3[...] = jnp.zeros_like(acc3)

        a = a_ref[...]
        acc1[...] += _dot(a, w1_ref[...])
        if dual:
            acc3[...] += _dot(a, w3_ref[...])

        @pl.when(k == nk - 1)
        def _():
            if dual:
                out_ref[...] = (_silu(acc1[...]) * acc3[...] * ws_ref[:, 0:1]).astype(BF16)
            else:
                out_ref[...] = acc1[...].astype(BF16)


def _moe_grouped(a, ws, w_sorted, tab, *, tn, tk, name):
    rows, kdim = a.shape
    n = ws[0].shape[-1]
    tn, tk = _tile(n, tn), _tile(kdim, tk)
    nj, nk = n // tn, kdim // tk
    nmb = tab['nblk'] * MOE_RB // MOE_MB
    dual = len(ws) == 2

    def live(m, nv):
        return m < nv[0]

    def me(m, nv):
        return jnp.minimum(m, nv[0] - 1)

    a_spec = pl.BlockSpec((MOE_MB, tk), lambda m, j, k, ex, nv: (me(m, nv), jnp.where(live(m, nv), k, nk - 1)))
    w_spec = pl.BlockSpec((None, tk, tn), lambda m, j, k, ex, nv: (
        ex[me(m, nv)], jnp.where(live(m, nv), k, nk - 1), jnp.where(live(m, nv), j, nj - 1)))
    o_spec = pl.BlockSpec((MOE_MB, tn), lambda m, j, k, ex, nv: (me(m, nv), jnp.where(live(m, nv), j, nj - 1)))
    in_specs = [a_spec, w_spec]
    args = [a, ws[0]]
    scratch = [pltpu.VMEM((MOE_MB, tn), F32)]
    if dual:
        in_specs += [w_spec, pl.BlockSpec((MOE_MB, LANES), lambda m, j, k, ex, nv: (me(m, nv), 0))]
        args += [ws[1], w_sorted]
        scratch.append(pltpu.VMEM((MOE_MB, tn), F32))
    grid_spec = pltpu.PrefetchScalarGridSpec(
        num_scalar_prefetch=2, grid=(nmb, nj, nk), in_specs=in_specs, out_specs=o_spec,
        scratch_shapes=scratch)
    return pl.pallas_call(
        functools.partial(_moe_grouped_kernel, nk=nk, dual=dual), grid_spec=grid_spec,
        out_shape=jax.ShapeDtypeStruct((rows, n), BF16),
        compiler_params=_cparams(("arbitrary", "arbitrary", "arbitrary")), name=name,
    )(tab['mexp'], tab['nvalid'], *args)


def _moe_scatter_kernel(iblk_ref, nit_ref, l0_ref, l1_ref, y_ref, x_ref, mod_ref, out_ref, acc_ref,
                        *, rb, ni, seg):
    del iblk_ref
    i = pl.program_id(0)
    it = pl.program_id(2)
    row = _mod_row(i, *seg)

    @pl.when(it == 0)
    def _():
        acc_ref[...] = jnp.zeros_like(acc_ref)

    @pl.when(it < nit_ref[i])
    def _():
        s = (it * rb + lax.broadcasted_iota(jnp.int32, (1, rb), 1)).astype(F32)
        pt = jnp.where((l0_ref[...] == s) | (l1_ref[...] == s), 1.0, 0.0).astype(BF16)
        acc_ref[...] += _dot(pt, y_ref[...])

    @pl.when(it == ni - 1)
    def _():
        out_ref[...] = x_ref[...] + mod_ref[pl.ds(row, 1), :] * acc_ref[...]


def _moe_scatter(l0, l1, ys, x, mod, gate_off, tab, *, tm, tn, seg_fn):
    t, d = x.shape
    rb, ni = MOE_RB, tab['ni']
    tn = _tile(d, tn)
    gblk = gate_off // tn
    lspec = pl.BlockSpec((tm, LANES), lambda i, j, it, ib, nit: (i, 0))
    grid_spec = pltpu.PrefetchScalarGridSpec(
        num_scalar_prefetch=2, grid=(t // tm, d // tn, ni),
        in_specs=[lspec, lspec,
                  pl.BlockSpec((rb, tn), lambda i, j, it, ib, nit: (ib[i * ni + it], j)),
                  pl.BlockSpec((tm, tn), lambda i, j, it, ib, nit: (i, j)),
                  pl.BlockSpec((MOD_ROWS, tn), lambda i, j, it, ib, nit: (0, gblk + j))],
        out_specs=pl.BlockSpec((tm, tn), lambda i, j, it, ib, nit: (i, j)),
        scratch_shapes=[pltpu.VMEM((tm, tn), F32)])
    return pl.pallas_call(
        functools.partial(_moe_scatter_kernel, rb=rb, ni=ni, seg=seg_fn(tm)), grid_spec=grid_spec,
        out_shape=jax.ShapeDtypeStruct((t, d), F32),
        compiler_params=_cparams(("parallel", "parallel", "arbitrary")), name="moe_scatter",
    )(tab['iblk'], tab['n_items'], l0, l1, ys, x, mod)


def kernel(x, c, ctx, c_ctx, mod_w, mod_b, norm1_w, norm2_w, in_w, conv_w, conv_b, hgrn_lb, hgrn_norm_w,
           mlstm_igate_b, mlstm_fgate_b, mlstm_norm_w, ssd_a_log, ssd_dt_bias, ssd_d, ssd_norm_w,
           gate_w, gate_b, branch_w, out_w, ffn_w1, ffn_w3, ffn_w2, router_w, moe_w1, moe_w3, moe_w2,
           final_norm_w):
    nb, seq, d = x.shape
    ctx_len = ctx.shape[1]
    depth = mod_w.shape[0]
    bw = d // 2
    hg_heads = bw // HGRN_DK
    ml_dv = bw // MLSTM_HEADS
    ml_dk = ml_dv // 2
    ss_heads = bw // SSD_HEADDIM
    rank = gate_w.shape[1]
    n_ml_qk = MLSTM_HEADS * ml_dk
    n_ss_bc = SSD_GROUPS * SSD_STATE
    off = {}
    pos = 0
    for nm, sz in (('ml_q', n_ml_qk), ('ml_k', n_ml_qk), ('ss_x', bw), ('ss_B', n_ss_bc), ('ss_C', n_ss_bc),
                   ('hg_q', bw), ('hg_f_fwd', bw), ('hg_f_bwd', bw), ('hg_i', bw), ('hg_g', bw),
                   ('ml_v', bw), ('ml_z', bw), ('ml_gates', 4 * MLSTM_HEADS), ('ss_z', bw),
                   ('ss_dt', 2 * ss_heads), ('merge', rank)):
        off[nm] = (pos, sz)
        pos += sz
    n_conv = off['hg_q'][0]
    n_main = off['ml_gates'][0]
    n_small = 4 * MLSTM_HEADS + 2 * ss_heads
    assert n_small <= LANES and ctx_len % GRID_W == 0 and seq % ctx_len == 0
    t_off_z, t_off_merge, t_off_small = 0, bw, bw + rank
    n_tail_raw = bw + rank + LANES
    n_tail = -(-n_tail_raw // 512) * 512

    tb = ctx_len
    ncb, nlb = 1, seq // tb
    n_ctx_rows = nb * ctx_len
    t = n_ctx_rows + nb * seq
    ctx_row = nb

    def seg_fn(tile):
        assert n_ctx_rows % tile == 0 and seq % tile == 0
        return (n_ctx_rows // tile, seq // tile, ctx_row)

    tm_big = _tile(math.gcd(n_ctx_rows, seq), 1024)

    xs = jnp.concatenate([ctx.reshape(n_ctx_rows, d), x.reshape(nb * seq, d)], axis=0)
    c_all = jnp.zeros((MOD_ROWS, d), F32).at[:nb].set(c).at[ctx_row].set(c_ctx)

    lb_cum = jnp.cumsum(jax.nn.softmax(hgrn_lb.astype(F32), axis=0), axis=0)
    lower_bounds = lb_cum - lb_cum[0]

    for l in range(depth):
        mod = _mm(c_all, mod_w, w_lead=l, out_dtype=F32, tm=MOD_ROWS, tn=1024, tk=2048,
                  a_silu=True, bias=mod_b[l][None, :], name="mod")
        h = _norm_mod(xs, norm1_w[l][None, :], mod, sh_off=0, sc_off=d, seg_fn=seg_fn, tr=tb, name="norm1")
        w_main = in_w[l, :, :n_main].astype(BF16)
        wl = in_w[l]
        w_tail = jnp.concatenate(
            [wl[:, off['ss_z'][0]:off['ss_z'][0] + bw],
             wl[:, off['merge'][0]:off['merge'][0] + rank],
             wl[:, off['ml_gates'][0]:off['ml_gates'][0] + 4 * MLSTM_HEADS],
             wl[:, off['ss_dt'][0]:off['ss_dt'][0] + 2 * ss_heads],
             jnp.zeros((d, n_tail - n_tail_raw + LANES - n_small), F32)], axis=1).astype(BF16)
        p_main = _mm(h, w_main, out_dtype=F32, tm=tm_big, tn=2048, tk=512, name="in_proj_main")
        p_tail = _mm(h, w_tail, out_dtype=F32, tm=tm_big, tn=n_tail, tk=512, name="in_proj_tail")
        conv = _conv(p_main, conv_w[l].reshape(9, n_conv), conv_b[l][None, :], n_conv=n_conv, tc=tb,
                     ctx_len=ctx_len, n_ctx_tiles=n_ctx_rows // tb, tiles_per_img=seq // tb, cb=512)

        hgf, hgb = _hgrn_scan(p_main, lower_bounds[l][None, :], nb=nb, ncb=ncb, nlb=nlb, tb=tb,
                              heads=hg_heads, off_q=off['hg_q'][0], off_ff=off['hg_f_fwd'][0],
                              off_fb=off['hg_f_bwd'][0], off_i=off['hg_i'][0])
        ml_bias = jnp.zeros((1, LANES), F32).at[0, :4 * MLSTM_HEADS].set(
            jnp.stack([mlstm_igate_b[l, 0], mlstm_fgate_b[l, 0],
                       mlstm_igate_b[l, 1], mlstm_fgate_b[l, 1]]).reshape(-1))
        mlf, mlb = _mlstm_scan(conv, p_main, p_tail, ml_bias, nb=nb, ncb=ncb, nlb=nlb, tb=tb,
                               dk=ml_dk, dv=ml_dv, off_q=off['ml_q'][0], off_k=off['ml_k'][0],
                               off_v=off['ml_v'][0], off_small=t_off_small)
        lo = 4 * MLSTM_HEADS
        dtb_row = jnp.zeros((1, LANES), F32).at[0, lo:lo + 2 * ss_heads].set(ssd_dt_bias[l].reshape(-1))
        alog_row = jnp.zeros((1, LANES), F32).at[0, lo:lo + 2 * ss_heads].set(ssd_a_log[l].reshape(-1))
        ssf, ssb = _ssd_scan(conv, p_tail, dtb_row, alog_row, nb=nb, ncb=ncb, nlb=nlb, tb=tb,
                             heads=ss_heads, off_x=off['ss_x'][0], off_b=off['ss_B'][0],
                             off_c=off['ss_C'][0], off_small=t_off_small)

        y_hg = _headnorm(hgf, hgb, p_main, off['hg_g'][0], hgrn_norm_w[l][None, :], hd=HGRN_DK,
                         tr=tb, cb=512, name="hgrn_finish")
        y_ml = _headnorm(mlf, mlb, p_main, off['ml_z'][0], mlstm_norm_w[l][None, :], hd=ml_dv,
                         tr=tb, cb=512, name="mlstm_finish")
        d_row = jnp.repeat(ssd_d[l], SSD_HEADDIM)[None, :]
        y_ss = _ssd_finish(ssf, ssb, conv, off['ss_x'][0], p_tail, t_off_z, d_row,
                           ssd_norm_w[l][None, :], tr=tb)
        ym = _merge((y_hg, y_ml, y_ss), branch_w[l].astype(BF16), p_tail, t_off_merge, rank,
                    gate_w[l].astype(BF16), gate_b[l][None, :], tm=tm_big, tn=1024)
        xs = _mm(ym, out_w[l].astype(BF16), out_dtype=F32, tm=tm_big, tn=1024, tk=1024,
                 resid=(xs, mod, 2 * d, seg_fn), name="out_proj")

        i = l // 2
        if l % 2 == 0:
            h2 = _norm_mod(xs, norm2_w[l][None, :], mod, sh_off=3 * d, sc_off=4 * d, seg_fn=seg_fn,
                           tr=tb, name="norm2")
            mid = _swiglu_up(h2, ffn_w1[i].astype(BF16), ffn_w3[i].astype(BF16),
                             tm=tm_big, tn=1024, tk=1024, name="ffn_up")
            xs = _mm(mid, ffn_w2[i].astype(BF16), out_dtype=F32, tm=tm_big, tn=1024, tk=1024,
                     resid=(xs, mod, 5 * d, seg_fn), name="ffn_down")
        else:
            rw = jnp.zeros((d, LANES), F32).at[:, :N_EXPERTS].set(router_w[i])
            h2, comb, cnt = _norm_mod(xs, norm2_w[l][None, :], mod, sh_off=3 * d, sc_off=4 * d,
                                      seg_fn=seg_fn, tr=tb, router_w=rw, name="norm2_router")
            cnt_tile = cnt.reshape(t // tm_big, tm_big // tb, SUBLANES, LANES)[:, :, 0, :N_EXPERTS]
            tab = _moe_tables(jnp.sum(cnt_tile, axis=1).astype(jnp.int32), tm_big)
            a_s, w_s, l0, l1 = _moe_gather(h2, comb, tab, tm=tm_big)
            mid = _moe_grouped(a_s, (moe_w1[i].astype(BF16), moe_w3[i].astype(BF16)), w_s, tab,
                               tn=1024, tk=1024, name="moe_up")
            ys = _moe_grouped(mid, (moe_w2[i].astype(BF16),), None, tab, tn=1024, tk=1024, name="moe_down")
            xs = _moe_scatter(l0, l1, ys, xs, mod, 5 * d, tab, tm=tm_big, tn=1024, seg_fn=seg_fn)

    out = _final_norm(xs, final_norm_w[None, :], row0=n_ctx_rows, rows=nb * seq, tr=tb)
    return out.reshape(nb, seq, d)
```

```python
import functools
import math

import jax
import jax.numpy as jnp
from jax import lax
from jax.experimental import pallas as pl
from jax.experimental.pallas import tpu as pltpu

F32 = jnp.float32
BF16 = jnp.bfloat16

GRID_W = 64
EPS = 1e-6
HGRN_DK = 128
MLSTM_HEADS = 8
SSD_HEADDIM = 64
SSD_GROUPS = 8
SSD_STATE = 128
N_EXPERTS = 8
LANES = 128
SUBLANES = 8
LOG2E = 1.4426950408889634
MOD_ROWS = 8
VMEM_LIMIT = 56 * 1024 * 1024


def _cparams(sem):
    return pltpu.CompilerParams(dimension_semantics=sem, vmem_limit_bytes=VMEM_LIMIT)


def _tile(n, pref):
    t = min(n, pref)
    while n % t:
        t //= 2
    return t


def _sigmoid(x):
    return 1.0 / (1.0 + jnp.exp(-x))


def _silu(x):
    return x * _sigmoid(x)


def _log_sigmoid(x):
    return jnp.minimum(x, 0.0) - jnp.log1p(jnp.exp(-jnp.abs(x)))


def _softplus(x):
    return jnp.maximum(x, 0.0) + jnp.log1p(jnp.exp(-jnp.abs(x)))


def _dot(a, b):
    return jnp.dot(a, b, preferred_element_type=F32)


def _dot_nt(a, b):
    return lax.dot_general(a, b, (((1,), (1,)), ((), ())), preferred_element_type=F32)


def _split3(x):
    x1 = x.astype(BF16)
    r = x - x1.astype(F32)
    x2 = r.astype(BF16)
    x3 = (r - x2.astype(F32)).astype(BF16)
    return x1, x2, x3


def _dot_exact(m01, x):
    x1, x2, x3 = _split3(x)
    return _dot(m01, x1) + _dot(m01, x2) + _dot(m01, x3)


def _mod_row(tile, n_ctx_tiles, tiles_per_batch, ctx_row):
    return jnp.where(tile < n_ctx_tiles, ctx_row, (tile - n_ctx_tiles) // tiles_per_batch)


def _accumulate(k, nk, part, acc_refs, finish):
    if nk == 1:
        finish(*part())
        return

    @pl.when(k == 0)
    def _():
        for acc, p in zip(acc_refs, part()):
            acc[...] = p

    @pl.when((k > 0) & (k < nk - 1))
    def _():
        for acc, p in zip(acc_refs, part()):
            acc[...] += p

    @pl.when(k == nk - 1)
    def _():
        finish(*[acc[...] + p for acc, p in zip(acc_refs, part())])


def _mm_kernel(*refs, nk, a_silu, has_bias, resid, seg):
    it = iter(refs)
    a_ref, w_ref = next(it), next(it)
    bias_ref = next(it) if has_bias else None
    x_ref = next(it) if resid else None
    mod_ref = next(it) if resid else None
    out_ref = next(it)
    acc_ref = next(it) if nk > 1 else None
    k = pl.program_id(2)
    row = _mod_row(pl.program_id(0), *seg) if resid else None

    def part():
        a = a_ref[...]
        if a_silu:
            a = _silu(a.astype(F32))
        return (_dot(a.astype(BF16), w_ref[...].astype(BF16)),)

    def finish(r):
        if has_bias:
            r = r + bias_ref[...]
        if resid:
            r = x_ref[...] + mod_ref[pl.ds(row, 1), :] * r
        out_ref[...] = r.astype(out_ref.dtype)

    _accumulate(k, nk, part, (acc_ref,), finish)


def _mm(a, w, *, out_dtype, tm, tn, tk, w_lead=None, a_silu=False, bias=None,
        resid=None, name):
    m, kdim = a.shape
    n = w.shape[-1]
    tm, tn, tk = _tile(m, tm), _tile(n, tn), _tile(kdim, tk)
    nk = kdim // tk
    if w_lead is None:
        w_spec = pl.BlockSpec((tk, tn), lambda i, j, k: (k, j))
    else:
        w_spec = pl.BlockSpec((None, tk, tn), lambda i, j, k: (w_lead, k, j))
    in_specs = [pl.BlockSpec((tm, tk), lambda i, j, k: (i, k)), w_spec]
    args = [a, w]
    if bias is not None:
        in_specs.append(pl.BlockSpec((1, tn), lambda i, j, k: (0, j)))
        args.append(bias)
    seg = None
    if resid is not None:
        x, mod, gate_off, seg_fn = resid
        seg = seg_fn(tm)
        gblk = gate_off // tn
        in_specs.append(pl.BlockSpec((tm, tn), lambda i, j, k: (i, j)))
        in_specs.append(pl.BlockSpec((MOD_ROWS, tn), lambda i, j, k: (0, gblk + j)))
        args += [x, mod]
    return pl.pallas_call(
        functools.partial(_mm_kernel, nk=nk, a_silu=a_silu, has_bias=bias is not None,
                          resid=resid is not None, seg=seg),
        grid=(m // tm, n // tn, nk),
        in_specs=in_specs,
        out_specs=pl.BlockSpec((tm, tn), lambda i, j, k: (i, j)),
        out_shape=jax.ShapeDtypeStruct((m, n), out_dtype),
        scratch_shapes=[pltpu.VMEM((tm, tn), F32)] if nk > 1 else [],
        compiler_params=_cparams(("parallel", "parallel", "arbitrary")),
        name=name,
    )(*args)


def _norm_mod_kernel(*refs, d, sh_off, sc_off, seg, router):
    if router:
        x_ref, nw_ref, mod_ref, rw_ref, out_ref, comb_ref, cnt_ref = refs
    else:
        x_ref, nw_ref, mod_ref, out_ref = refs
    row = _mod_row(pl.program_id(0), *seg)
    x = x_ref[...]
    y = x * lax.rsqrt(jnp.mean(x * x, axis=-1, keepdims=True) + EPS) * nw_ref[...]
    sc = mod_ref[pl.ds(row, 1), sc_off:sc_off + d]
    sh = mod_ref[pl.ds(row, 1), sh_off:sh_off + d]
    h = y * (1.0 + sc) + sh
    out_ref[...] = h.astype(BF16)
    if router:
        h1, h2, h3 = _split3(h)
        r1, r2, r3 = _split3(rw_ref[...])
        logits = (_dot(h1, r1) + _dot(h1, r2) + _dot(h2, r1)
                  + _dot(h2, r2) + _dot(h1, r3) + _dot(h3, r1))
        lane = lax.broadcasted_iota(jnp.int32, logits.shape, 1).astype(F32)
        valid = lane < N_EXPERTS
        logits = jnp.where(valid, logits, -jnp.inf)
        mx = jnp.max(logits, axis=-1, keepdims=True)
        e = jnp.exp(logits - mx)
        probs = e / jnp.sum(e, axis=-1, keepdims=True)
        p1 = jnp.max(probs, axis=-1, keepdims=True)
        i1 = jnp.min(jnp.where((probs == p1) & valid, lane, float(LANES)), axis=-1, keepdims=True)
        rest = jnp.where((lane == i1) | (lane >= N_EXPERTS), -1.0, probs)
        p2 = jnp.max(rest, axis=-1, keepdims=True)
        i2 = jnp.min(jnp.where(rest == p2, lane, float(LANES)), axis=-1, keepdims=True)
        tot = p1 + p2
        comb = jnp.where(lane == i1, p1 / tot, jnp.where(lane == i2, p2 / tot, 0.0))
        comb_ref[...] = comb
        cnt = jnp.sum(jnp.where(comb > 0.0, 1.0, 0.0), axis=0, keepdims=True)
        cnt_ref[...] = jnp.broadcast_to(cnt, cnt_ref.shape)


def _norm_mod(x, nw, mod, *, sh_off, sc_off, seg_fn, tr, router_w=None, name):
    t, d = x.shape
    tr = _tile(t, tr)
    router = router_w is not None
    in_specs = [pl.BlockSpec((tr, d), lambda i: (i, 0)),
                pl.BlockSpec((1, d), lambda i: (0, 0)),
                pl.BlockSpec(mod.shape, lambda i: (0, 0))]
    args = [x, nw, mod]
    out_specs = [pl.BlockSpec((tr, d), lambda i: (i, 0))]
    out_shape = [jax.ShapeDtypeStruct((t, d), BF16)]
    if router:
        in_specs.append(pl.BlockSpec(router_w.shape, lambda i: (0, 0)))
        args.append(router_w)
        out_specs.append(pl.BlockSpec((tr, LANES), lambda i: (i, 0)))
        out_shape.append(jax.ShapeDtypeStruct((t, LANES), F32))
        out_specs.append(pl.BlockSpec((SUBLANES, LANES), lambda i: (i, 0)))
        out_shape.append(jax.ShapeDtypeStruct((t // tr * SUBLANES, LANES), F32))
    res = pl.pallas_call(
        functools.partial(_norm_mod_kernel, d=d, sh_off=sh_off, sc_off=sc_off, seg=seg_fn(tr),
                          router=router),
        grid=(t // tr,), in_specs=in_specs, out_specs=out_specs, out_shape=out_shape,
        compiler_params=_cparams(("parallel",)), name=name,
    )(*args)
    return res if router else res[0]


def _final_norm_kernel(x_ref, w_ref, out_ref):
    x = x_ref[...]
    out_ref[...] = x * lax.rsqrt(jnp.mean(x * x, axis=-1, keepdims=True) + EPS) * w_ref[...]


def _final_norm(x, w, *, row0, rows, tr):
    d = x.shape[1]
    tr = _tile(math.gcd(row0, rows), tr)
    off = row0 // tr
    return pl.pallas_call(
        _final_norm_kernel, grid=(rows // tr,),
        in_specs=[pl.BlockSpec((tr, d), lambda i: (i + off, 0)),
                  pl.BlockSpec((1, d), lambda i: (0, 0))],
        out_specs=pl.BlockSpec((tr, d), lambda i: (i, 0)),
        out_shape=jax.ShapeDtypeStruct((rows, d), F32),
        compiler_params=_cparams(("parallel",)), name="final_norm",
    )(x, w)


def _conv_kernel(main_ref, prev_ref, next_ref, w_ref, b_ref, out_ref, *,
                 tc, ctx_len, n_ctx_tiles, tiles_per_img):
    i = pl.program_id(0)
    is_ctx = i < n_ctx_tiles
    li = i - n_ctx_tiles
    first = (li % tiles_per_img) == 0
    last = (li % tiles_per_img) == tiles_per_img - 1
    main = main_ref[...]
    prev = jnp.where(is_ctx | first, 0.0, prev_ref[...])
    nxt = jnp.where(is_ctx | last, 0.0, next_ref[...])
    z = jnp.concatenate([prev, main, nxt], axis=0)
    n = tc + 2 * GRID_W
    pos = lax.broadcasted_iota(jnp.int32, (n, 1), 0)
    col_lat = pos % GRID_W
    col_ctx = (pos + (ctx_len - GRID_W)) % ctx_len
    col = jnp.where(is_ctx, col_ctx, col_lat)
    width = jnp.where(is_ctx, ctx_len, GRID_W)
    zl = jnp.where(col != 0, pltpu.roll(z, 1, 0), 0.0)
    zr = jnp.where(col != width - 1, pltpu.roll(z, n - 1, 0), 0.0)
    taps = (zl, z, zr)
    w = w_ref[...]
    acc = b_ref[...] + jnp.zeros((tc, main.shape[1]), F32)
    for dr in range(3):
        for dc in range(3):
            wt = w[3 * dr + dc:3 * dr + dc + 1, :]
            if dr != 1:
                wt = jnp.where(is_ctx, 0.0, wt)
            acc = acc + wt * taps[dc][GRID_W * dr:GRID_W * dr + tc, :]
    out_ref[...] = _silu(acc)


def _conv(p_main, conv_w9, conv_b, *, n_conv, tc, ctx_len, n_ctx_tiles, tiles_per_img, cb):
    t = p_main.shape[0]
    cb = _tile(n_conv, cb)
    rpt = tc // GRID_W
    nrow = t // GRID_W
    return pl.pallas_call(
        functools.partial(_conv_kernel, tc=tc, ctx_len=ctx_len, n_ctx_tiles=n_ctx_tiles,
                          tiles_per_img=tiles_per_img),
        grid=(t // tc, n_conv // cb),
        in_specs=[pl.BlockSpec((tc, cb), lambda i, j: (i, j)),
                  pl.BlockSpec((GRID_W, cb), lambda i, j: (jnp.maximum(i * rpt - 1, 0), j)),
                  pl.BlockSpec((GRID_W, cb), lambda i, j: (jnp.minimum((i + 1) * rpt, nrow - 1), j)),
                  pl.BlockSpec((9, cb), lambda i, j: (0, j)),
                  pl.BlockSpec((1, cb), lambda i, j: (0, j))],
        out_specs=pl.BlockSpec((tc, cb), lambda i, j: (i, j)),
        out_shape=jax.ShapeDtypeStruct((t, n_conv), F32),
        compiler_params=_cparams(("parallel", "parallel")), name="conv_silu",
    )(p_main, p_main, p_main, conv_w9, conv_b)


def _rowblk(b, s, rev, ncb, nlb, nb):
    if rev:
        ctx = b * ncb + (ncb - 1 - s)
        lat = nb * ncb + b * nlb + (nlb - 1 - (s - ncb))
    else:
        ctx = b * ncb + s
        lat = nb * ncb + b * nlb + (s - ncb)
    return jnp.where(s < ncb, ctx, lat)


def _tri(n, rev):
    r = lax.broadcasted_iota(jnp.int32, (n, n), 0)
    c = lax.broadcasted_iota(jnp.int32, (n, n), 1)
    return (c >= r) if rev else (c <= r)


def _sel_col(x, c):
    lane = lax.broadcasted_iota(jnp.int32, x.shape, 1)
    return jnp.sum(jnp.where(lane == c, x, 0.0), axis=1, keepdims=True)


def _sel_row(xt, c):
    sub = lax.broadcasted_iota(jnp.int32, xt.shape, 0)
    return jnp.sum(jnp.where(sub == c, xt, 0.0), axis=0, keepdims=True)


def _level_ref(bl, half, rev):
    tb, dk = bl.shape
    blk = 2 * half
    idx = half if rev else half - 1
    if blk >= SUBLANES:
        b3 = bl.reshape(tb // blk, blk, dk)
        return jnp.broadcast_to(b3[:, idx:idx + 1, :], b3.shape).reshape(tb, dk)
    tmod = lax.broadcasted_iota(jnp.int32, (tb, 1), 0) % blk
    r = bl
    for m in range(blk):
        if idx != m:
            r = jnp.where(tmod == m, pltpu.roll(bl, (m - idx) % tb, 0), r)
    return r


def _hgrn_dir(q_ref, u_ref, v_ref, lb_ref, o_ref, st_ref, d, rev, tb):
    dk = HGRN_DK
    lbv = lb_ref[...]
    qraw = q_ref[...]
    u = u_ref[...]
    v = v_ref[...]
    q = _silu(qraw) * dk ** -0.5
    la = jnp.log(lbv)
    lc = jnp.log1p(-lbv) + _log_sigmoid(u)
    mx = jnp.maximum(la, lc)
    mn = jnp.minimum(la, lc)
    logf = mx + jnp.log1p(jnp.exp(mn - mx))
    k = (1.0 - lbv) * _sigmoid(-u)

    ri = lax.broadcasted_iota(jnp.int32, (tb, tb), 0)
    cj = lax.broadcasted_iota(jnp.int32, (tb, tb), 1)
    keep = (cj >= ri) if rev else (cj <= ri)
    bl = _dot_exact(jnp.where(keep, 1.0, 0.0).astype(BF16), logf * LOG2E)
    b_end = bl[0:1] if rev else bl[tb - 1:tb]
    level = jnp.where((cj > ri) if rev else (cj < ri), 31 - lax.clz(ri ^ cj), -1)
    scores = jnp.zeros((tb, tb), F32)
    for lv in range(tb.bit_length() - 1):
        dq = bl - _level_ref(bl, 1 << lv, rev)
        qt = (q * jnp.exp2(dq)).astype(BF16)
        kt = (k * jnp.exp2(-dq)).astype(BF16)
        scores = jnp.where(level == lv, _dot_nt(qt, kt), scores)
    vb = v.astype(BF16)
    st = st_ref[d]
    o_ref[...] = (_dot(scores.astype(BF16), vb) + jnp.sum(q * k, axis=-1, keepdims=True) * v
                  + _dot_nt((q * jnp.exp2(bl)).astype(BF16), st.astype(BF16)))
    st_ref[d] = st * jnp.exp2(b_end) + _dot(v.T.astype(BF16), (k * jnp.exp2(b_end - bl)).astype(BF16))


def _hgrn_kernel(qf, uf, vf, qb, ub, vb, lb_ref, of, ob, st_ref, *, tb):
    @pl.when(pl.program_id(2) == 0)
    def _():
        st_ref[...] = jnp.zeros_like(st_ref)
    _hgrn_dir(qf, uf, vf, lb_ref, of, st_ref, 0, False, tb)
    _hgrn_dir(qb, ub, vb, lb_ref, ob, st_ref, 1, True, tb)


def _hgrn_scan(p_main, lb_row, *, nb, ncb, nlb, tb, heads, off_q, off_ff, off_fb, off_i):
    t = p_main.shape[0]
    dk = HGRN_DK

    def spec(off, rev):
        return pl.BlockSpec((tb, dk), lambda b, h, s: (_rowblk(b, s, rev, ncb, nlb, nb), off // dk + h))

    def ospec(rev):
        return pl.BlockSpec((tb, dk), lambda b, h, s: (_rowblk(b, s, rev, ncb, nlb, nb), h))

    out = jax.ShapeDtypeStruct((t, heads * dk), F32)
    return pl.pallas_call(
        functools.partial(_hgrn_kernel, tb=tb),
        grid=(nb, heads, ncb + nlb),
        in_specs=[spec(off_q, False), spec(off_ff, False), spec(off_i, False),
                  spec(off_q, True), spec(off_fb, True), spec(off_i, True),
                  pl.BlockSpec((1, dk), lambda b, h, s: (0, h))],
        out_specs=[ospec(False), ospec(True)],
        out_shape=[out, out],
        scratch_shapes=[pltpu.VMEM((2, dk, dk), F32)],
        compiler_params=_cparams(("parallel", "parallel", "arbitrary")), name="hgrn_scan",
    )(p_main, p_main, p_main, p_main, p_main, p_main, lb_row)


def _mlstm_dir(q_ref, k_ref, v_ref, g_ref, bias_ref, o_ref, c_ref, n_ref, m_ref, d, rev, tb, dk):
    h = pl.program_id(1)
    ci = 2 * MLSTM_HEADS * d + h
    cf = ci + MLSTM_HEADS
    q = q_ref[...]
    ks = k_ref[...] * dk ** -0.5
    v = v_ref[...]
    g = g_ref[...] + bias_ref[...]
    lf = _log_sigmoid(g)
    keep = _tri(tb, rev)
    b = _dot_exact(jnp.where(keep, 1.0, 0.0).astype(BF16), lf)
    b_col, ic_col = _sel_col(b, cf), _sel_col(g, ci)
    b_row, ic_row = _sel_row(b.T, cf), _sel_row(g.T, ci)
    b_end = b_col[0:1] if rev else b_col[tb - 1:tb]
    m_prev = m_ref[d][:, 0:1]
    w_in = jnp.where(keep, b_col - b_row + ic_row, -jnp.inf)
    w_st = b_col + m_prev
    m_row = jnp.maximum(jnp.max(w_in, axis=1, keepdims=True), w_st)
    qb = q.astype(BF16)
    vb = v.astype(BF16)
    p = jnp.exp(w_in - m_row) * _dot_nt(qb, ks.astype(BF16))
    e_st = jnp.exp(w_st - m_row)
    num = _dot(p.astype(BF16), vb) + e_st * _dot(qb, c_ref[d].astype(BF16))
    nrm = (jnp.sum(p, axis=1, keepdims=True)
           + e_st * jnp.sum(q * n_ref[d], axis=1, keepdims=True))
    o_ref[...] = num / jnp.maximum(jnp.abs(nrm), jnp.exp(-m_row))
    w_end = b_end - b_col + ic_col
    m_new = jnp.maximum(b_end + m_prev, jnp.max(w_end, axis=0, keepdims=True))
    s_old = jnp.exp(b_end + m_prev - m_new)
    kt = ks * jnp.exp(w_end - m_new)
    c_ref[d] = s_old * c_ref[d] + _dot(kt.T.astype(BF16), vb)
    n_ref[d] = s_old * n_ref[d] + jnp.sum(kt, axis=0, keepdims=True)
    m_ref[d] = jnp.broadcast_to(m_new, m_ref.shape[1:])


def _mlstm_kernel(qf, kf, vf, gf, qb, kb, vb, gb, bias_ref, of, ob, c_ref, n_ref, m_ref, *, tb, dk):
    @pl.when(pl.program_id(2) == 0)
    def _():
        c_ref[...] = jnp.zeros_like(c_ref)
        n_ref[...] = jnp.zeros_like(n_ref)
        m_ref[...] = jnp.zeros_like(m_ref)
    _mlstm_dir(qf, kf, vf, gf, bias_ref, of, c_ref, n_ref, m_ref, 0, False, tb, dk)
    _mlstm_dir(qb, kb, vb, gb, bias_ref, ob, c_ref, n_ref, m_ref, 1, True, tb, dk)


def _mlstm_scan(conv, p_main, p_tail, bias_row, *, nb, ncb, nlb, tb, dk, dv,
                off_q, off_k, off_v, off_small):
    t = conv.shape[0]

    def spec(width, off, rev):
        return pl.BlockSpec((tb, width),
                            lambda b, h, s: (_rowblk(b, s, rev, ncb, nlb, nb), off // width + h))

    def gspec(rev):
        return pl.BlockSpec((tb, LANES),
                            lambda b, h, s: (_rowblk(b, s, rev, ncb, nlb, nb), off_small // LANES))

    def ospec(rev):
        return pl.BlockSpec((tb, dv), lambda b, h, s: (_rowblk(b, s, rev, ncb, nlb, nb), h))

    out = jax.ShapeDtypeStruct((t, MLSTM_HEADS * dv), F32)
    return pl.pallas_call(
        functools.partial(_mlstm_kernel, tb=tb, dk=dk),
        grid=(nb, MLSTM_HEADS, ncb + nlb),
        in_specs=[spec(dk, off_q, False), spec(dk, off_k, False), spec(dv, off_v, False), gspec(False),
                  spec(dk, off_q, True), spec(dk, off_k, True), spec(dv, off_v, True), gspec(True),
                  pl.BlockSpec((1, LANES), lambda b, h, s: (0, 0))],
        out_specs=[ospec(False), ospec(True)],
        out_shape=[out, out],
        scratch_shapes=[pltpu.VMEM((2, dk, dv), F32), pltpu.VMEM((2, 1, dk), F32),
                        pltpu.VMEM((2, 1, LANES), F32)],
        compiler_params=_cparams(("parallel", "parallel", "arbitrary")), name="mlstm_scan",
    )(conv, conv, p_main, p_tail, conv, conv, p_main, p_tail, bias_row)


def _ssd_dir(c_ref, bm_ref, x_ref, g_ref, dtb_ref, alog_ref, o_ref, s_ref, d, rev, tb, heads, hpg):
    grp = pl.program_id(1)
    lane0 = MLSTM_HEADS * 4 + heads * d + hpg * grp
    cm = c_ref[...].astype(BF16)
    bm = bm_ref[...]
    x = x_ref[...]
    dt = _softplus(g_ref[...] + dtb_ref[...])
    la = -jnp.exp(alog_ref[...]) * dt
    keep = _tri(tb, rev)
    b = _dot_exact(jnp.where(keep, 1.0, 0.0).astype(BF16), la * LOG2E)
    bt = b.T
    width = hpg * SSD_HEADDIM
    hl = lax.broadcasted_iota(jnp.int32, (1, width), 1) // SSD_HEADDIM
    b_cols = [_sel_col(b, lane0 + i) for i in range(hpg)]
    b_all = jnp.zeros((tb, width), F32)
    dt_all = jnp.zeros((tb, width), F32)
    for i in range(hpg):
        b_all = jnp.where(hl == i, b_cols[i], b_all)
        dt_all = jnp.where(hl == i, _sel_col(dt, lane0 + i), dt_all)
    b_end = b_all[0:1] if rev else b_all[tb - 1:tb]
    v_all = x * dt_all
    v_bf = v_all.astype(BF16)
    gmat = _dot_nt(cm, bm.astype(BF16))
    s_old = s_ref[d]
    inter = jnp.exp2(b_all) * _dot(cm, s_old.astype(BF16))
    outs = []
    for i in range(hpg):
        dec = jnp.where(keep, jnp.exp2(b_cols[i] - _sel_row(bt, lane0 + i)), 0.0)
        outs.append(_dot((gmat * dec).astype(BF16), v_bf[:, i * SSD_HEADDIM:(i + 1) * SSD_HEADDIM]))
    o_ref[...] = jnp.concatenate(outs, axis=1) + inter
    sv = (v_all * jnp.exp2(b_end - b_all)).astype(BF16)
    s_ref[d] = jnp.exp2(b_end) * s_old + _dot(bm.T.astype(BF16), sv)


def _ssd_kernel(cf, bf, xf, gf, cb, bb, xb, gb, dtb_ref, alog_ref, of, ob, s_ref, *, tb, heads, hpg):
    @pl.when(pl.program_id(2) == 0)
    def _():
        s_ref[...] = jnp.zeros_like(s_ref)
    _ssd_dir(cf, bf, xf, gf, dtb_ref, alog_ref, of, s_ref, 0, False, tb, heads, hpg)
    _ssd_dir(cb, bb, xb, gb, dtb_ref, alog_ref, ob, s_ref, 1, True, tb, heads, hpg)


def _ssd_scan(conv, p_tail, dtb_row, alog_row, *, nb, ncb, nlb, tb, heads, off_x, off_b, off_c, off_small):
    t = conv.shape[0]
    hpg = heads // SSD_GROUPS
    width = hpg * SSD_HEADDIM

    def spec(w, off, rev):
        return pl.BlockSpec((tb, w), lambda b, g, s: (_rowblk(b, s, rev, ncb, nlb, nb), off // w + g))

    def gspec(rev):
        return pl.BlockSpec((tb, LANES),
                            lambda b, g, s: (_rowblk(b, s, rev, ncb, nlb, nb), off_small // LANES))

    def ospec(rev):
        return pl.BlockSpec((tb, width), lambda b, g, s: (_rowblk(b, s, rev, ncb, nlb, nb), g))

    row = pl.BlockSpec((1, LANES), lambda b, g, s: (0, 0))
    out = jax.ShapeDtypeStruct((t, heads * SSD_HEADDIM), F32)
    return pl.pallas_call(
        functools.partial(_ssd_kernel, tb=tb, heads=heads, hpg=hpg),
        grid=(nb, SSD_GROUPS, ncb + nlb),
        in_specs=[spec(SSD_STATE, off_c, False), spec(SSD_STATE, off_b, False),
                  spec(width, off_x, False), gspec(False),
                  spec(SSD_STATE, off_c, True), spec(SSD_STATE, off_b, True),
                  spec(width, off_x, True), gspec(True), row, row],
        out_specs=[ospec(False), ospec(True)],
        out_shape=[out, out],
        scratch_shapes=[pltpu.VMEM((2, SSD_STATE, width), F32)],
        compiler_params=_cparams(("parallel", "parallel", "arbitrary")), name="ssd_scan",
    )(conv, conv, conv, p_tail, conv, conv, conv, p_tail, dtb_row, alog_row)


def _headnorm_kernel(of_ref, ob_ref, g_ref, w_ref, out_ref, *, hd):
    cb = of_ref.shape[1]
    for i in range(cb // hd):
        sl = slice(i * hd, (i + 1) * hd)
        o = of_ref[:, sl] + ob_ref[:, sl]
        y = o * lax.rsqrt(jnp.mean(o * o, axis=-1, keepdims=True) + EPS) * w_ref[:, sl]
        out_ref[:, sl] = (y * _silu(g_ref[:, sl])).astype(BF16)


def _headnorm(of, ob, gsrc, goff, w, *, hd, tr, cb, name):
    t, wd = of.shape
    tr, cb = _tile(t, tr), _tile(wd, cb)
    gblk = goff // cb
    return pl.pallas_call(
        functools.partial(_headnorm_kernel, hd=hd), grid=(t // tr, wd // cb),
        in_specs=[pl.BlockSpec((tr, cb), lambda i, j: (i, j)),
                  pl.BlockSpec((tr, cb), lambda i, j: (i, j)),
                  pl.BlockSpec((tr, cb), lambda i, j: (i, gblk + j)),
                  pl.BlockSpec((1, cb), lambda i, j: (0, j))],
        out_specs=pl.BlockSpec((tr, cb), lambda i, j: (i, j)),
        out_shape=jax.ShapeDtypeStruct((t, wd), BF16),
        compiler_params=_cparams(("parallel", "parallel")), name=name,
    )(of, ob, gsrc, w)


def _ssd_finish_kernel(of_ref, ob_ref, x_ref, z_ref, d_ref, w_ref, out_ref):
    y = of_ref[...] + ob_ref[...] + d_ref[...] * x_ref[...]
    tt = y * _silu(z_ref[...])
    out_ref[...] = (tt * lax.rsqrt(jnp.mean(tt * tt, axis=-1, keepdims=True) + EPS)
                    * w_ref[...]).astype(BF16)


def _ssd_finish(of, ob, conv, off_x, p_tail, off_z, d_row, w, *, tr):
    t, wd = of.shape
    tr = _tile(t, tr)
    row = pl.BlockSpec((1, wd), lambda i: (0, 0))
    return pl.pallas_call(
        _ssd_finish_kernel, grid=(t // tr,),
        in_specs=[pl.BlockSpec((tr, wd), lambda i: (i, 0)),
                  pl.BlockSpec((tr, wd), lambda i: (i, 0)),
                  pl.BlockSpec((tr, wd), lambda i: (i, off_x // wd)),
                  pl.BlockSpec((tr, wd), lambda i: (i, off_z // wd)), row, row],
        out_specs=pl.BlockSpec((tr, wd), lambda i: (i, 0)),
        out_shape=jax.ShapeDtypeStruct((t, wd), BF16),
        compiler_params=_cparams(("parallel",)), name="ssd_finish",
    )(of, ob, conv, p_tail, d_row, w)


def _merge_kernel(y0, y1, y2, bw_ref, mg_ref, gw_ref, gb_ref, out_ref, acc_ref):
    k = pl.program_id(2)
    gate = _sigmoid(_dot(mg_ref[...].astype(BF16), gw_ref[...]) + gb_ref[...])
    for idx, y_ref in enumerate((y0, y1, y2)):
        @pl.when(k == idx)
        def _(y_ref=y_ref, idx=idx):
            r = gate * _dot(y_ref[...], bw_ref[...])
            if idx == 0:
                acc_ref[...] = r
            elif idx == 1:
                acc_ref[...] += r
            else:
                out_ref[...] = (acc_ref[...] + r).astype(BF16)


def _merge(ys, bw, p_tail, off_merge, rank, gw, gb, *, tm, tn):
    t, bwid = ys[0].shape
    d = bw.shape[-1]
    tm, tn = _tile(t, tm), _tile(d, tn)
    nj = d // tn
    yspec = pl.BlockSpec((tm, bwid), lambda i, j, k: (i, 0))
    return pl.pallas_call(
        _merge_kernel, grid=(t // tm, nj, 3),
        in_specs=[yspec, yspec, yspec,
                  pl.BlockSpec((None, bwid, tn), lambda i, j, k: (k, 0, j)),
                  pl.BlockSpec((tm, rank), lambda i, j, k: (i, off_merge // rank)),
                  pl.BlockSpec((rank, tn), lambda i, j, k: (0, k * nj + j)),
                  pl.BlockSpec((1, tn), lambda i, j, k: (0, k * nj + j))],
        out_specs=pl.BlockSpec((tm, tn), lambda i, j, k: (i, j)),
        out_shape=jax.ShapeDtypeStruct((t, d), BF16),
        scratch_shapes=[pltpu.VMEM((tm, tn), F32)],
        compiler_params=_cparams(("parallel", "parallel", "arbitrary")), name="branch_merge",
    )(ys[0], ys[1], ys[2], bw, p_tail, gw, gb)


def _swiglu_kernel(a_ref, w1_ref, w3_ref, out_ref, *accs, nk):
    def part():
        a = a_ref[...]
        return _dot(a, w1_ref[...]), _dot(a, w3_ref[...])

    def finish(r1, r3):
        out_ref[...] = (_silu(r1) * r3).astype(BF16)

    _accumulate(pl.program_id(2), nk, part, accs, finish)


def _swiglu_up(a, w1, w3, *, tm, tn, tk, name):
    t, kdim = a.shape
    n = w1.shape[1]
    tm, tn, tk = _tile(t, tm), _tile(n, tn), _tile(kdim, tk)
    nk = kdim // tk
    wspec = pl.BlockSpec((tk, tn), lambda i, j, k: (k, j))
    return pl.pallas_call(
        functools.partial(_swiglu_kernel, nk=nk),
        grid=(t // tm, n // tn, nk),
        in_specs=[pl.BlockSpec((tm, tk), lambda i, j, k: (i, k)), wspec, wspec],
        out_specs=pl.BlockSpec((tm, tn), lambda i, j, k: (i, j)),
        out_shape=jax.ShapeDtypeStruct((t, n), BF16),
        scratch_shapes=[pltpu.VMEM((tm, tn), F32)] * (2 if nk > 1 else 0),
        compiler_params=_cparams(("parallel", "parallel", "arbitrary")), name=name,
    )(a, w1, w3)


MOE_RB = LANES
MOE_MB = 512


def _moe_tables(cnt_tile, tm):
    nt, ne = cnt_tile.shape
    rb, bpm = MOE_RB, MOE_MB // MOE_RB
    ni = 2 * tm // rb + ne
    nblk = -(-(nt * ni + ne * (bpm - 1)) // bpm) * bpm
    nbk = (cnt_tile + (rb - 1)) // rb
    reg_e = (jnp.sum(nbk, axis=0) + (bpm - 1)) // bpm * bpm
    end_e = jnp.cumsum(reg_e)
    start_e = end_e - reg_e
    pre_ie = jnp.cumsum(nbk, axis=0) - nbk
    ends_ie = jnp.cumsum(nbk, axis=1)
    off_ie = ends_ie - nbk
    n_items = ends_ie[:, -1]
    it = jnp.arange(ni, dtype=jnp.int32)[None, :]
    itc = jnp.minimum(it, n_items[:, None] - 1)
    e_idx = jnp.sum((itc[:, :, None] >= ends_ie[:, None, :]).astype(jnp.int32), axis=-1)
    e_idx = jnp.minimum(e_idx, ne - 1)
    chunk = itc - jnp.take_along_axis(off_ie, e_idx, axis=1)
    gblk = start_e[e_idx] + jnp.take_along_axis(pre_ie, e_idx, axis=1) + chunk
    iout = jnp.where(it < n_items[:, None], gblk, nblk)
    m = jnp.arange(nblk // bpm, dtype=jnp.int32)
    mexp = jnp.sum((m[:, None] * bpm >= end_e[None, :]).astype(jnp.int32), axis=-1)
    off_row = jnp.zeros((nt, 1, LANES), F32).at[:, 0, :ne].set((off_ie * rb).astype(F32))
    return dict(ni=ni, nblk=nblk, iout=iout.reshape(-1).astype(jnp.int32),
                iblk=gblk.reshape(-1).astype(jnp.int32), n_items=n_items.astype(jnp.int32),
                mexp=jnp.minimum(mexp, ne - 1).astype(jnp.int32),
                nvalid=(end_e[-1:] // bpm).astype(jnp.int32), off_row=off_row)


def _moe_gather_kernel(iout_ref, h_ref, comb_ref, off_ref, za_ref, zw_ref,
                       a_ref, w_ref, l0_ref, l1_ref, lt_ref, wt_ref, *, tm, rb):
    del iout_ref, za_ref, zw_ref
    it = pl.program_id(1)

    @pl.when(it == 0)
    def _():
        comb = comb_ref[...]
        pick = comb > 0.0
        r = lax.broadcasted_iota(jnp.int32, (tm, tm), 0)
        c = lax.broadcasted_iota(jnp.int32, (tm, tm), 1)
        rank = _dot(jnp.where(c < r, 1.0, 0.0).astype(BF16), jnp.where(pick, 1.0, 0.0).astype(BF16))
        loc = off_ref[...] + rank
        lane = lax.broadcasted_iota(jnp.int32, comb.shape, 1).astype(F32)
        m1 = jnp.min(jnp.where(pick, lane, float(LANES)), axis=1, keepdims=True)
        first = pick & (lane == m1)
        second = pick & (lane != m1)
        l0 = jnp.sum(jnp.where(first, loc, 0.0), axis=1, keepdims=True)
        l1 = jnp.sum(jnp.where(second, loc + 1.0, 0.0), axis=1, keepdims=True) - 1.0
        l0b = jnp.broadcast_to(l0, comb.shape)
        l1b = jnp.broadcast_to(l1, comb.shape)
        l0_ref[...] = l0b
        l1_ref[...] = l1b
        lt_ref[0] = l0b.T
        lt_ref[1] = l1b.T
        wt_ref[0] = jnp.broadcast_to(jnp.sum(jnp.where(first, comb, 0.0), axis=1, keepdims=True), comb.shape)
        wt_ref[1] = jnp.broadcast_to(jnp.sum(jnp.where(second, comb, 0.0), axis=1, keepdims=True), comb.shape)

    s = (it * rb + lax.broadcasted_iota(jnp.int32, (rb, 1), 0)).astype(F32)
    p0 = jnp.where(lt_ref[0, 0:1, :] == s, 1.0, 0.0).astype(BF16)
    p1 = jnp.where(lt_ref[1, 0:1, :] == s, 1.0, 0.0).astype(BF16)
    a_ref[...] = _dot(p0 + p1, h_ref[...]).astype(BF16)
    w_ref[...] = _dot_exact(p0, wt_ref[0]) + _dot_exact(p1, wt_ref[1])


def _moe_gather(h, comb, tab, *, tm):
    t, d = h.shape
    rb, ni = MOE_RB, tab['ni']
    nrows = tab['nblk'] * rb + MOE_MB
    grid_spec = pltpu.PrefetchScalarGridSpec(
        num_scalar_prefetch=1, grid=(t // tm, ni),
        in_specs=[pl.BlockSpec((tm, d), lambda i, it, io: (i, 0)),
                  pl.BlockSpec((tm, LANES), lambda i, it, io: (i, 0)),
                  pl.BlockSpec((None, 1, LANES), lambda i, it, io: (i, 0, 0)),
                  pl.BlockSpec(memory_space=pl.ANY), pl.BlockSpec(memory_space=pl.ANY)],
        out_specs=[pl.BlockSpec((rb, d), lambda i, it, io: (io[i * ni + it], 0)),
                   pl.BlockSpec((rb, LANES), lambda i, it, io: (io[i * ni + it], 0)),
                   pl.BlockSpec((tm, LANES), lambda i, it, io: (i, 0)),
                   pl.BlockSpec((tm, LANES), lambda i, it, io: (i, 0))],
        scratch_shapes=[pltpu.VMEM((2, LANES, tm), F32), pltpu.VMEM((2, tm, LANES), F32)])
    return pl.pallas_call(
        functools.partial(_moe_gather_kernel, tm=tm, rb=rb), grid_spec=grid_spec,
        out_shape=[jax.ShapeDtypeStruct((nrows, d), BF16), jax.ShapeDtypeStruct((nrows, LANES), F32),
                   jax.ShapeDtypeStruct((t, LANES), F32), jax.ShapeDtypeStruct((t, LANES), F32)],
        input_output_aliases={4: 0, 5: 1},
        compiler_params=_cparams(("parallel", "arbitrary")), name="moe_gather",
    )(tab['iout'], h, comb, tab['off_row'], jnp.zeros((nrows, d), BF16), jnp.zeros((nrows, LANES), F32))


def _moe_grouped_kernel(*refs, nk, dual):
    if dual:
        mexp_ref, nv_ref, a_ref, w1_ref, w3_ref, ws_ref, out_ref, *accs = refs
    else:
        mexp_ref, nv_ref, a_ref, w1_ref, out_ref, *accs = refs
    del mexp_ref
    k = pl.program_id(2)

    def part():
        a = a_ref[...]
        if dual:
            return _dot(a, w1_ref[...]), _dot(a, w3_ref[...])
        return (_dot(a, w1_ref[...]),)

    def finish(r1, r3=None):
        if dual:
            out_ref[...] = (_silu(r1) * r3 * ws_ref[:, 0:1]).astype(BF16)
        else:
            out_ref[...] = r1.astype(BF16)

    @pl.when(pl.program_id(0) < nv_ref[0])
    def _():
        _accumulate(k, nk, part, accs, finish)


def _moe_grouped(a, ws, w_sorted, tab, *, tn, tk, name):
    rows, kdim = a.shape
    n = ws[0].shape[-1]
    tn, tk = _tile(n, tn), _tile(kdim, tk)
    nj, nk = n // tn, kdim // tk
    nmb = tab['nblk'] * MOE_RB // MOE_MB
    dual = len(ws) == 2

    def live(m, nv):
        return m < nv[0]

    def me(m, nv):
        return jnp.minimum(m, nv[0] - 1)

    a_spec = pl.BlockSpec((MOE_MB, tk), lambda m, j, k, ex, nv: (me(m, nv), jnp.where(live(m, nv), k, nk - 1)))
    w_spec = pl.BlockSpec((None, tk, tn), lambda m, j, k, ex, nv: (
        ex[me(m, nv)], jnp.where(live(m, nv), k, nk - 1), jnp.where(live(m, nv), j, nj - 1)))
    o_spec = pl.BlockSpec((MOE_MB, tn), lambda m, j, k, ex, nv: (me(m, nv), jnp.where(live(m, nv), j, nj - 1)))
    in_specs = [a_spec, w_spec]
    args = [a, ws[0]]
    if dual:
        in_specs += [w_spec, pl.BlockSpec((MOE_MB, LANES), lambda m, j, k, ex, nv: (me(m, nv), 0))]
        args += [ws[1], w_sorted]
    scratch = [pltpu.VMEM((MOE_MB, tn), F32)] * (len(ws) if nk > 1 else 0)
    grid_spec = pltpu.PrefetchScalarGridSpec(
        num_scalar_prefetch=2, grid=(nmb, nj, nk), in_specs=in_specs, out_specs=o_spec,
        scratch_shapes=scratch)
    return pl.pallas_call(
        functools.partial(_moe_grouped_kernel, nk=nk, dual=dual), grid_spec=grid_spec,
        out_shape=jax.ShapeDtypeStruct((rows, n), BF16),
        compiler_params=_cparams(("arbitrary", "arbitrary", "arbitrary")), name=name,
    )(tab['mexp'], tab['nvalid'], *args)


def _moe_scatter_kernel(iblk_ref, l0_ref, l1_ref, *rest, rb, ni, seg):
    del iblk_ref
    y_refs = rest[:ni]
    x_ref, mod_ref, out_ref, pt_ref, ya_ref = rest[ni:]
    row = _mod_row(pl.program_id(0), *seg)

    @pl.when(pl.program_id(1) == 0)
    def _():
        l0, l1 = l0_ref[...], l1_ref[...]
        lane = lax.broadcasted_iota(jnp.int32, (1, rb), 1).astype(F32)
        for q in range(ni):
            s = lane + float(q * rb)
            pt_ref[:, q * rb:(q + 1) * rb] = jnp.where((l0 == s) | (l1 == s), 1.0, 0.0).astype(BF16)

    for q in range(ni):
        ya_ref[q * rb:(q + 1) * rb, :] = y_refs[q][...]
    out_ref[...] = x_ref[...] + mod_ref[pl.ds(row, 1), :] * _dot(pt_ref[...], ya_ref[...])


def _moe_scatter(l0, l1, ys, x, mod, gate_off, tab, *, tm, tn, seg_fn):
    t, d = x.shape
    rb, ni = MOE_RB, tab['ni']
    tn = _tile(d, tn)
    gblk = gate_off // tn
    lspec = pl.BlockSpec((tm, LANES), lambda i, j, ib: (i, 0))
    yspecs = [pl.BlockSpec((rb, tn), lambda i, j, ib, q=q: (ib[i * ni + q], j)) for q in range(ni)]
    grid_spec = pltpu.PrefetchScalarGridSpec(
        num_scalar_prefetch=1, grid=(t // tm, d // tn),
        in_specs=[lspec, lspec, *yspecs,
                  pl.BlockSpec((tm, tn), lambda i, j, ib: (i, j)),
                  pl.BlockSpec((MOD_ROWS, tn), lambda i, j, ib: (0, gblk + j))],
        out_specs=pl.BlockSpec((tm, tn), lambda i, j, ib: (i, j)),
        scratch_shapes=[pltpu.VMEM((tm, ni * rb), BF16), pltpu.VMEM((ni * rb, tn), BF16)])
    return pl.pallas_call(
        functools.partial(_moe_scatter_kernel, rb=rb, ni=ni, seg=seg_fn(tm)), grid_spec=grid_spec,
        out_shape=jax.ShapeDtypeStruct((t, d), F32),
        compiler_params=_cparams(("parallel", "arbitrary")), name="moe_scatter",
    )(tab['iblk'], l0, l1, *([ys] * ni), x, mod)


def kernel(x, c, ctx, c_ctx, mod_w, mod_b, norm1_w, norm2_w, in_w, conv_w, conv_b, hgrn_lb, hgrn_norm_w,
           mlstm_igate_b, mlstm_fgate_b, mlstm_norm_w, ssd_a_log, ssd_dt_bias, ssd_d, ssd_norm_w,
           gate_w, gate_b, branch_w, out_w, ffn_w1, ffn_w3, ffn_w2, router_w, moe_w1, moe_w3, moe_w2,
           final_norm_w):
    nb, seq, d = x.shape
    ctx_len = ctx.shape[1]
    depth = mod_w.shape[0]
    bw = d // 2
    hg_heads = bw // HGRN_DK
    ml_dv = bw // MLSTM_HEADS
    ml_dk = ml_dv // 2
    ss_heads = bw // SSD_HEADDIM
    rank = gate_w.shape[1]
    n_ml_qk = MLSTM_HEADS * ml_dk
    n_ss_bc = SSD_GROUPS * SSD_STATE
    off = {}
    pos = 0
    for nm, sz in (('ml_q', n_ml_qk), ('ml_k', n_ml_qk), ('ss_x', bw), ('ss_B', n_ss_bc), ('ss_C', n_ss_bc),
                   ('hg_q', bw), ('hg_f_fwd', bw), ('hg_f_bwd', bw), ('hg_i', bw), ('hg_g', bw),
                   ('ml_v', bw), ('ml_z', bw), ('ml_gates', 4 * MLSTM_HEADS), ('ss_z', bw),
                   ('ss_dt', 2 * ss_heads), ('merge', rank)):
        off[nm] = (pos, sz)
        pos += sz
    n_conv = off['hg_q'][0]
    n_main = off['ml_gates'][0]
    n_small = 4 * MLSTM_HEADS + 2 * ss_heads
    assert n_small <= LANES and ctx_len % GRID_W == 0 and seq % ctx_len == 0
    t_off_z, t_off_merge, t_off_small = 0, bw, bw + rank
    n_tail_raw = bw + rank + LANES
    n_tail = -(-n_tail_raw // 512) * 512

    tb = ctx_len
    ncb, nlb = 1, seq // tb
    n_ctx_rows = nb * ctx_len
    t = n_ctx_rows + nb * seq
    ctx_row = nb

    def seg_fn(tile):
        assert n_ctx_rows % tile == 0 and seq % tile == 0
        return (n_ctx_rows // tile, seq // tile, ctx_row)

    tm_big = _tile(math.gcd(n_ctx_rows, seq), 1024)

    xs = jnp.concatenate([ctx.reshape(n_ctx_rows, d), x.reshape(nb * seq, d)], axis=0)
    c_all = jnp.zeros((MOD_ROWS, d), F32).at[:nb].set(c).at[ctx_row].set(c_ctx)

    lb_cum = jnp.cumsum(jax.nn.softmax(hgrn_lb.astype(F32), axis=0), axis=0)
    lower_bounds = lb_cum - lb_cum[0]

    for l in range(depth):
        mod = _mm(c_all, mod_w, w_lead=l, out_dtype=F32, tm=MOD_ROWS, tn=1024, tk=2048,
                  a_silu=True, bias=mod_b[l][None, :], name="mod")
        h = _norm_mod(xs, norm1_w[l][None, :], mod, sh_off=0, sc_off=d, seg_fn=seg_fn, tr=tb, name="norm1")
        w_main = in_w[l, :, :n_main].astype(BF16)
        wl = in_w[l]
        w_tail = jnp.concatenate(
            [wl[:, off['ss_z'][0]:off['ss_z'][0] + bw],
             wl[:, off['merge'][0]:off['merge'][0] + rank],
             wl[:, off['ml_gates'][0]:off['ml_gates'][0] + 4 * MLSTM_HEADS],
             wl[:, off['ss_dt'][0]:off['ss_dt'][0] + 2 * ss_heads],
             jnp.zeros((d, n_tail - n_tail_raw + LANES - n_small), F32)], axis=1).astype(BF16)
        p_main = _mm(h, w_main, out_dtype=F32, tm=tm_big, tn=1024, tk=d, name="in_proj_main")
        p_tail = _mm(h, w_tail, out_dtype=F32, tm=tm_big, tn=n_tail // 2, tk=2048, name="in_proj_tail")
        conv = _conv(p_main, conv_w[l].reshape(9, n_conv), conv_b[l][None, :], n_conv=n_conv, tc=tb,
                     ctx_len=ctx_len, n_ctx_tiles=n_ctx_rows // tb, tiles_per_img=seq // tb, cb=512)

        hgf, hgb = _hgrn_scan(p_main, lower_bounds[l][None, :], nb=nb, ncb=ncb, nlb=nlb, tb=tb,
                              heads=hg_heads, off_q=off['hg_q'][0], off_ff=off['hg_f_fwd'][0],
                              off_fb=off['hg_f_bwd'][0], off_i=off['hg_i'][0])
        ml_bias = jnp.zeros((1, LANES), F32).at[0, :4 * MLSTM_HEADS].set(
            jnp.stack([mlstm_igate_b[l, 0], mlstm_fgate_b[l, 0],
                       mlstm_igate_b[l, 1], mlstm_fgate_b[l, 1]]).reshape(-1))
        mlf, mlb = _mlstm_scan(conv, p_main, p_tail, ml_bias, nb=nb, ncb=ncb, nlb=nlb, tb=tb,
                               dk=ml_dk, dv=ml_dv, off_q=off['ml_q'][0], off_k=off['ml_k'][0],
                               off_v=off['ml_v'][0], off_small=t_off_small)
        lo = 4 * MLSTM_HEADS
        dtb_row = jnp.zeros((1, LANES), F32).at[0, lo:lo + 2 * ss_heads].set(ssd_dt_bias[l].reshape(-1))
        alog_row = jnp.zeros((1, LANES), F32).at[0, lo:lo + 2 * ss_heads].set(ssd_a_log[l].reshape(-1))
        ssf, ssb = _ssd_scan(conv, p_tail, dtb_row, alog_row, nb=nb, ncb=ncb, nlb=nlb, tb=tb,
                             heads=ss_heads, off_x=off['ss_x'][0], off_b=off['ss_B'][0],
                             off_c=off['ss_C'][0], off_small=t_off_small)

        y_hg = _headnorm(hgf, hgb, p_main, off['hg_g'][0], hgrn_norm_w[l][None, :], hd=HGRN_DK,
                         tr=tb, cb=512, name="hgrn_finish")
        y_ml = _headnorm(mlf, mlb, p_main, off['ml_z'][0], mlstm_norm_w[l][None, :], hd=ml_dv,
                         tr=tb, cb=512, name="mlstm_finish")
        d_row = jnp.repeat(ssd_d[l], SSD_HEADDIM)[None, :]
        y_ss = _ssd_finish(ssf, ssb, conv, off['ss_x'][0], p_tail, t_off_z, d_row,
                           ssd_norm_w[l][None, :], tr=tb)
        ym = _merge((y_hg, y_ml, y_ss), branch_w[l].astype(BF16), p_tail, t_off_merge, rank,
                    gate_w[l].astype(BF16), gate_b[l][None, :], tm=tm_big, tn=1024)
        xs = _mm(ym, out_w[l].astype(BF16), out_dtype=F32, tm=tm_big, tn=1024, tk=d,
                 resid=(xs, mod, 2 * d, seg_fn), name="out_proj")

        i = l // 2
        if l % 2 == 0:
            h2 = _norm_mod(xs, norm2_w[l][None, :], mod, sh_off=3 * d, sc_off=4 * d, seg_fn=seg_fn,
                           tr=tb, name="norm2")
            mid = _swiglu_up(h2, ffn_w1[i].astype(BF16), ffn_w3[i].astype(BF16),
                             tm=tm_big, tn=1024, tk=2048, name="ffn_up")
            xs = _mm(mid, ffn_w2[i].astype(BF16), out_dtype=F32, tm=tm_big, tn=1024, tk=2048,
                     resid=(xs, mod, 5 * d, seg_fn), name="ffn_down")
        else:
            rw = jnp.zeros((d, LANES), F32).at[:, :N_EXPERTS].set(router_w[i])
            h2, comb, cnt = _norm_mod(xs, norm2_w[l][None, :], mod, sh_off=3 * d, sc_off=4 * d,
                                      seg_fn=seg_fn, tr=tb, router_w=rw, name="norm2_router")
            cnt_tile = cnt.reshape(t // tm_big, tm_big // tb, SUBLANES, LANES)[:, :, 0, :N_EXPERTS]
            tab = _moe_tables(jnp.sum(cnt_tile, axis=1).astype(jnp.int32), tm_big)
            a_s, w_s, l0, l1 = _moe_gather(h2, comb, tab, tm=tm_big)
            mid = _moe_grouped(a_s, (moe_w1[i].astype(BF16), moe_w3[i].astype(BF16)), w_s, tab,
                               tn=1024, tk=d, name="moe_up")
            ys = _moe_grouped(mid, (moe_w2[i].astype(BF16),), None, tab, tn=1024, tk=mid.shape[1],
                              name="moe_down")
            xs = _moe_scatter(l0, l1, ys, xs, mod, 5 * d, tab, tm=tm_big, tn=1024, seg_fn=seg_fn)

    out = _final_norm(xs, final_norm_w[None, :], row0=n_ctx_rows, rows=nb * seq, tr=tb)
    return out.reshape(nb, seq, d)
```

```python
import functools
import math

import jax
import jax.numpy as jnp
from jax import lax
from jax.experimental import pallas as pl
from jax.experimental.pallas import tpu as pltpu

F32 = jnp.float32
BF16 = jnp.bfloat16

GRID_W = 64
EPS = 1e-6
HGRN_DK = 128
MLSTM_HEADS = 8
SSD_HEADDIM = 64
SSD_GROUPS = 8
SSD_STATE = 128
N_EXPERTS = 8
LANES = 128
SUBLANES = 8
LOG2E = 1.4426950408889634
MOD_ROWS = 8
VMEM_LIMIT = 56 * 1024 * 1024


def _cparams(sem):
    return pltpu.CompilerParams(dimension_semantics=sem, vmem_limit_bytes=VMEM_LIMIT)


def _tile(n, pref):
    t = min(n, pref)
    while n % t:
        t //= 2
    return t


def _sigmoid(x):
    return 1.0 / (1.0 + jnp.exp(-x))


def _silu(x):
    return x * _sigmoid(x)


def _log_sigmoid(x):
    return jnp.minimum(x, 0.0) - jnp.log1p(jnp.exp(-jnp.abs(x)))


def _softplus(x):
    return jnp.maximum(x, 0.0) + jnp.log1p(jnp.exp(-jnp.abs(x)))


def _dot(a, b):
    return jnp.dot(a, b, preferred_element_type=F32)


def _dot_nt(a, b):
    return lax.dot_general(a, b, (((1,), (1,)), ((), ())), preferred_element_type=F32)


def _split3(x):
    x1 = x.astype(BF16)
    r = x - x1.astype(F32)
    x2 = r.astype(BF16)
    x3 = (r - x2.astype(F32)).astype(BF16)
    return x1, x2, x3


def _dot_exact(m01, x):
    x1, x2, x3 = _split3(x)
    return _dot(m01, x1) + _dot(m01, x2) + _dot(m01, x3)


def _mod_row(tile, n_ctx_tiles, tiles_per_batch, ctx_row):
    return jnp.where(tile < n_ctx_tiles, ctx_row, (tile - n_ctx_tiles) // tiles_per_batch)


def _accumulate(k, nk, part, acc_refs, finish):
    if nk == 1:
        finish(*part())
        return

    @pl.when(k == 0)
    def _():
        for acc, p in zip(acc_refs, part()):
            acc[...] = p

    @pl.when((k > 0) & (k < nk - 1))
    def _():
        for acc, p in zip(acc_refs, part()):
            acc[...] += p

    @pl.when(k == nk - 1)
    def _():
        finish(*[acc[...] + p for acc, p in zip(acc_refs, part())])


def _mm_kernel(*refs, nk, a_silu, has_bias, resid, seg):
    it = iter(refs)
    a_ref, w_ref = next(it), next(it)
    bias_ref = next(it) if has_bias else None
    x_ref = next(it) if resid else None
    mod_ref = next(it) if resid else None
    out_ref = next(it)
    acc_ref = next(it) if nk > 1 else None
    k = pl.program_id(2)
    row = _mod_row(pl.program_id(0), *seg) if resid else None

    def part():
        a = a_ref[...]
        if a_silu:
            a = _silu(a.astype(F32))
        return (_dot(a.astype(BF16), w_ref[...].astype(BF16)),)

    def finish(r):
        if has_bias:
            r = r + bias_ref[...]
        if resid:
            r = x_ref[...] + mod_ref[pl.ds(row, 1), :] * r
        out_ref[...] = r.astype(out_ref.dtype)

    _accumulate(k, nk, part, (acc_ref,), finish)


def _mm(a, w, *, out_dtype, tm, tn, tk, w_lead=None, a_silu=False, bias=None,
        resid=None, name):
    m, kdim = a.shape
    n = w.shape[-1]
    tm, tn, tk = _tile(m, tm), _tile(n, tn), _tile(kdim, tk)
    nk = kdim // tk
    if w_lead is None:
        w_spec = pl.BlockSpec((tk, tn), lambda i, j, k: (k, j))
    else:
        w_spec = pl.BlockSpec((None, tk, tn), lambda i, j, k: (w_lead, k, j))
    in_specs = [pl.BlockSpec((tm, tk), lambda i, j, k: (i, k)), w_spec]
    args = [a, w]
    if bias is not None:
        in_specs.append(pl.BlockSpec((1, tn), lambda i, j, k: (0, j)))
        args.append(bias)
    seg = None
    if resid is not None:
        x, mod, gate_off, seg_fn = resid
        seg = seg_fn(tm)
        gblk = gate_off // tn
        in_specs.append(pl.BlockSpec((tm, tn), lambda i, j, k: (i, j)))
        in_specs.append(pl.BlockSpec((MOD_ROWS, tn), lambda i, j, k: (0, gblk + j)))
        args += [x, mod]
    return pl.pallas_call(
        functools.partial(_mm_kernel, nk=nk, a_silu=a_silu, has_bias=bias is not None,
                          resid=resid is not None, seg=seg),
        grid=(m // tm, n // tn, nk),
        in_specs=in_specs,
        out_specs=pl.BlockSpec((tm, tn), lambda i, j, k: (i, j)),
        out_shape=jax.ShapeDtypeStruct((m, n), out_dtype),
        scratch_shapes=[pltpu.VMEM((tm, tn), F32)] if nk > 1 else [],
        compiler_params=_cparams(("parallel", "parallel", "arbitrary")),
        name=name,
    )(*args)


def _norm_mod_kernel(*refs, d, sh_off, sc_off, seg, router):
    if router:
        x_ref, nw_ref, mod_ref, rw_ref, out_ref, comb_ref, cnt_ref = refs
    else:
        x_ref, nw_ref, mod_ref, out_ref = refs
    row = _mod_row(pl.program_id(0), *seg)
    x = x_ref[...]
    y = x * lax.rsqrt(jnp.mean(x * x, axis=-1, keepdims=True) + EPS) * nw_ref[...]
    sc = mod_ref[pl.ds(row, 1), sc_off:sc_off + d]
    sh = mod_ref[pl.ds(row, 1), sh_off:sh_off + d]
    h = y * (1.0 + sc) + sh
    out_ref[...] = h.astype(BF16)
    if router:
        h1, h2, h3 = _split3(h)
        r1, r2, r3 = _split3(rw_ref[...])
        logits = (_dot(h1, r1) + _dot(h1, r2) + _dot(h2, r1)
                  + _dot(h2, r2) + _dot(h1, r3) + _dot(h3, r1))
        lane = lax.broadcasted_iota(jnp.int32, logits.shape, 1).astype(F32)
        valid = lane < N_EXPERTS
        logits = jnp.where(valid, logits, -jnp.inf)
        mx = jnp.max(logits, axis=-1, keepdims=True)
        e = jnp.exp(logits - mx)
        probs = e / jnp.sum(e, axis=-1, keepdims=True)
        p1 = jnp.max(probs, axis=-1, keepdims=True)
        i1 = jnp.min(jnp.where((probs == p1) & valid, lane, float(LANES)), axis=-1, keepdims=True)
        rest = jnp.where((lane == i1) | (lane >= N_EXPERTS), -1.0, probs)
        p2 = jnp.max(rest, axis=-1, keepdims=True)
        i2 = jnp.min(jnp.where(rest == p2, lane, float(LANES)), axis=-1, keepdims=True)
        tot = p1 + p2
        comb = jnp.where(lane == i1, p1 / tot, jnp.where(lane == i2, p2 / tot, 0.0))
        comb_ref[...] = comb
        cnt = jnp.sum(jnp.where(comb > 0.0, 1.0, 0.0), axis=0, keepdims=True)
        cnt_ref[...] = jnp.broadcast_to(cnt, cnt_ref.shape)


def _norm_mod(x, nw, mod, *, sh_off, sc_off, seg_fn, tr, router_w=None, name):
    t, d = x.shape
    tr = _tile(t, tr)
    router = router_w is not None
    in_specs = [pl.BlockSpec((tr, d), lambda i: (i, 0)),
                pl.BlockSpec((1, d), lambda i: (0, 0)),
                pl.BlockSpec(mod.shape, lambda i: (0, 0))]
    args = [x, nw, mod]
    out_specs = [pl.BlockSpec((tr, d), lambda i: (i, 0))]
    out_shape = [jax.ShapeDtypeStruct((t, d), BF16)]
    if router:
        in_specs.append(pl.BlockSpec(router_w.shape, lambda i: (0, 0)))
        args.append(router_w)
        out_specs.append(pl.BlockSpec((tr, LANES), lambda i: (i, 0)))
        out_shape.append(jax.ShapeDtypeStruct((t, LANES), F32))
        out_specs.append(pl.BlockSpec((SUBLANES, LANES), lambda i: (i, 0)))
        out_shape.append(jax.ShapeDtypeStruct((t // tr * SUBLANES, LANES), F32))
    res = pl.pallas_call(
        functools.partial(_norm_mod_kernel, d=d, sh_off=sh_off, sc_off=sc_off, seg=seg_fn(tr),
                          router=router),
        grid=(t // tr,), in_specs=in_specs, out_specs=out_specs, out_shape=out_shape,
        compiler_params=_cparams(("parallel",)), name=name,
    )(*args)
    return res if router else res[0]


def _norm_mod_join_kernel(ctx_ref, lat_ref, nw_ref, mod_ref, out_ref, xs_ref, *, d, sh_off, sc_off, seg):
    i = pl.program_id(0)
    row = _mod_row(i, *seg)
    x = jnp.where(i < seg[0], ctx_ref[...], lat_ref[...])
    xs_ref[...] = x
    y = x * lax.rsqrt(jnp.mean(x * x, axis=-1, keepdims=True) + EPS) * nw_ref[...]
    sc = mod_ref[pl.ds(row, 1), sc_off:sc_off + d]
    sh = mod_ref[pl.ds(row, 1), sh_off:sh_off + d]
    out_ref[...] = (y * (1.0 + sc) + sh).astype(BF16)


def _norm_mod_join(ctx2d, lat2d, nw, mod, *, sh_off, sc_off, seg_fn, tr):
    d = ctx2d.shape[1]
    t = ctx2d.shape[0] + lat2d.shape[0]
    seg = seg_fn(tr)
    nct = seg[0]
    return pl.pallas_call(
        functools.partial(_norm_mod_join_kernel, d=d, sh_off=sh_off, sc_off=sc_off, seg=seg),
        grid=(t // tr,),
        in_specs=[pl.BlockSpec((tr, d), lambda i: (jnp.minimum(i, nct - 1), 0)),
                  pl.BlockSpec((tr, d), lambda i: (jnp.maximum(i - nct, 0), 0)),
                  pl.BlockSpec((1, d), lambda i: (0, 0)),
                  pl.BlockSpec(mod.shape, lambda i: (0, 0))],
        out_specs=[pl.BlockSpec((tr, d), lambda i: (i, 0)), pl.BlockSpec((tr, d), lambda i: (i, 0))],
        out_shape=[jax.ShapeDtypeStruct((t, d), BF16), jax.ShapeDtypeStruct((t, d), F32)],
        compiler_params=_cparams(("arbitrary",)), name="norm1_join",
    )(ctx2d, lat2d, nw, mod)


def _final_norm_kernel(x_ref, w_ref, out_ref):
    x = x_ref[...]
    out_ref[...] = x * lax.rsqrt(jnp.mean(x * x, axis=-1, keepdims=True) + EPS) * w_ref[...]


def _final_norm(x, w, *, row0, rows, tr):
    d = x.shape[1]
    tr = _tile(math.gcd(row0, rows), tr)
    off = row0 // tr
    return pl.pallas_call(
        _final_norm_kernel, grid=(rows // tr,),
        in_specs=[pl.BlockSpec((tr, d), lambda i: (i + off, 0)),
                  pl.BlockSpec((1, d), lambda i: (0, 0))],
        out_specs=pl.BlockSpec((tr, d), lambda i: (i, 0)),
        out_shape=jax.ShapeDtypeStruct((rows, d), F32),
        compiler_params=_cparams(("parallel",)), name="final_norm",
    )(x, w)


def _conv_kernel(main_ref, prev_ref, next_ref, w_ref, b_ref, out_ref, *,
                 tc, ctx_len, n_ctx_tiles, tiles_per_img):
    i = pl.program_id(0)
    is_ctx = i < n_ctx_tiles
    li = i - n_ctx_tiles
    first = (li % tiles_per_img) == 0
    last = (li % tiles_per_img) == tiles_per_img - 1
    main = main_ref[...]
    prev = jnp.where(is_ctx | first, 0.0, prev_ref[...])
    nxt = jnp.where(is_ctx | last, 0.0, next_ref[...])
    z = jnp.concatenate([prev, main, nxt], axis=0)
    n = tc + 2 * GRID_W
    pos = lax.broadcasted_iota(jnp.int32, (n, 1), 0)
    col_lat = pos % GRID_W
    col_ctx = (pos + (ctx_len - GRID_W)) % ctx_len
    col = jnp.where(is_ctx, col_ctx, col_lat)
    width = jnp.where(is_ctx, ctx_len, GRID_W)
    zl = jnp.where(col != 0, pltpu.roll(z, 1, 0), 0.0)
    zr = jnp.where(col != width - 1, pltpu.roll(z, n - 1, 0), 0.0)
    taps = (zl, z, zr)
    w = w_ref[...]
    acc = b_ref[...] + jnp.zeros((tc, main.shape[1]), F32)
    for dr in range(3):
        for dc in range(3):
            wt = w[3 * dr + dc:3 * dr + dc + 1, :]
            if dr != 1:
                wt = jnp.where(is_ctx, 0.0, wt)
            acc = acc + wt * taps[dc][GRID_W * dr:GRID_W * dr + tc, :]
    out_ref[...] = _silu(acc)


def _conv(p_main, conv_w9, conv_b, *, n_conv, tc, ctx_len, n_ctx_tiles, tiles_per_img, cb):
    t = p_main.shape[0]
    cb = _tile(n_conv, cb)
    rpt = tc // GRID_W
    nrow = t // GRID_W
    return pl.pallas_call(
        functools.partial(_conv_kernel, tc=tc, ctx_len=ctx_len, n_ctx_tiles=n_ctx_tiles,
                          tiles_per_img=tiles_per_img),
        grid=(t // tc, n_conv // cb),
        in_specs=[pl.BlockSpec((tc, cb), lambda i, j: (i, j)),
                  pl.BlockSpec((GRID_W, cb), lambda i, j: (jnp.maximum(i * rpt - 1, 0), j)),
                  pl.BlockSpec((GRID_W, cb), lambda i, j: (jnp.minimum((i + 1) * rpt, nrow - 1), j)),
                  pl.BlockSpec((9, cb), lambda i, j: (0, j)),
                  pl.BlockSpec((1, cb), lambda i, j: (0, j))],
        out_specs=pl.BlockSpec((tc, cb), lambda i, j: (i, j)),
        out_shape=jax.ShapeDtypeStruct((t, n_conv), F32),
        compiler_params=_cparams(("parallel", "parallel")), name="conv_silu",
    )(p_main, p_main, p_main, conv_w9, conv_b)


def _rowblk(b, s, rev, ncb, nlb, nb):
    if rev:
        ctx = b * ncb + (ncb - 1 - s)
        lat = nb * ncb + b * nlb + (nlb - 1 - (s - ncb))
    else:
        ctx = b * ncb + s
        lat = nb * ncb + b * nlb + (s - ncb)
    return jnp.where(s < ncb, ctx, lat)


def _tri(n, rev):
    r = lax.broadcasted_iota(jnp.int32, (n, n), 0)
    c = lax.broadcasted_iota(jnp.int32, (n, n), 1)
    return (c >= r) if rev else (c <= r)


def _level_ref(bl, half, rev):
    tb, dk = bl.shape
    blk = 2 * half
    idx = half if rev else half - 1
    if blk == tb:
        return bl[idx:idx + 1, :]
    if blk >= SUBLANES:
        b3 = bl.reshape(tb // blk, blk, dk)
        return jnp.broadcast_to(b3[:, idx:idx + 1, :], b3.shape).reshape(tb, dk)
    tmod = lax.broadcasted_iota(jnp.int32, (tb, 1), 0) % blk
    r = bl
    for m in range(blk):
        if idx != m:
            r = jnp.where(tmod == m, pltpu.roll(bl, (m - idx) % tb, 0), r)
    return r


def _hgrn_dir(q_ref, u_ref, v_ref, lb_ref, o_ref, st_ref, d, rev, tb):
    dk = HGRN_DK
    lbv = lb_ref[...]
    qraw = q_ref[...]
    u = u_ref[...]
    v = v_ref[...]
    q = _silu(qraw) * dk ** -0.5
    la = jnp.log(lbv)
    lc = jnp.log1p(-lbv) + _log_sigmoid(u)
    mx = jnp.maximum(la, lc)
    mn = jnp.minimum(la, lc)
    logf = mx + jnp.log1p(jnp.exp(mn - mx))
    k = (1.0 - lbv) * _sigmoid(-u)

    ri = lax.broadcasted_iota(jnp.int32, (tb, tb), 0)
    cj = lax.broadcasted_iota(jnp.int32, (tb, tb), 1)
    keep = (cj >= ri) if rev else (cj <= ri)
    bl = _dot_exact(jnp.where(keep, 1.0, 0.0).astype(BF16), logf * LOG2E)
    b_end = bl[0:1] if rev else bl[tb - 1:tb]
    half = tb // 2
    ri = lax.broadcasted_iota(jnp.int32, (half, half), 0)
    cj = lax.broadcasted_iota(jnp.int32, (half, half), 1)
    level = jnp.where((cj > ri) if rev else (cj < ri), 31 - lax.clz(ri ^ cj), -1)
    halves = (slice(0, half), slice(half, tb))
    scores = [jnp.zeros((half, half), F32), jnp.zeros((half, half), F32)]
    for lv in range(half.bit_length() - 1):
        dq = bl - _level_ref(bl, 1 << lv, rev)
        qt = (q * jnp.exp2(dq)).astype(BF16)
        kt = (k * jnp.exp2(-dq)).astype(BF16)
        for hb, rows in enumerate(halves):
            scores[hb] = jnp.where(level == lv, _dot_nt(qt[rows], kt[rows]), scores[hb])
    isl, jsl = (halves[0], halves[1]) if rev else (halves[1], halves[0])
    dq = bl - _level_ref(bl, half, rev)
    cross = _dot_nt((q[isl] * jnp.exp2(dq[isl])).astype(BF16), (k[jsl] * jnp.exp2(-dq[jsl])).astype(BF16))
    vb = v.astype(BF16)
    st = st_ref[d]
    rest = (jnp.sum(q * k, axis=-1, keepdims=True) * v
            + _dot_nt((q * jnp.exp2(bl)).astype(BF16), st.astype(BF16)))
    for hb, rows in enumerate(halves):
        o = _dot(scores[hb].astype(BF16), vb[rows]) + rest[rows]
        if rows == isl:
            o = o + _dot(cross.astype(BF16), vb[jsl])
        o_ref[rows, :] = o
    st_ref[d] = st * jnp.exp2(b_end) + _dot(v.T.astype(BF16), (k * jnp.exp2(b_end - bl)).astype(BF16))


def _hgrn_kernel(qf, uf, vf, qb, ub, vb, lb_ref, of, ob, st_ref, *, tb):
    @pl.when(pl.program_id(2) == 0)
    def _():
        st_ref[...] = jnp.zeros_like(st_ref)
    _hgrn_dir(qf, uf, vf, lb_ref, of, st_ref, 0, False, tb)
    _hgrn_dir(qb, ub, vb, lb_ref, ob, st_ref, 1, True, tb)


def _hgrn_scan(p_main, lb_row, *, nb, ncb, nlb, tb, heads, off_q, off_ff, off_fb, off_i):
    t = p_main.shape[0]
    dk = HGRN_DK

    def spec(off, rev):
        return pl.BlockSpec((tb, dk), lambda b, h, s: (_rowblk(b, s, rev, ncb, nlb, nb), off // dk + h))

    def ospec(rev):
        return pl.BlockSpec((tb, dk), lambda b, h, s: (_rowblk(b, s, rev, ncb, nlb, nb), h))

    out = jax.ShapeDtypeStruct((t, heads * dk), F32)
    return pl.pallas_call(
        functools.partial(_hgrn_kernel, tb=tb),
        grid=(nb, heads, ncb + nlb),
        in_specs=[spec(off_q, False), spec(off_ff, False), spec(off_i, False),
                  spec(off_q, True), spec(off_fb, True), spec(off_i, True),
                  pl.BlockSpec((1, dk), lambda b, h, s: (0, h))],
        out_specs=[ospec(False), ospec(True)],
        out_shape=[out, out],
        scratch_shapes=[pltpu.VMEM((2, dk, dk), F32)],
        compiler_params=_cparams(("parallel", "parallel", "arbitrary")), name="hgrn_scan",
    )(p_main, p_main, p_main, p_main, p_main, p_main, lb_row)


def _mlstm_dir(q_ref, k_ref, v_ref, g_ref, bias_ref, o_ref, c_ref, n_ref, m_ref, d, rev, tb, dk, dv):
    g = g_ref[...] + bias_ref[...]
    lf = _log_sigmoid(g)
    keep = _tri(tb, rev)
    b = _dot_exact(jnp.where(keep, 1.0, 0.0).astype(BF16), lf)
    bt, gt = b.T, g.T
    for h in range(MLSTM_HEADS):
        ci = 2 * MLSTM_HEADS * d + h
        cf = ci + MLSTM_HEADS
        q = q_ref[:, h * dk:(h + 1) * dk]
        ks = k_ref[:, h * dk:(h + 1) * dk] * dk ** -0.5
        v = v_ref[:, h * dv:(h + 1) * dv]
        b_col, ic_col = b[:, cf:cf + 1], g[:, ci:ci + 1]
        b_row, ic_row = bt[cf:cf + 1, :], gt[ci:ci + 1, :]
        b_end = b_col[0:1] if rev else b_col[tb - 1:tb]
        m_prev = m_ref[d, h][:, 0:1]
        w_in = jnp.where(keep, b_col - b_row + ic_row, -jnp.inf)
        w_st = b_col + m_prev
        m_row = jnp.maximum(jnp.max(w_in, axis=1, keepdims=True), w_st)
        qb = q.astype(BF16)
        vb = v.astype(BF16)
        p = jnp.exp(w_in - m_row) * _dot_nt(qb, ks.astype(BF16))
        e_st = jnp.exp(w_st - m_row)
        num = _dot(p.astype(BF16), vb) + e_st * _dot(qb, c_ref[d, h].astype(BF16))
        nrm = (jnp.sum(p, axis=1, keepdims=True)
               + e_st * jnp.sum(q * n_ref[d, h], axis=1, keepdims=True))
        o_ref[:, h * dv:(h + 1) * dv] = num / jnp.maximum(jnp.abs(nrm), jnp.exp(-m_row))
        w_end = b_end - b_col + ic_col
        m_new = jnp.maximum(b_end + m_prev, jnp.max(w_end, axis=0, keepdims=True))
        s_old = jnp.exp(b_end + m_prev - m_new)
        kt = ks * jnp.exp(w_end - m_new)
        c_ref[d, h] = s_old * c_ref[d, h] + _dot(kt.T.astype(BF16), vb)
        n_ref[d, h] = s_old * n_ref[d, h] + jnp.sum(kt, axis=0, keepdims=True)
        m_ref[d, h] = jnp.broadcast_to(m_new, m_ref.shape[2:])


def _mlstm_kernel(qf, kf, vf, gf, qb, kb, vb, gb, bias_ref, of, ob, c_ref, n_ref, m_ref, *, tb, dk, dv):
    @pl.when(pl.program_id(1) == 0)
    def _():
        c_ref[...] = jnp.zeros_like(c_ref)
        n_ref[...] = jnp.zeros_like(n_ref)
        m_ref[...] = jnp.zeros_like(m_ref)
    _mlstm_dir(qf, kf, vf, gf, bias_ref, of, c_ref, n_ref, m_ref, 0, False, tb, dk, dv)
    _mlstm_dir(qb, kb, vb, gb, bias_ref, ob, c_ref, n_ref, m_ref, 1, True, tb, dk, dv)


def _mlstm_scan(conv, p_main, p_tail, bias_row, *, nb, ncb, nlb, tb, dk, dv,
                off_q, off_k, off_v, off_small):
    t = conv.shape[0]

    nh = MLSTM_HEADS

    def spec(width, off, rev):
        return pl.BlockSpec((tb, width), lambda b, s: (_rowblk(b, s, rev, ncb, nlb, nb), off // width))

    def ospec(rev):
        return pl.BlockSpec((tb, nh * dv), lambda b, s: (_rowblk(b, s, rev, ncb, nlb, nb), 0))

    out = jax.ShapeDtypeStruct((t, nh * dv), F32)
    return pl.pallas_call(
        functools.partial(_mlstm_kernel, tb=tb, dk=dk, dv=dv),
        grid=(nb, ncb + nlb),
        in_specs=[spec(nh * dk, off_q, False), spec(nh * dk, off_k, False), spec(nh * dv, off_v, False),
                  spec(LANES, off_small, False),
                  spec(nh * dk, off_q, True), spec(nh * dk, off_k, True), spec(nh * dv, off_v, True),
                  spec(LANES, off_small, True),
                  pl.BlockSpec((1, LANES), lambda b, s: (0, 0))],
        out_specs=[ospec(False), ospec(True)],
        out_shape=[out, out],
        scratch_shapes=[pltpu.VMEM((2, nh, dk, dv), F32), pltpu.VMEM((2, nh, 1, dk), F32),
                        pltpu.VMEM((2, nh, 1, LANES), F32)],
        compiler_params=_cparams(("parallel", "arbitrary")), name="mlstm_scan",
    )(conv, conv, p_main, p_tail, conv, conv, p_main, p_tail, bias_row)


def _ssd_dir(c_ref, bm_ref, x_ref, g_ref, dtb_ref, alog_ref, o_ref, s_ref, d, rev, tb, heads, hpg):
    dt = _softplus(g_ref[...] + dtb_ref[...])
    la = -jnp.exp(alog_ref[...]) * dt
    keep = _tri(tb, rev)
    b = _dot_exact(jnp.where(keep, 1.0, 0.0).astype(BF16), la * LOG2E)
    bt = b.T
    width = hpg * SSD_HEADDIM
    hl = lax.broadcasted_iota(jnp.int32, (1, width), 1) // SSD_HEADDIM
    for grp in range(SSD_GROUPS):
        lane0 = MLSTM_HEADS * 4 + heads * d + hpg * grp
        cm = c_ref[:, grp * SSD_STATE:(grp + 1) * SSD_STATE].astype(BF16)
        bm = bm_ref[:, grp * SSD_STATE:(grp + 1) * SSD_STATE]
        x = x_ref[:, grp * width:(grp + 1) * width]
        b_all = jnp.zeros((tb, width), F32)
        dt_all = jnp.zeros((tb, width), F32)
        for i in range(hpg):
            b_all = jnp.where(hl == i, b[:, lane0 + i:lane0 + i + 1], b_all)
            dt_all = jnp.where(hl == i, dt[:, lane0 + i:lane0 + i + 1], dt_all)
        b_end = b_all[0:1] if rev else b_all[tb - 1:tb]
        v_all = x * dt_all
        v_bf = v_all.astype(BF16)
        gmat = _dot_nt(cm, bm.astype(BF16))
        s_old = s_ref[d, grp]
        inter = jnp.exp2(b_all) * _dot(cm, s_old.astype(BF16))
        outs = []
        for i in range(hpg):
            c = lane0 + i
            dec = jnp.where(keep, jnp.exp2(b[:, c:c + 1] - bt[c:c + 1, :]), 0.0)
            outs.append(_dot((gmat * dec).astype(BF16), v_bf[:, i * SSD_HEADDIM:(i + 1) * SSD_HEADDIM]))
        o_ref[:, grp * width:(grp + 1) * width] = jnp.concatenate(outs, axis=1) + inter
        sv = (v_all * jnp.exp2(b_end - b_all)).astype(BF16)
        s_ref[d, grp] = jnp.exp2(b_end) * s_old + _dot(bm.T.astype(BF16), sv)


def _ssd_kernel(cf, bf, xf, gf, cb, bb, xb, gb, dtb_ref, alog_ref, of, ob, s_ref, *, tb, heads, hpg):
    @pl.when(pl.program_id(1) == 0)
    def _():
        s_ref[...] = jnp.zeros_like(s_ref)
    _ssd_dir(cf, bf, xf, gf, dtb_ref, alog_ref, of, s_ref, 0, False, tb, heads, hpg)
    _ssd_dir(cb, bb, xb, gb, dtb_ref, alog_ref, ob, s_ref, 1, True, tb, heads, hpg)


def _ssd_scan(conv, p_tail, dtb_row, alog_row, *, nb, ncb, nlb, tb, heads, off_x, off_b, off_c, off_small):
    t = conv.shape[0]
    hpg = heads // SSD_GROUPS
    width = hpg * SSD_HEADDIM

    ng = SSD_GROUPS

    def spec(w, off, rev):
        return pl.BlockSpec((tb, w), lambda b, s: (_rowblk(b, s, rev, ncb, nlb, nb), off // w))

    def ospec(rev):
        return pl.BlockSpec((tb, ng * width), lambda b, s: (_rowblk(b, s, rev, ncb, nlb, nb), 0))

    row = pl.BlockSpec((1, LANES), lambda b, s: (0, 0))
    out = jax.ShapeDtypeStruct((t, heads * SSD_HEADDIM), F32)
    return pl.pallas_call(
        functools.partial(_ssd_kernel, tb=tb, heads=heads, hpg=hpg),
        grid=(nb, ncb + nlb),
        in_specs=[spec(ng * SSD_STATE, off_c, False), spec(ng * SSD_STATE, off_b, False),
                  spec(ng * width, off_x, False), spec(LANES, off_small, False),
                  spec(ng * SSD_STATE, off_c, True), spec(ng * SSD_STATE, off_b, True),
                  spec(ng * width, off_x, True), spec(LANES, off_small, True), row, row],
        out_specs=[ospec(False), ospec(True)],
        out_shape=[out, out],
        scratch_shapes=[pltpu.VMEM((2, ng, SSD_STATE, width), F32)],
        compiler_params=_cparams(("parallel", "arbitrary")), name="ssd_scan",
    )(conv, conv, conv, p_tail, conv, conv, conv, p_tail, dtb_row, alog_row)


def _headnorm_kernel(of_ref, ob_ref, g_ref, w_ref, out_ref, *, hd):
    cb = of_ref.shape[1]
    for i in range(cb // hd):
        sl = slice(i * hd, (i + 1) * hd)
        o = of_ref[:, sl] + ob_ref[:, sl]
        y = o * lax.rsqrt(jnp.mean(o * o, axis=-1, keepdims=True) + EPS) * w_ref[:, sl]
        out_ref[:, sl] = (y * _silu(g_ref[:, sl])).astype(BF16)


def _headnorm(of, ob, gsrc, goff, w, *, hd, tr, cb, name):
    t, wd = of.shape
    tr, cb = _tile(t, tr), _tile(wd, cb)
    gblk = goff // cb
    return pl.pallas_call(
        functools.partial(_headnorm_kernel, hd=hd), grid=(t // tr, wd // cb),
        in_specs=[pl.BlockSpec((tr, cb), lambda i, j: (i, j)),
                  pl.BlockSpec((tr, cb), lambda i, j: (i, j)),
                  pl.BlockSpec((tr, cb), lambda i, j: (i, gblk + j)),
                  pl.BlockSpec((1, cb), lambda i, j: (0, j))],
        out_specs=pl.BlockSpec((tr, cb), lambda i, j: (i, j)),
        out_shape=jax.ShapeDtypeStruct((t, wd), BF16),
        compiler_params=_cparams(("parallel", "parallel")), name=name,
    )(of, ob, gsrc, w)


def _ssd_finish_kernel(of_ref, ob_ref, x_ref, z_ref, d_ref, w_ref, out_ref):
    y = of_ref[...] + ob_ref[...] + d_ref[...] * x_ref[...]
    tt = y * _silu(z_ref[...])
    out_ref[...] = (tt * lax.rsqrt(jnp.mean(tt * tt, axis=-1, keepdims=True) + EPS)
                    * w_ref[...]).astype(BF16)


def _ssd_finish(of, ob, conv, off_x, p_tail, off_z, d_row, w, *, tr):
    t, wd = of.shape
    tr = _tile(t, tr)
    row = pl.BlockSpec((1, wd), lambda i: (0, 0))
    return pl.pallas_call(
        _ssd_finish_kernel, grid=(t // tr,),
        in_specs=[pl.BlockSpec((tr, wd), lambda i: (i, 0)),
                  pl.BlockSpec((tr, wd), lambda i: (i, 0)),
                  pl.BlockSpec((tr, wd), lambda i: (i, off_x // wd)),
                  pl.BlockSpec((tr, wd), lambda i: (i, off_z // wd)), row, row],
        out_specs=pl.BlockSpec((tr, wd), lambda i: (i, 0)),
        out_shape=jax.ShapeDtypeStruct((t, wd), BF16),
        compiler_params=_cparams(("parallel",)), name="ssd_finish",
    )(of, ob, conv, p_tail, d_row, w)


def _merge_kernel(y0, y1, y2, bw_ref, mg_ref, gw_ref, gb_ref, out_ref, acc_ref):
    k = pl.program_id(2)
    gate = _sigmoid(_dot(mg_ref[...].astype(BF16), gw_ref[...]) + gb_ref[...])
    for idx, y_ref in enumerate((y0, y1, y2)):
        @pl.when(k == idx)
        def _(y_ref=y_ref, idx=idx):
            r = gate * _dot(y_ref[...], bw_ref[...])
            if idx == 0:
                acc_ref[...] = r
            elif idx == 1:
                acc_ref[...] += r
            else:
                out_ref[...] = (acc_ref[...] + r).astype(BF16)


def _merge(ys, bw, p_tail, off_merge, rank, gw, gb, *, tm, tn):
    t, bwid = ys[0].shape
    d = bw.shape[-1]
    tm, tn = _tile(t, tm), _tile(d, tn)
    nj = d // tn
    yspec = pl.BlockSpec((tm, bwid), lambda i, j, k: (i, 0))
    return pl.pallas_call(
        _merge_kernel, grid=(t // tm, nj, 3),
        in_specs=[yspec, yspec, yspec,
                  pl.BlockSpec((None, bwid, tn), lambda i, j, k: (k, 0, j)),
                  pl.BlockSpec((tm, rank), lambda i, j, k: (i, off_merge // rank)),
                  pl.BlockSpec((rank, tn), lambda i, j, k: (0, k * nj + j)),
                  pl.BlockSpec((1, tn), lambda i, j, k: (0, k * nj + j))],
        out_specs=pl.BlockSpec((tm, tn), lambda i, j, k: (i, j)),
        out_shape=jax.ShapeDtypeStruct((t, d), BF16),
        scratch_shapes=[pltpu.VMEM((tm, tn), F32)],
        compiler_params=_cparams(("parallel", "parallel", "arbitrary")), name="branch_merge",
    )(ys[0], ys[1], ys[2], bw, p_tail, gw, gb)


def _swiglu_kernel(a_ref, w1_ref, w3_ref, out_ref, *accs, nk):
    def part():
        a = a_ref[...]
        return _dot(a, w1_ref[...]), _dot(a, w3_ref[...])

    def finish(r1, r3):
        out_ref[...] = (_silu(r1) * r3).astype(BF16)

    _accumulate(pl.program_id(2), nk, part, accs, finish)


def _swiglu_up(a, w1, w3, *, tm, tn, tk, name):
    t, kdim = a.shape
    n = w1.shape[1]
    tm, tn, tk = _tile(t, tm), _tile(n, tn), _tile(kdim, tk)
    nk = kdim // tk
    wspec = pl.BlockSpec((tk, tn), lambda i, j, k: (k, j))
    return pl.pallas_call(
        functools.partial(_swiglu_kernel, nk=nk),
        grid=(t // tm, n // tn, nk),
        in_specs=[pl.BlockSpec((tm, tk), lambda i, j, k: (i, k)), wspec, wspec],
        out_specs=pl.BlockSpec((tm, tn), lambda i, j, k: (i, j)),
        out_shape=jax.ShapeDtypeStruct((t, n), BF16),
        scratch_shapes=[pltpu.VMEM((tm, tn), F32)] * (2 if nk > 1 else 0),
        compiler_params=_cparams(("parallel", "parallel", "arbitrary")), name=name,
    )(a, w1, w3)


MOE_RB = LANES
MOE_MB = 512


def _moe_tables(cnt_tile, tm):
    nt, ne = cnt_tile.shape
    rb, bpm = MOE_RB, MOE_MB // MOE_RB
    ni = 2 * tm // rb + ne
    nblk = -(-(nt * ni + ne * (bpm - 1)) // bpm) * bpm
    nbk = (cnt_tile + (rb - 1)) // rb
    reg_e = (jnp.sum(nbk, axis=0) + (bpm - 1)) // bpm * bpm
    end_e = jnp.cumsum(reg_e)
    start_e = end_e - reg_e
    pre_ie = jnp.cumsum(nbk, axis=0) - nbk
    ends_ie = jnp.cumsum(nbk, axis=1)
    off_ie = ends_ie - nbk
    n_items = ends_ie[:, -1]
    it = jnp.arange(ni, dtype=jnp.int32)[None, :]
    itc = jnp.minimum(it, n_items[:, None] - 1)
    e_idx = jnp.sum((itc[:, :, None] >= ends_ie[:, None, :]).astype(jnp.int32), axis=-1)
    e_idx = jnp.minimum(e_idx, ne - 1)
    chunk = itc - jnp.take_along_axis(off_ie, e_idx, axis=1)
    gblk = start_e[e_idx] + jnp.take_along_axis(pre_ie, e_idx, axis=1) + chunk
    iout = jnp.where(it < n_items[:, None], gblk, nblk)
    m = jnp.arange(nblk // bpm, dtype=jnp.int32)
    mexp = jnp.sum((m[:, None] * bpm >= end_e[None, :]).astype(jnp.int32), axis=-1)
    off_row = jnp.zeros((nt, 1, LANES), F32).at[:, 0, :ne].set((off_ie * rb).astype(F32))
    return dict(ni=ni, nblk=nblk, iout=iout.reshape(-1).astype(jnp.int32),
                iblk=gblk.reshape(-1).astype(jnp.int32), n_items=n_items.astype(jnp.int32),
                mexp=jnp.minimum(mexp, ne - 1).astype(jnp.int32),
                nvalid=(end_e[-1:] // bpm).astype(jnp.int32), off_row=off_row)


def _moe_gather_kernel(iout_ref, h_ref, comb_ref, off_ref, za_ref, zw_ref,
                       a_ref, w_ref, l0_ref, l1_ref, lt_ref, wt_ref, *, tm, rb):
    del iout_ref, za_ref, zw_ref
    it = pl.program_id(1)

    @pl.when(it == 0)
    def _():
        comb = comb_ref[...]
        pick = comb > 0.0
        r = lax.broadcasted_iota(jnp.int32, (tm, tm), 0)
        c = lax.broadcasted_iota(jnp.int32, (tm, tm), 1)
        rank = _dot(jnp.where(c < r, 1.0, 0.0).astype(BF16), jnp.where(pick, 1.0, 0.0).astype(BF16))
        loc = off_ref[...] + rank
        lane = lax.broadcasted_iota(jnp.int32, comb.shape, 1).astype(F32)
        m1 = jnp.min(jnp.where(pick, lane, float(LANES)), axis=1, keepdims=True)
        first = pick & (lane == m1)
        second = pick & (lane != m1)
        l0 = jnp.sum(jnp.where(first, loc, 0.0), axis=1, keepdims=True)
        l1 = jnp.sum(jnp.where(second, loc + 1.0, 0.0), axis=1, keepdims=True) - 1.0
        l0b = jnp.broadcast_to(l0, comb.shape)
        l1b = jnp.broadcast_to(l1, comb.shape)
        l0_ref[...] = l0b
        l1_ref[...] = l1b
        lt_ref[0] = l0b.T
        lt_ref[1] = l1b.T
        wt_ref[0] = jnp.broadcast_to(jnp.sum(jnp.where(first, comb, 0.0), axis=1, keepdims=True), comb.shape)
        wt_ref[1] = jnp.broadcast_to(jnp.sum(jnp.where(second, comb, 0.0), axis=1, keepdims=True), comb.shape)

    s = (it * rb + lax.broadcasted_iota(jnp.int32, (rb, 1), 0)).astype(F32)
    p0 = jnp.where(lt_ref[0, 0:1, :] == s, 1.0, 0.0).astype(BF16)
    p1 = jnp.where(lt_ref[1, 0:1, :] == s, 1.0, 0.0).astype(BF16)
    a_ref[...] = _dot(p0 + p1, h_ref[...]).astype(BF16)
    w_ref[...] = _dot_exact(p0, wt_ref[0]) + _dot_exact(p1, wt_ref[1])


def _moe_gather(h, comb, tab, *, tm):
    t, d = h.shape
    rb, ni = MOE_RB, tab['ni']
    nrows = tab['nblk'] * rb + MOE_MB
    grid_spec = pltpu.PrefetchScalarGridSpec(
        num_scalar_prefetch=1, grid=(t // tm, ni),
        in_specs=[pl.BlockSpec((tm, d), lambda i, it, io: (i, 0)),
                  pl.BlockSpec((tm, LANES), lambda i, it, io: (i, 0)),
                  pl.BlockSpec((None, 1, LANES), lambda i, it, io: (i, 0, 0)),
                  pl.BlockSpec(memory_space=pl.ANY), pl.BlockSpec(memory_space=pl.ANY)],
        out_specs=[pl.BlockSpec((rb, d), lambda i, it, io: (io[i * ni + it], 0)),
                   pl.BlockSpec((rb, LANES), lambda i, it, io: (io[i * ni + it], 0)),
                   pl.BlockSpec((tm, LANES), lambda i, it, io: (i, 0)),
                   pl.BlockSpec((tm, LANES), lambda i, it, io: (i, 0))],
        scratch_shapes=[pltpu.VMEM((2, LANES, tm), F32), pltpu.VMEM((2, tm, LANES), F32)])
    return pl.pallas_call(
        functools.partial(_moe_gather_kernel, tm=tm, rb=rb), grid_spec=grid_spec,
        out_shape=[jax.ShapeDtypeStruct((nrows, d), BF16), jax.ShapeDtypeStruct((nrows, LANES), F32),
                   jax.ShapeDtypeStruct((t, LANES), F32), jax.ShapeDtypeStruct((t, LANES), F32)],
        input_output_aliases={4: 0, 5: 1},
        compiler_params=_cparams(("parallel", "arbitrary")), name="moe_gather",
    )(tab['iout'], h, comb, tab['off_row'], jnp.zeros((nrows, d), BF16), jnp.zeros((nrows, LANES), F32))


def _moe_grouped_kernel(*refs, nk, dual):
    if dual:
        mexp_ref, nv_ref, a_ref, w1_ref, w3_ref, ws_ref, out_ref, *accs = refs
    else:
        mexp_ref, nv_ref, a_ref, w1_ref, out_ref, *accs = refs
    del mexp_ref
    k = pl.program_id(2)

    def part():
        a = a_ref[...]
        if dual:
            return _dot(a, w1_ref[...]), _dot(a, w3_ref[...])
        return (_dot(a, w1_ref[...]),)

    def finish(r1, r3=None):
        if dual:
            out_ref[...] = (_silu(r1) * r3 * ws_ref[:, 0:1]).astype(BF16)
        else:
            out_ref[...] = r1.astype(BF16)

    @pl.when(pl.program_id(0) < nv_ref[0])
    def _():
        _accumulate(k, nk, part, accs, finish)


def _moe_grouped(a, ws, w_sorted, tab, *, tn, tk, name):
    rows, kdim = a.shape
    n = ws[0].shape[-1]
    tn, tk = _tile(n, tn), _tile(kdim, tk)
    nj, nk = n // tn, kdim // tk
    nmb = tab['nblk'] * MOE_RB // MOE_MB
    dual = len(ws) == 2

    def live(m, nv):
        return m < nv[0]

    def me(m, nv):
        return jnp.minimum(m, nv[0] - 1)

    a_spec = pl.BlockSpec((MOE_MB, tk), lambda m, j, k, ex, nv: (me(m, nv), jnp.where(live(m, nv), k, nk - 1)))
    w_spec = pl.BlockSpec((None, tk, tn), lambda m, j, k, ex, nv: (
        ex[me(m, nv)], jnp.where(live(m, nv), k, nk - 1), jnp.where(live(m, nv), j, nj - 1)))
    o_spec = pl.BlockSpec((MOE_MB, tn), lambda m, j, k, ex, nv: (me(m, nv), jnp.where(live(m, nv), j, nj - 1)))
    in_specs = [a_spec, w_spec]
    args = [a, ws[0]]
    if dual:
        in_specs += [w_spec, pl.BlockSpec((MOE_MB, LANES), lambda m, j, k, ex, nv: (me(m, nv), 0))]
        args += [ws[1], w_sorted]
    scratch = [pltpu.VMEM((MOE_MB, tn), F32)] * (len(ws) if nk > 1 else 0)
    grid_spec = pltpu.PrefetchScalarGridSpec(
        num_scalar_prefetch=2, grid=(nmb, nj, nk), in_specs=in_specs, out_specs=o_spec,
        scratch_shapes=scratch)
    return pl.pallas_call(
        functools.partial(_moe_grouped_kernel, nk=nk, dual=dual), grid_spec=grid_spec,
        out_shape=jax.ShapeDtypeStruct((rows, n), BF16),
        compiler_params=_cparams(("arbitrary", "arbitrary", "arbitrary")), name=name,
    )(tab['mexp'], tab['nvalid'], *args)


def _moe_scatter_kernel(iblk_ref, l0_ref, l1_ref, *rest, rb, ni, seg):
    del iblk_ref
    y_refs = rest[:ni]
    x_ref, mod_ref, out_ref, pt_ref, ya_ref = rest[ni:]
    row = _mod_row(pl.program_id(0), *seg)

    @pl.when(pl.program_id(1) == 0)
    def _():
        l0, l1 = l0_ref[...], l1_ref[...]
        lane = lax.broadcasted_iota(jnp.int32, (1, rb), 1).astype(F32)
        for q in range(ni):
            s = lane + float(q * rb)
            pt_ref[:, q * rb:(q + 1) * rb] = jnp.where((l0 == s) | (l1 == s), 1.0, 0.0).astype(BF16)

    for q in range(ni):
        ya_ref[q * rb:(q + 1) * rb, :] = y_refs[q][...]
    out_ref[...] = x_ref[...] + mod_ref[pl.ds(row, 1), :] * _dot(pt_ref[...], ya_ref[...])


def _moe_scatter(l0, l1, ys, x, mod, gate_off, tab, *, tm, tn, seg_fn):
    t, d = x.shape
    rb, ni = MOE_RB, tab['ni']
    tn = _tile(d, tn)
    gblk = gate_off // tn
    lspec = pl.BlockSpec((tm, LANES), lambda i, j, ib: (i, 0))
    yspecs = [pl.BlockSpec((rb, tn), lambda i, j, ib, q=q: (ib[i * ni + q], j)) for q in range(ni)]
    grid_spec = pltpu.PrefetchScalarGridSpec(
        num_scalar_prefetch=1, grid=(t // tm, d // tn),
        in_specs=[lspec, lspec, *yspecs,
                  pl.BlockSpec((tm, tn), lambda i, j, ib: (i, j)),
                  pl.BlockSpec((MOD_ROWS, tn), lambda i, j, ib: (0, gblk + j))],
        out_specs=pl.BlockSpec((tm, tn), lambda i, j, ib: (i, j)),
        scratch_shapes=[pltpu.VMEM((tm, ni * rb), BF16), pltpu.VMEM((ni * rb, tn), BF16)])
    return pl.pallas_call(
        functools.partial(_moe_scatter_kernel, rb=rb, ni=ni, seg=seg_fn(tm)), grid_spec=grid_spec,
        out_shape=jax.ShapeDtypeStruct((t, d), F32),
        compiler_params=_cparams(("parallel", "arbitrary")), name="moe_scatter",
    )(tab['iblk'], l0, l1, *([ys] * ni), x, mod)


def kernel(x, c, ctx, c_ctx, mod_w, mod_b, norm1_w, norm2_w, in_w, conv_w, conv_b, hgrn_lb, hgrn_norm_w,
           mlstm_igate_b, mlstm_fgate_b, mlstm_norm_w, ssd_a_log, ssd_dt_bias, ssd_d, ssd_norm_w,
           gate_w, gate_b, branch_w, out_w, ffn_w1, ffn_w3, ffn_w2, router_w, moe_w1, moe_w3, moe_w2,
           final_norm_w):
    nb, seq, d = x.shape
    ctx_len = ctx.shape[1]
    depth = mod_w.shape[0]
    bw = d // 2
    hg_heads = bw // HGRN_DK
    ml_dv = bw // MLSTM_HEADS
    ml_dk = ml_dv // 2
    ss_heads = bw // SSD_HEADDIM
    rank = gate_w.shape[1]
    n_ml_qk = MLSTM_HEADS * ml_dk
    n_ss_bc = SSD_GROUPS * SSD_STATE
    off = {}
    pos = 0
    for nm, sz in (('ml_q', n_ml_qk), ('ml_k', n_ml_qk), ('ss_x', bw), ('ss_B', n_ss_bc), ('ss_C', n_ss_bc),
                   ('hg_q', bw), ('hg_f_fwd', bw), ('hg_f_bwd', bw), ('hg_i', bw), ('hg_g', bw),
                   ('ml_v', bw), ('ml_z', bw), ('ml_gates', 4 * MLSTM_HEADS), ('ss_z', bw),
                   ('ss_dt', 2 * ss_heads), ('merge', rank)):
        off[nm] = (pos, sz)
        pos += sz
    n_conv = off['hg_q'][0]
    n_main = off['ml_gates'][0]
    n_small = 4 * MLSTM_HEADS + 2 * ss_heads
    assert n_small <= LANES and ctx_len % GRID_W == 0 and seq % ctx_len == 0
    t_off_z, t_off_merge, t_off_small = 0, bw, bw + rank
    n_tail_raw = bw + rank + LANES
    n_tail = -(-n_tail_raw // 512) * 512

    tb = ctx_len
    ncb, nlb = 1, seq // tb
    n_ctx_rows = nb * ctx_len
    t = n_ctx_rows + nb * seq
    ctx_row = nb

    def seg_fn(tile):
        assert n_ctx_rows % tile == 0 and seq % tile == 0
        return (n_ctx_rows // tile, seq // tile, ctx_row)

    tm_big = _tile(math.gcd(n_ctx_rows, seq), 1024)

    c_all = jnp.zeros((MOD_ROWS, d), F32).at[:nb].set(c).at[ctx_row].set(c_ctx)

    lb_cum = jnp.cumsum(jax.nn.softmax(hgrn_lb.astype(F32), axis=0), axis=0)
    lower_bounds = lb_cum - lb_cum[0]

    for l in range(depth):
        mod = _mm(c_all, mod_w, w_lead=l, out_dtype=F32, tm=MOD_ROWS, tn=1024, tk=2048,
                  a_silu=True, bias=mod_b[l][None, :], name="mod")
        if l == 0:
            h, xs = _norm_mod_join(ctx.reshape(n_ctx_rows, d), x.reshape(nb * seq, d), norm1_w[l][None, :],
                                   mod, sh_off=0, sc_off=d, seg_fn=seg_fn, tr=tb)
        else:
            h = _norm_mod(xs, norm1_w[l][None, :], mod, sh_off=0, sc_off=d, seg_fn=seg_fn, tr=tb,
                          name="norm1")
        w_main = in_w[l, :, :n_main].astype(BF16)
        wl = in_w[l]
        w_tail = jnp.concatenate(
            [wl[:, off['ss_z'][0]:off['ss_z'][0] + bw],
             wl[:, off['merge'][0]:off['merge'][0] + rank],
             wl[:, off['ml_gates'][0]:off['ml_gates'][0] + 4 * MLSTM_HEADS],
             wl[:, off['ss_dt'][0]:off['ss_dt'][0] + 2 * ss_heads],
             jnp.zeros((d, n_tail - n_tail_raw + LANES - n_small), F32)], axis=1).astype(BF16)
        p_main = _mm(h, w_main, out_dtype=F32, tm=tm_big, tn=1024, tk=d, name="in_proj_main")
        p_tail = _mm(h, w_tail, out_dtype=F32, tm=tm_big, tn=n_tail // 2, tk=2048, name="in_proj_tail")
        conv = _conv(p_main, conv_w[l].reshape(9, n_conv), conv_b[l][None, :], n_conv=n_conv, tc=tb,
                     ctx_len=ctx_len, n_ctx_tiles=n_ctx_rows // tb, tiles_per_img=seq // tb, cb=512)

        hgf, hgb = _hgrn_scan(p_main, lower_bounds[l][None, :], nb=nb, ncb=ncb, nlb=nlb, tb=tb,
                              heads=hg_heads, off_q=off['hg_q'][0], off_ff=off['hg_f_fwd'][0],
                              off_fb=off['hg_f_bwd'][0], off_i=off['hg_i'][0])
        ml_bias = jnp.zeros((1, LANES), F32).at[0, :4 * MLSTM_HEADS].set(
            jnp.stack([mlstm_igate_b[l, 0], mlstm_fgate_b[l, 0],
                       mlstm_igate_b[l, 1], mlstm_fgate_b[l, 1]]).reshape(-1))
        mlf, mlb = _mlstm_scan(conv, p_main, p_tail, ml_bias, nb=nb, ncb=ncb, nlb=nlb, tb=tb,
                               dk=ml_dk, dv=ml_dv, off_q=off['ml_q'][0], off_k=off['ml_k'][0],
                               off_v=off['ml_v'][0], off_small=t_off_small)
        lo = 4 * MLSTM_HEADS
        dtb_row = jnp.zeros((1, LANES), F32).at[0, lo:lo + 2 * ss_heads].set(ssd_dt_bias[l].reshape(-1))
        alog_row = jnp.zeros((1, LANES), F32).at[0, lo:lo + 2 * ss_heads].set(ssd_a_log[l].reshape(-1))
        ssf, ssb = _ssd_scan(conv, p_tail, dtb_row, alog_row, nb=nb, ncb=ncb, nlb=nlb, tb=tb,
                             heads=ss_heads, off_x=off['ss_x'][0], off_b=off['ss_B'][0],
                             off_c=off['ss_C'][0], off_small=t_off_small)

        y_hg = _headnorm(hgf, hgb, p_main, off['hg_g'][0], hgrn_norm_w[l][None, :], hd=HGRN_DK,
                         tr=tb, cb=512, name="hgrn_finish")
        y_ml = _headnorm(mlf, mlb, p_main, off['ml_z'][0], mlstm_norm_w[l][None, :], hd=ml_dv,
                         tr=tb, cb=512, name="mlstm_finish")
        d_row = jnp.repeat(ssd_d[l], SSD_HEADDIM)[None, :]
        y_ss = _ssd_finish(ssf, ssb, conv, off['ss_x'][0], p_tail, t_off_z, d_row,
                           ssd_norm_w[l][None, :], tr=tb)
        ym = _merge((y_hg, y_ml, y_ss), branch_w[l].astype(BF16), p_tail, t_off_merge, rank,
                    gate_w[l].astype(BF16), gate_b[l][None, :], tm=tm_big, tn=1024)
        xs = _mm(ym, out_w[l].astype(BF16), out_dtype=F32, tm=tm_big, tn=1024, tk=d,
                 resid=(xs, mod, 2 * d, seg_fn), name="out_proj")

        i = l // 2
        if l % 2 == 0:
            h2 = _norm_mod(xs, norm2_w[l][None, :], mod, sh_off=3 * d, sc_off=4 * d, seg_fn=seg_fn,
                           tr=tb, name="norm2")
            mid = _swiglu_up(h2, ffn_w1[i].astype(BF16), ffn_w3[i].astype(BF16),
                             tm=tm_big, tn=1024, tk=2048, name="ffn_up")
            xs = _mm(mid, ffn_w2[i].astype(BF16), out_dtype=F32, tm=tm_big, tn=1024, tk=2048,
                     resid=(xs, mod, 5 * d, seg_fn), name="ffn_down")
        else:
            rw = jnp.zeros((d, LANES), F32).at[:, :N_EXPERTS].set(router_w[i])
            h2, comb, cnt = _norm_mod(xs, norm2_w[l][None, :], mod, sh_off=3 * d, sc_off=4 * d,
                                      seg_fn=seg_fn, tr=tb, router_w=rw, name="norm2_router")
            cnt_tile = cnt.reshape(t // tm_big, tm_big // tb, SUBLANES, LANES)[:, :, 0, :N_EXPERTS]
            tab = _moe_tables(jnp.sum(cnt_tile, axis=1).astype(jnp.int32), tm_big)
            a_s, w_s, l0, l1 = _moe_gather(h2, comb, tab, tm=tm_big)
            mid = _moe_grouped(a_s, (moe_w1[i].astype(BF16), moe_w3[i].astype(BF16)), w_s, tab,
                               tn=1024, tk=d, name="moe_up")
            ys = _moe_grouped(mid, (moe_w2[i].astype(BF16),), None, tab, tn=1024, tk=mid.shape[1],
                              name="moe_down")
            xs = _moe_scatter(l0, l1, ys, xs, mod, 5 * d, tab, tm=tm_big, tn=1024, seg_fn=seg_fn)

    out = _final_norm(xs, final_norm_w[None, :], row0=n_ctx_rows, rows=nb * seq, tr=tb)
    return out.reshape(nb, seq, d)
```

```python
import functools
import math

import jax
import jax.numpy as jnp
from jax import lax
from jax.experimental import pallas as pl
from jax.experimental.pallas import tpu as pltpu

F32 = jnp.float32
BF16 = jnp.bfloat16

GRID_W = 64
EPS = 1e-6
HGRN_DK = 128
HGRN_HEADS_PER_STEP = 4
MLSTM_HEADS = 8
SSD_HEADDIM = 64
SSD_GROUPS = 8
SSD_STATE = 128
N_EXPERTS = 8
LANES = 128
SUBLANES = 8
LOG2E = 1.4426950408889634
MOD_ROWS = 8
VMEM_LIMIT = 56 * 1024 * 1024


def _cparams(sem):
    return pltpu.CompilerParams(dimension_semantics=sem, vmem_limit_bytes=VMEM_LIMIT)


def _tile(n, pref):
    t = min(n, pref)
    while n % t:
        t //= 2
    return t


def _sigmoid(x):
    return 1.0 / (1.0 + jnp.exp(-x))


def _silu(x):
    return x * _sigmoid(x)


def _log_sigmoid(x):
    return jnp.minimum(x, 0.0) - jnp.log1p(jnp.exp(-jnp.abs(x)))


def _softplus(x):
    return jnp.maximum(x, 0.0) + jnp.log1p(jnp.exp(-jnp.abs(x)))


def _dot(a, b):
    return jnp.dot(a, b, preferred_element_type=F32)


def _dot_nt(a, b):
    return lax.dot_general(a, b, (((1,), (1,)), ((), ())), preferred_element_type=F32)


def _split3(x):
    x1 = x.astype(BF16)
    r = x - x1.astype(F32)
    x2 = r.astype(BF16)
    x3 = (r - x2.astype(F32)).astype(BF16)
    return x1, x2, x3


def _dot_exact(m01, x):
    x1, x2, x3 = _split3(x)
    return _dot(m01, x1) + _dot(m01, x2) + _dot(m01, x3)


def _mod_row(tile, n_ctx_tiles, tiles_per_batch, ctx_row):
    return jnp.where(tile < n_ctx_tiles, ctx_row, (tile - n_ctx_tiles) // tiles_per_batch)


def _accumulate(k, nk, part, acc_refs, finish):
    if nk == 1:
        finish(*part())
        return

    @pl.when(k == 0)
    def _():
        for acc, p in zip(acc_refs, part()):
            acc[...] = p

    @pl.when((k > 0) & (k < nk - 1))
    def _():
        for acc, p in zip(acc_refs, part()):
            acc[...] += p

    @pl.when(k == nk - 1)
    def _():
        finish(*[acc[...] + p for acc, p in zip(acc_refs, part())])


def _mm_kernel(*refs, nk, a_silu, has_bias, resid, seg):
    it = iter(refs)
    a_ref, w_ref = next(it), next(it)
    bias_ref = next(it) if has_bias else None
    x_ref = next(it) if resid else None
    mod_ref = next(it) if resid else None
    out_ref = next(it)
    acc_ref = next(it) if nk > 1 else None
    k = pl.program_id(2)
    row = _mod_row(pl.program_id(0), *seg) if resid else None

    def part():
        a = a_ref[...]
        if a_silu:
            a = _silu(a.astype(F32))
        return (_dot(a.astype(BF16), w_ref[...].astype(BF16)),)

    def finish(r):
        if has_bias:
            r = r + bias_ref[...]
        if resid:
            r = x_ref[...] + mod_ref[pl.ds(row, 1), :] * r
        out_ref[...] = r.astype(out_ref.dtype)

    _accumulate(k, nk, part, (acc_ref,), finish)


def _mm(a, w, *, out_dtype, tm, tn, tk, w_lead=None, n=None, a_silu=False, bias=None,
        resid=None, name):
    m, kdim = a.shape
    n = w.shape[-1] if n is None else n
    tm, tn, tk = _tile(m, tm), _tile(n, tn), _tile(kdim, tk)
    nk = kdim // tk
    if w_lead is None:
        w_spec = pl.BlockSpec((tk, tn), lambda i, j, k: (k, j))
    else:
        w_spec = pl.BlockSpec((None, tk, tn), lambda i, j, k: (w_lead, k, j))
    in_specs = [pl.BlockSpec((tm, tk), lambda i, j, k: (i, k)), w_spec]
    args = [a, w]
    if bias is not None:
        in_specs.append(pl.BlockSpec((1, tn), lambda i, j, k: (0, j)))
        args.append(bias)
    seg = None
    if resid is not None:
        x, mod, gate_off, seg_fn = resid
        seg = seg_fn(tm)
        gblk = gate_off // tn
        in_specs.append(pl.BlockSpec((tm, tn), lambda i, j, k: (i, j)))
        in_specs.append(pl.BlockSpec((MOD_ROWS, tn), lambda i, j, k: (0, gblk + j)))
        args += [x, mod]
    return pl.pallas_call(
        functools.partial(_mm_kernel, nk=nk, a_silu=a_silu, has_bias=bias is not None,
                          resid=resid is not None, seg=seg),
        grid=(m // tm, n // tn, nk),
        in_specs=in_specs,
        out_specs=pl.BlockSpec((tm, tn), lambda i, j, k: (i, j)),
        out_shape=jax.ShapeDtypeStruct((m, n), out_dtype),
        scratch_shapes=[pltpu.VMEM((tm, tn), F32)] if nk > 1 else [],
        compiler_params=_cparams(("parallel", "parallel", "arbitrary")),
        name=name,
    )(*args)


def _norm_mod_kernel(*refs, d, sh_off, sc_off, seg, router):
    if router:
        x_ref, nw_ref, mod_ref, rw_ref, out_ref, comb_ref, cnt_ref = refs
    else:
        x_ref, nw_ref, mod_ref, out_ref = refs
    row = _mod_row(pl.program_id(0), *seg)
    x = x_ref[...]
    y = x * lax.rsqrt(jnp.mean(x * x, axis=-1, keepdims=True) + EPS) * nw_ref[...]
    sc = mod_ref[pl.ds(row, 1), sc_off:sc_off + d]
    sh = mod_ref[pl.ds(row, 1), sh_off:sh_off + d]
    h = y * (1.0 + sc) + sh
    out_ref[...] = h.astype(BF16)
    if router:
        h1, h2, h3 = _split3(h)
        r1, r2, r3 = _split3(rw_ref[...])
        logits = (_dot(h1, r1) + _dot(h1, r2) + _dot(h2, r1)
                  + _dot(h2, r2) + _dot(h1, r3) + _dot(h3, r1))
        lane = lax.broadcasted_iota(jnp.int32, logits.shape, 1).astype(F32)
        valid = lane < N_EXPERTS
        logits = jnp.where(valid, logits, -jnp.inf)
        mx = jnp.max(logits, axis=-1, keepdims=True)
        e = jnp.exp(logits - mx)
        probs = e / jnp.sum(e, axis=-1, keepdims=True)
        p1 = jnp.max(probs, axis=-1, keepdims=True)
        i1 = jnp.min(jnp.where((probs == p1) & valid, lane, float(LANES)), axis=-1, keepdims=True)
        rest = jnp.where((lane == i1) | (lane >= N_EXPERTS), -1.0, probs)
        p2 = jnp.max(rest, axis=-1, keepdims=True)
        i2 = jnp.min(jnp.where(rest == p2, lane, float(LANES)), axis=-1, keepdims=True)
        tot = p1 + p2
        comb = jnp.where(lane == i1, p1 / tot, jnp.where(lane == i2, p2 / tot, 0.0))
        comb_ref[...] = comb
        cnt = jnp.sum(jnp.where(comb > 0.0, 1.0, 0.0), axis=0, keepdims=True)
        cnt_ref[...] = jnp.broadcast_to(cnt, cnt_ref.shape)


def _norm_mod(x, nw, mod, *, sh_off, sc_off, seg_fn, tr, router_w=None, name):
    t, d = x.shape
    tr = _tile(t, tr)
    router = router_w is not None
    in_specs = [pl.BlockSpec((tr, d), lambda i: (i, 0)),
                pl.BlockSpec((1, d), lambda i: (0, 0)),
                pl.BlockSpec(mod.shape, lambda i: (0, 0))]
    args = [x, nw, mod]
    out_specs = [pl.BlockSpec((tr, d), lambda i: (i, 0))]
    out_shape = [jax.ShapeDtypeStruct((t, d), BF16)]
    if router:
        in_specs.append(pl.BlockSpec(router_w.shape, lambda i: (0, 0)))
        args.append(router_w)
        out_specs.append(pl.BlockSpec((tr, LANES), lambda i: (i, 0)))
        out_shape.append(jax.ShapeDtypeStruct((t, LANES), F32))
        out_specs.append(pl.BlockSpec((SUBLANES, LANES), lambda i: (i, 0)))
        out_shape.append(jax.ShapeDtypeStruct((t // tr * SUBLANES, LANES), F32))
    res = pl.pallas_call(
        functools.partial(_norm_mod_kernel, d=d, sh_off=sh_off, sc_off=sc_off, seg=seg_fn(tr),
                          router=router),
        grid=(t // tr,), in_specs=in_specs, out_specs=out_specs, out_shape=out_shape,
        compiler_params=_cparams(("parallel",)), name=name,
    )(*args)
    return res if router else res[0]


def _norm_mod_join_kernel(ctx_ref, lat_ref, nw_ref, mod_ref, out_ref, xs_ref, *, d, sh_off, sc_off, seg):
    i = pl.program_id(0)
    row = _mod_row(i, *seg)
    x = jnp.where(i < seg[0], ctx_ref[...], lat_ref[...])
    xs_ref[...] = x
    y = x * lax.rsqrt(jnp.mean(x * x, axis=-1, keepdims=True) + EPS) * nw_ref[...]
    sc = mod_ref[pl.ds(row, 1), sc_off:sc_off + d]
    sh = mod_ref[pl.ds(row, 1), sh_off:sh_off + d]
    out_ref[...] = (y * (1.0 + sc) + sh).astype(BF16)


def _norm_mod_join(ctx2d, lat2d, nw, mod, *, sh_off, sc_off, seg_fn, tr):
    d = ctx2d.shape[1]
    t = ctx2d.shape[0] + lat2d.shape[0]
    seg = seg_fn(tr)
    nct = seg[0]
    return pl.pallas_call(
        functools.partial(_norm_mod_join_kernel, d=d, sh_off=sh_off, sc_off=sc_off, seg=seg),
        grid=(t // tr,),
        in_specs=[pl.BlockSpec((tr, d), lambda i: (jnp.minimum(i, nct - 1), 0)),
                  pl.BlockSpec((tr, d), lambda i: (jnp.maximum(i - nct, 0), 0)),
                  pl.BlockSpec((1, d), lambda i: (0, 0)),
                  pl.BlockSpec(mod.shape, lambda i: (0, 0))],
        out_specs=[pl.BlockSpec((tr, d), lambda i: (i, 0)), pl.BlockSpec((tr, d), lambda i: (i, 0))],
        out_shape=[jax.ShapeDtypeStruct((t, d), BF16), jax.ShapeDtypeStruct((t, d), F32)],
        compiler_params=_cparams(("arbitrary",)), name="norm1_join",
    )(ctx2d, lat2d, nw, mod)


def _final_norm_kernel(x_ref, w_ref, out_ref):
    x = x_ref[...]
    out_ref[...] = x * lax.rsqrt(jnp.mean(x * x, axis=-1, keepdims=True) + EPS) * w_ref[...]


def _final_norm(x, w, *, row0, rows, tr):
    d = x.shape[1]
    tr = _tile(math.gcd(row0, rows), tr)
    off = row0 // tr
    return pl.pallas_call(
        _final_norm_kernel, grid=(rows // tr,),
        in_specs=[pl.BlockSpec((tr, d), lambda i: (i + off, 0)),
                  pl.BlockSpec((1, d), lambda i: (0, 0))],
        out_specs=pl.BlockSpec((tr, d), lambda i: (i, 0)),
        out_shape=jax.ShapeDtypeStruct((rows, d), F32),
        compiler_params=_cparams(("parallel",)), name="final_norm",
    )(x, w)


def _conv_kernel(main_ref, prev_ref, next_ref, w_ref, b_ref, out_ref, z_ref, *,
                 tc, ctx_len, n_ctx_tiles, tiles_per_img):
    i = pl.program_id(0)
    is_ctx = i < n_ctx_tiles
    li = i - n_ctx_tiles
    first = (li % tiles_per_img) == 0
    last = (li % tiles_per_img) == tiles_per_img - 1
    z_ref[0:GRID_W, :] = jnp.where(is_ctx | first, 0.0, prev_ref[...])
    z_ref[GRID_W:GRID_W + tc, :] = main_ref[...]
    z_ref[GRID_W + tc:, :] = jnp.where(is_ctx | last, 0.0, next_ref[...])
    w = w_ref[...]

    def column_sum(dc):
        s = w[3 + dc:4 + dc, :] * z_ref[GRID_W:GRID_W + tc, :]
        for dr in (0, 2):
            wt = jnp.where(is_ctx, 0.0, w[3 * dr + dc:3 * dr + dc + 1, :])
            s = s + wt * z_ref[GRID_W * dr:GRID_W * dr + tc, :]
        return s

    pos = lax.broadcasted_iota(jnp.int32, (tc, 1), 0)
    col = jnp.where(is_ctx, pos % ctx_len, pos % GRID_W)
    width = jnp.where(is_ctx, ctx_len, GRID_W)
    left = jnp.where(col != 0, pltpu.roll(column_sum(0), 1, 0), 0.0)
    right = jnp.where(col != width - 1, pltpu.roll(column_sum(2), tc - 1, 0), 0.0)
    out_ref[...] = _silu(b_ref[...] + column_sum(1) + left + right)


def _conv(p_main, conv_w9, conv_b, *, n_conv, tc, ctx_len, n_ctx_tiles, tiles_per_img, cb):
    t = p_main.shape[0]
    cb = _tile(n_conv, cb)
    rpt = tc // GRID_W
    nrow = t // GRID_W
    return pl.pallas_call(
        functools.partial(_conv_kernel, tc=tc, ctx_len=ctx_len, n_ctx_tiles=n_ctx_tiles,
                          tiles_per_img=tiles_per_img),
        grid=(t // tc, n_conv // cb),
        in_specs=[pl.BlockSpec((tc, cb), lambda i, j: (i, j)),
                  pl.BlockSpec((GRID_W, cb), lambda i, j: (jnp.maximum(i * rpt - 1, 0), j)),
                  pl.BlockSpec((GRID_W, cb), lambda i, j: (jnp.minimum((i + 1) * rpt, nrow - 1), j)),
                  pl.BlockSpec((9, cb), lambda i, j: (0, j)),
                  pl.BlockSpec((1, cb), lambda i, j: (0, j))],
        out_specs=pl.BlockSpec((tc, cb), lambda i, j: (i, j)),
        out_shape=jax.ShapeDtypeStruct((t, n_conv), F32),
        scratch_shapes=[pltpu.VMEM((tc + 2 * GRID_W, cb), F32)],
        compiler_params=_cparams(("parallel", "parallel")), name="conv_silu",
    )(p_main, p_main, p_main, conv_w9, conv_b)


def _rowblk(b, s, rev, ncb, nlb, nb):
    if rev:
        ctx = b * ncb + (ncb - 1 - s)
        lat = nb * ncb + b * nlb + (nlb - 1 - (s - ncb))
    else:
        ctx = b * ncb + s
        lat = nb * ncb + b * nlb + (s - ncb)
    return jnp.where(s < ncb, ctx, lat)


def _tri(n, rev):
    r = lax.broadcasted_iota(jnp.int32, (n, n), 0)
    c = lax.broadcasted_iota(jnp.int32, (n, n), 1)
    return (c >= r) if rev else (c <= r)


def _level_ref(bl, half, rev):
    tb, dk = bl.shape
    blk = 2 * half
    idx = half if rev else half - 1
    if blk == tb:
        return bl[idx:idx + 1, :]
    if blk >= SUBLANES:
        b3 = bl.reshape(tb // blk, blk, dk)
        return jnp.broadcast_to(b3[:, idx:idx + 1, :], b3.shape).reshape(tb, dk)
    tmod = lax.broadcasted_iota(jnp.int32, (tb, 1), 0) % blk
    r = bl
    for m in range(blk):
        if idx != m:
            r = jnp.where(tmod == m, pltpu.roll(bl, (m - idx) % tb, 0), r)
    return r


def _hgrn_dir(q_ref, u_ref, v_ref, lb_ref, o_ref, st_ref, d, rev, tb, hps):
    ri = lax.broadcasted_iota(jnp.int32, (tb, tb), 0)
    cj = lax.broadcasted_iota(jnp.int32, (tb, tb), 1)
    keep = jnp.where((cj >= ri) if rev else (cj <= ri), 1.0, 0.0).astype(BF16)
    half = tb // 2
    ri = lax.broadcasted_iota(jnp.int32, (half, half), 0)
    cj = lax.broadcasted_iota(jnp.int32, (half, half), 1)
    level = jnp.where((cj > ri) if rev else (cj < ri), 31 - lax.clz(ri ^ cj), -1)
    for h in range(hps):
        _hgrn_head(q_ref, u_ref, v_ref, lb_ref, o_ref, st_ref, d, h, rev, tb, keep, level)


def _hgrn_head(q_ref, u_ref, v_ref, lb_ref, o_ref, st_ref, d, h, rev, tb, keep, level):
    dk = HGRN_DK
    cols = slice(h * dk, (h + 1) * dk)
    lbv = lb_ref[:, cols]
    qraw = q_ref[:, cols]
    u = u_ref[:, cols]
    v = v_ref[:, cols]
    q = _silu(qraw) * dk ** -0.5
    la = jnp.log(lbv)
    lc = jnp.log1p(-lbv) + _log_sigmoid(u)
    mx = jnp.maximum(la, lc)
    mn = jnp.minimum(la, lc)
    logf = mx + jnp.log1p(jnp.exp(mn - mx))
    k = (1.0 - lbv) * _sigmoid(-u)

    bl = _dot_exact(keep, logf * LOG2E)
    b_end = bl[0:1] if rev else bl[tb - 1:tb]
    half = tb // 2
    halves = (slice(0, half), slice(half, tb))
    scores = [jnp.zeros((half, half), F32), jnp.zeros((half, half), F32)]
    for lv in range(half.bit_length() - 1):
        dq = bl - _level_ref(bl, 1 << lv, rev)
        qt = (q * jnp.exp2(dq)).astype(BF16)
        kt = (k * jnp.exp2(-dq)).astype(BF16)
        for hb, rows in enumerate(halves):
            scores[hb] = jnp.where(level == lv, _dot_nt(qt[rows], kt[rows]), scores[hb])
    isl, jsl = (halves[0], halves[1]) if rev else (halves[1], halves[0])
    dq = bl - _level_ref(bl, half, rev)
    cross = _dot_nt((q[isl] * jnp.exp2(dq[isl])).astype(BF16), (k[jsl] * jnp.exp2(-dq[jsl])).astype(BF16))
    vb = v.astype(BF16)
    st = st_ref[d, h]
    rest = (jnp.sum(q * k, axis=-1, keepdims=True) * v
            + _dot_nt((q * jnp.exp2(bl)).astype(BF16), st.astype(BF16)))
    for hb, rows in enumerate(halves):
        o = _dot(scores[hb].astype(BF16), vb[rows]) + rest[rows]
        if rows == isl:
            o = o + _dot(cross.astype(BF16), vb[jsl])
        o_ref[rows, cols] = o
    st_ref[d, h] = st * jnp.exp2(b_end) + _dot(v.T.astype(BF16), (k * jnp.exp2(b_end - bl)).astype(BF16))


def _hgrn_kernel(qf, uf, vf, qb, ub, vb, lb_ref, of, ob, st_ref, *, tb, hps):
    @pl.when(pl.program_id(2) == 0)
    def _():
        st_ref[...] = jnp.zeros_like(st_ref)
    _hgrn_dir(qf, uf, vf, lb_ref, of, st_ref, 0, False, tb, hps)
    _hgrn_dir(qb, ub, vb, lb_ref, ob, st_ref, 1, True, tb, hps)


def _hgrn_scan(p_main, lb_row, *, nb, ncb, nlb, tb, heads, off_q, off_ff, off_fb, off_i):
    t = p_main.shape[0]
    dk = HGRN_DK

    hps = HGRN_HEADS_PER_STEP
    wd = hps * dk

    def spec(off, rev):
        return pl.BlockSpec((tb, wd), lambda b, h, s: (_rowblk(b, s, rev, ncb, nlb, nb), off // wd + h))

    def ospec(rev):
        return pl.BlockSpec((tb, wd), lambda b, h, s: (_rowblk(b, s, rev, ncb, nlb, nb), h))

    out = jax.ShapeDtypeStruct((t, heads * dk), F32)
    return pl.pallas_call(
        functools.partial(_hgrn_kernel, tb=tb, hps=hps),
        grid=(nb, heads // hps, ncb + nlb),
        in_specs=[spec(off_q, False), spec(off_ff, False), spec(off_i, False),
                  spec(off_q, True), spec(off_fb, True), spec(off_i, True),
                  pl.BlockSpec((1, wd), lambda b, h, s: (0, h))],
        out_specs=[ospec(False), ospec(True)],
        out_shape=[out, out],
        scratch_shapes=[pltpu.VMEM((2, hps, dk, dk), F32)],
        compiler_params=_cparams(("parallel", "parallel", "arbitrary")), name="hgrn_scan",
    )(p_main, p_main, p_main, p_main, p_main, p_main, lb_row)


def _mlstm_dir(q_ref, k_ref, v_ref, g_ref, bias_ref, o_ref, c_ref, n_ref, m_ref, d, rev, tb, dk, dv):
    g = g_ref[...] + bias_ref[...]
    lf = _log_sigmoid(g)
    keep = _tri(tb, rev)
    b = _dot_exact(jnp.where(keep, 1.0, 0.0).astype(BF16), lf)
    bt, gt = b.T, g.T
    for h in range(MLSTM_HEADS):
        ci = 2 * MLSTM_HEADS * d + h
        cf = ci + MLSTM_HEADS
        q = q_ref[:, h * dk:(h + 1) * dk]
        ks = k_ref[:, h * dk:(h + 1) * dk] * dk ** -0.5
        v = v_ref[:, h * dv:(h + 1) * dv]
        b_col, ic_col = b[:, cf:cf + 1], g[:, ci:ci + 1]
        b_row, ic_row = bt[cf:cf + 1, :], gt[ci:ci + 1, :]
        b_end = b_col[0:1] if rev else b_col[tb - 1:tb]
        m_prev = m_ref[d, h][:, 0:1]
        w_in = jnp.where(keep, b_col - b_row + ic_row, -jnp.inf)
        w_st = b_col + m_prev
        m_row = jnp.maximum(jnp.max(w_in, axis=1, keepdims=True), w_st)
        qb = q.astype(BF16)
        vb = v.astype(BF16)
        p = jnp.exp(w_in - m_row) * _dot_nt(qb, ks.astype(BF16))
        e_st = jnp.exp(w_st - m_row)
        num = _dot(p.astype(BF16), vb) + e_st * _dot(qb, c_ref[d, h].astype(BF16))
        nrm = (jnp.sum(p, axis=1, keepdims=True)
               + e_st * jnp.sum(q * n_ref[d, h], axis=1, keepdims=True))
        o_ref[:, h * dv:(h + 1) * dv] = num / jnp.maximum(jnp.abs(nrm), jnp.exp(-m_row))
        w_end = b_end - b_col + ic_col
        m_new = jnp.maximum(b_end + m_prev, jnp.max(w_end, axis=0, keepdims=True))
        s_old = jnp.exp(b_end + m_prev - m_new)
        kt = ks * jnp.exp(w_end - m_new)
        c_ref[d, h] = s_old * c_ref[d, h] + _dot(kt.T.astype(BF16), vb)
        n_ref[d, h] = s_old * n_ref[d, h] + jnp.sum(kt, axis=0, keepdims=True)
        m_ref[d, h] = jnp.broadcast_to(m_new, m_ref.shape[2:])


def _mlstm_kernel(qf, kf, vf, gf, qb, kb, vb, gb, bias_ref, of, ob, c_ref, n_ref, m_ref, *, tb, dk, dv):
    @pl.when(pl.program_id(1) == 0)
    def _():
        c_ref[...] = jnp.zeros_like(c_ref)
        n_ref[...] = jnp.zeros_like(n_ref)
        m_ref[...] = jnp.zeros_like(m_ref)
    _mlstm_dir(qf, kf, vf, gf, bias_ref, of, c_ref, n_ref, m_ref, 0, False, tb, dk, dv)
    _mlstm_dir(qb, kb, vb, gb, bias_ref, ob, c_ref, n_ref, m_ref, 1, True, tb, dk, dv)


def _mlstm_scan(conv, p_main, p_tail, bias_row, *, nb, ncb, nlb, tb, dk, dv,
                off_q, off_k, off_v, off_small):
    t = conv.shape[0]

    nh = MLSTM_HEADS

    def spec(width, off, rev):
        return pl.BlockSpec((tb, width), lambda b, s: (_rowblk(b, s, rev, ncb, nlb, nb), off // width))

    def ospec(rev):
        return pl.BlockSpec((tb, nh * dv), lambda b, s: (_rowblk(b, s, rev, ncb, nlb, nb), 0))

    out = jax.ShapeDtypeStruct((t, nh * dv), F32)
    return pl.pallas_call(
        functools.partial(_mlstm_kernel, tb=tb, dk=dk, dv=dv),
        grid=(nb, ncb + nlb),
        in_specs=[spec(nh * dk, off_q, False), spec(nh * dk, off_k, False), spec(nh * dv, off_v, False),
                  spec(LANES, off_small, False),
                  spec(nh * dk, off_q, True), spec(nh * dk, off_k, True), spec(nh * dv, off_v, True),
                  spec(LANES, off_small, True),
                  pl.BlockSpec((1, LANES), lambda b, s: (0, 0))],
        out_specs=[ospec(False), ospec(True)],
        out_shape=[out, out],
        scratch_shapes=[pltpu.VMEM((2, nh, dk, dv), F32), pltpu.VMEM((2, nh, 1, dk), F32),
                        pltpu.VMEM((2, nh, 1, LANES), F32)],
        compiler_params=_cparams(("parallel", "arbitrary")), name="mlstm_scan",
    )(conv, conv, p_main, p_tail, conv, conv, p_main, p_tail, bias_row)


def _ssd_dir(c_ref, bm_ref, x_ref, g_ref, dtb_ref, alog_ref, o_ref, s_ref, d, rev, tb, heads, hpg):
    dt = _softplus(g_ref[...] + dtb_ref[...])
    la = -jnp.exp(alog_ref[...]) * dt
    keep = _tri(tb, rev)
    b = _dot_exact(jnp.where(keep, 1.0, 0.0).astype(BF16), la * LOG2E)
    bt = b.T
    width = hpg * SSD_HEADDIM
    hl = lax.broadcasted_iota(jnp.int32, (1, width), 1) // SSD_HEADDIM
    for grp in range(SSD_GROUPS):
        lane0 = MLSTM_HEADS * 4 + heads * d + hpg * grp
        cm = c_ref[:, grp * SSD_STATE:(grp + 1) * SSD_STATE].astype(BF16)
        bm = bm_ref[:, grp * SSD_STATE:(grp + 1) * SSD_STATE]
        x = x_ref[:, grp * width:(grp + 1) * width]
        b_all = jnp.zeros((tb, width), F32)
        dt_all = jnp.zeros((tb, width), F32)
        for i in range(hpg):
            b_all = jnp.where(hl == i, b[:, lane0 + i:lane0 + i + 1], b_all)
            dt_all = jnp.where(hl == i, dt[:, lane0 + i:lane0 + i + 1], dt_all)
        b_end = b_all[0:1] if rev else b_all[tb - 1:tb]
        v_all = x * dt_all
        v_bf = v_all.astype(BF16)
        gmat = _dot_nt(cm, bm.astype(BF16))
        s_old = s_ref[d, grp]
        inter = jnp.exp2(b_all) * _dot(cm, s_old.astype(BF16))
        outs = []
        for i in range(hpg):
            c = lane0 + i
            dec = jnp.where(keep, jnp.exp2(b[:, c:c + 1] - bt[c:c + 1, :]), 0.0)
            outs.append(_dot((gmat * dec).astype(BF16), v_bf[:, i * SSD_HEADDIM:(i + 1) * SSD_HEADDIM]))
        o_ref[:, grp * width:(grp + 1) * width] = jnp.concatenate(outs, axis=1) + inter
        sv = (v_all * jnp.exp2(b_end - b_all)).astype(BF16)
        s_ref[d, grp] = jnp.exp2(b_end) * s_old + _dot(bm.T.astype(BF16), sv)


def _ssd_kernel(cf, bf, xf, gf, cb, bb, xb, gb, dtb_ref, alog_ref, of, ob, s_ref, *, tb, heads, hpg):
    @pl.when(pl.program_id(1) == 0)
    def _():
        s_ref[...] = jnp.zeros_like(s_ref)
    _ssd_dir(cf, bf, xf, gf, dtb_ref, alog_ref, of, s_ref, 0, False, tb, heads, hpg)
    _ssd_dir(cb, bb, xb, gb, dtb_ref, alog_ref, ob, s_ref, 1, True, tb, heads, hpg)


def _ssd_scan(conv, p_tail, dtb_row, alog_row, *, nb, ncb, nlb, tb, heads, off_x, off_b, off_c, off_small):
    t = conv.shape[0]
    hpg = heads // SSD_GROUPS
    width = hpg * SSD_HEADDIM

    ng = SSD_GROUPS

    def spec(w, off, rev):
        return pl.BlockSpec((tb, w), lambda b, s: (_rowblk(b, s, rev, ncb, nlb, nb), off // w))

    def ospec(rev):
        return pl.BlockSpec((tb, ng * width), lambda b, s: (_rowblk(b, s, rev, ncb, nlb, nb), 0))

    row = pl.BlockSpec((1, LANES), lambda b, s: (0, 0))
    out = jax.ShapeDtypeStruct((t, heads * SSD_HEADDIM), F32)
    return pl.pallas_call(
        functools.partial(_ssd_kernel, tb=tb, heads=heads, hpg=hpg),
        grid=(nb, ncb + nlb),
        in_specs=[spec(ng * SSD_STATE, off_c, False), spec(ng * SSD_STATE, off_b, False),
                  spec(ng * width, off_x, False), spec(LANES, off_small, False),
                  spec(ng * SSD_STATE, off_c, True), spec(ng * SSD_STATE, off_b, True),
                  spec(ng * width, off_x, True), spec(LANES, off_small, True), row, row],
        out_specs=[ospec(False), ospec(True)],
        out_shape=[out, out],
        scratch_shapes=[pltpu.VMEM((2, ng, SSD_STATE, width), F32)],
        compiler_params=_cparams(("parallel", "arbitrary")), name="ssd_scan",
    )(conv, conv, conv, p_tail, conv, conv, conv, p_tail, dtb_row, alog_row)


def _headnorm_kernel(of_ref, ob_ref, g_ref, w_ref, out_ref, *, hd):
    cb = of_ref.shape[1]
    for i in range(cb // hd):
        sl = slice(i * hd, (i + 1) * hd)
        o = of_ref[:, sl] + ob_ref[:, sl]
        y = o * lax.rsqrt(jnp.mean(o * o, axis=-1, keepdims=True) + EPS) * w_ref[:, sl]
        out_ref[:, sl] = (y * _silu(g_ref[:, sl])).astype(BF16)


def _headnorm(of, ob, gsrc, goff, w, *, hd, tr, cb, name):
    t, wd = of.shape
    tr, cb = _tile(t, tr), _tile(wd, cb)
    gblk = goff // cb
    return pl.pallas_call(
        functools.partial(_headnorm_kernel, hd=hd), grid=(t // tr, wd // cb),
        in_specs=[pl.BlockSpec((tr, cb), lambda i, j: (i, j)),
                  pl.BlockSpec((tr, cb), lambda i, j: (i, j)),
                  pl.BlockSpec((tr, cb), lambda i, j: (i, gblk + j)),
                  pl.BlockSpec((1, cb), lambda i, j: (0, j))],
        out_specs=pl.BlockSpec((tr, cb), lambda i, j: (i, j)),
        out_shape=jax.ShapeDtypeStruct((t, wd), BF16),
        compiler_params=_cparams(("parallel", "parallel")), name=name,
    )(of, ob, gsrc, w)


def _ssd_finish_kernel(of_ref, ob_ref, x_ref, z_ref, d_ref, w_ref, out_ref):
    y = of_ref[...] + ob_ref[...] + d_ref[...] * x_ref[...]
    tt = y * _silu(z_ref[...])
    out_ref[...] = (tt * lax.rsqrt(jnp.mean(tt * tt, axis=-1, keepdims=True) + EPS)
                    * w_ref[...]).astype(BF16)


def _ssd_finish(of, ob, conv, off_x, p_tail, off_z, d_row, w, *, tr):
    t, wd = of.shape
    tr = _tile(t, tr)
    row = pl.BlockSpec((1, wd), lambda i: (0, 0))
    return pl.pallas_call(
        _ssd_finish_kernel, grid=(t // tr,),
        in_specs=[pl.BlockSpec((tr, wd), lambda i: (i, 0)),
                  pl.BlockSpec((tr, wd), lambda i: (i, 0)),
                  pl.BlockSpec((tr, wd), lambda i: (i, off_x // wd)),
                  pl.BlockSpec((tr, wd), lambda i: (i, off_z // wd)), row, row],
        out_specs=pl.BlockSpec((tr, wd), lambda i: (i, 0)),
        out_shape=jax.ShapeDtypeStruct((t, wd), BF16),
        compiler_params=_cparams(("parallel",)), name="ssd_finish",
    )(of, ob, conv, p_tail, d_row, w)


def _merge_kernel(y0, y1, y2, bw_ref, mg_ref, gw_ref, gb_ref, out_ref, acc_ref):
    k = pl.program_id(2)
    gate = _sigmoid(_dot(mg_ref[...].astype(BF16), gw_ref[...]) + gb_ref[...])
    for idx, y_ref in enumerate((y0, y1, y2)):
        @pl.when(k == idx)
        def _(y_ref=y_ref, idx=idx):
            r = gate * _dot(y_ref[...], bw_ref[...])
            if idx == 0:
                acc_ref[...] = r
            elif idx == 1:
                acc_ref[...] += r
            else:
                out_ref[...] = (acc_ref[...] + r).astype(BF16)


def _merge(ys, bw, p_tail, off_merge, rank, gw, gb, *, tm, tn):
    t, bwid = ys[0].shape
    d = bw.shape[-1]
    tm, tn = _tile(t, tm), _tile(d, tn)
    nj = d // tn
    yspec = pl.BlockSpec((tm, bwid), lambda i, j, k: (i, 0))
    return pl.pallas_call(
        _merge_kernel, grid=(t // tm, nj, 3),
        in_specs=[yspec, yspec, yspec,
                  pl.BlockSpec((None, bwid, tn), lambda i, j, k: (k, 0, j)),
                  pl.BlockSpec((tm, rank), lambda i, j, k: (i, off_merge // rank)),
                  pl.BlockSpec((rank, tn), lambda i, j, k: (0, k * nj + j)),
                  pl.BlockSpec((1, tn), lambda i, j, k: (0, k * nj + j))],
        out_specs=pl.BlockSpec((tm, tn), lambda i, j, k: (i, j)),
        out_shape=jax.ShapeDtypeStruct((t, d), BF16),
        scratch_shapes=[pltpu.VMEM((tm, tn), F32)],
        compiler_params=_cparams(("parallel", "parallel", "arbitrary")), name="branch_merge",
    )(ys[0], ys[1], ys[2], bw, p_tail, gw, gb)


def _swiglu_kernel(a_ref, w1_ref, w3_ref, out_ref, *accs, nk):
    def part():
        a = a_ref[...]
        return _dot(a, w1_ref[...]), _dot(a, w3_ref[...])

    def finish(r1, r3):
        out_ref[...] = (_silu(r1) * r3).astype(BF16)

    _accumulate(pl.program_id(2), nk, part, accs, finish)


def _swiglu_up(a, w1, w3, *, tm, tn, tk, name):
    t, kdim = a.shape
    n = w1.shape[1]
    tm, tn, tk = _tile(t, tm), _tile(n, tn), _tile(kdim, tk)
    nk = kdim // tk
    wspec = pl.BlockSpec((tk, tn), lambda i, j, k: (k, j))
    return pl.pallas_call(
        functools.partial(_swiglu_kernel, nk=nk),
        grid=(t // tm, n // tn, nk),
        in_specs=[pl.BlockSpec((tm, tk), lambda i, j, k: (i, k)), wspec, wspec],
        out_specs=pl.BlockSpec((tm, tn), lambda i, j, k: (i, j)),
        out_shape=jax.ShapeDtypeStruct((t, n), BF16),
        scratch_shapes=[pltpu.VMEM((tm, tn), F32)] * (2 if nk > 1 else 0),
        compiler_params=_cparams(("parallel", "parallel", "arbitrary")), name=name,
    )(a, w1, w3)


MOE_RB = LANES
MOE_MB = 512


def _moe_tables(cnt_tile, tm):
    nt, ne = cnt_tile.shape
    rb, bpm = MOE_RB, MOE_MB // MOE_RB
    ni = 2 * tm // rb + ne
    nblk = -(-(nt * ni + ne * (bpm - 1)) // bpm) * bpm
    nbk = (cnt_tile + (rb - 1)) // rb
    reg_e = (jnp.sum(nbk, axis=0) + (bpm - 1)) // bpm * bpm
    end_e = jnp.cumsum(reg_e)
    start_e = end_e - reg_e
    pre_ie = jnp.cumsum(nbk, axis=0) - nbk
    ends_ie = jnp.cumsum(nbk, axis=1)
    off_ie = ends_ie - nbk
    n_items = ends_ie[:, -1]
    it = jnp.arange(ni, dtype=jnp.int32)[None, :]
    itc = jnp.minimum(it, n_items[:, None] - 1)
    e_idx = jnp.sum((itc[:, :, None] >= ends_ie[:, None, :]).astype(jnp.int32), axis=-1)
    e_idx = jnp.minimum(e_idx, ne - 1)
    chunk = itc - jnp.take_along_axis(off_ie, e_idx, axis=1)
    gblk = start_e[e_idx] + jnp.take_along_axis(pre_ie, e_idx, axis=1) + chunk
    iout = jnp.where(it < n_items[:, None], gblk, nblk)
    m = jnp.arange(nblk // bpm, dtype=jnp.int32)
    mexp = jnp.sum((m[:, None] * bpm >= end_e[None, :]).astype(jnp.int32), axis=-1)
    off_row = jnp.zeros((nt, 1, LANES), F32).at[:, 0, :ne].set((off_ie * rb).astype(F32))
    return dict(ni=ni, nblk=nblk, iout=iout.reshape(-1).astype(jnp.int32),
                iblk=gblk.reshape(-1).astype(jnp.int32), n_items=n_items.astype(jnp.int32),
                mexp=jnp.minimum(mexp, ne - 1).astype(jnp.int32),
                nvalid=(end_e[-1:] // bpm).astype(jnp.int32), off_row=off_row)


def _moe_gather_kernel(iout_ref, h_ref, comb_ref, off_ref, za_ref, zw_ref,
                       a_ref, w_ref, l0_ref, l1_ref, lt_ref, wt_ref, *, tm, rb):
    del iout_ref, za_ref, zw_ref
    it = pl.program_id(1)

    @pl.when(it == 0)
    def _():
        comb = comb_ref[...]
        pick = comb > 0.0
        r = lax.broadcasted_iota(jnp.int32, (tm, tm), 0)
        c = lax.broadcasted_iota(jnp.int32, (tm, tm), 1)
        rank = _dot(jnp.where(c < r, 1.0, 0.0).astype(BF16), jnp.where(pick, 1.0, 0.0).astype(BF16))
        loc = off_ref[...] + rank
        lane = lax.broadcasted_iota(jnp.int32, comb.shape, 1).astype(F32)
        m1 = jnp.min(jnp.where(pick, lane, float(LANES)), axis=1, keepdims=True)
        first = pick & (lane == m1)
        second = pick & (lane != m1)
        l0 = jnp.sum(jnp.where(first, loc, 0.0), axis=1, keepdims=True)
        l1 = jnp.sum(jnp.where(second, loc + 1.0, 0.0), axis=1, keepdims=True) - 1.0
        l0b = jnp.broadcast_to(l0, comb.shape)
        l1b = jnp.broadcast_to(l1, comb.shape)
        l0_ref[...] = l0b
        l1_ref[...] = l1b
        lt_ref[0] = l0b.T
        lt_ref[1] = l1b.T
        wt_ref[0] = jnp.broadcast_to(jnp.sum(jnp.where(first, comb, 0.0), axis=1, keepdims=True), comb.shape)
        wt_ref[1] = jnp.broadcast_to(jnp.sum(jnp.where(second, comb, 0.0), axis=1, keepdims=True), comb.shape)

    s = (it * rb + lax.broadcasted_iota(jnp.int32, (rb, 1), 0)).astype(F32)
    p0 = jnp.where(lt_ref[0, 0:1, :] == s, 1.0, 0.0).astype(BF16)
    p1 = jnp.where(lt_ref[1, 0:1, :] == s, 1.0, 0.0).astype(BF16)
    a_ref[...] = _dot(p0 + p1, h_ref[...]).astype(BF16)
    w_ref[...] = _dot_exact(p0, wt_ref[0]) + _dot_exact(p1, wt_ref[1])


def _moe_gather(h, comb, tab, *, tm):
    t, d = h.shape
    rb, ni = MOE_RB, tab['ni']
    nrows = tab['nblk'] * rb + MOE_MB
    grid_spec = pltpu.PrefetchScalarGridSpec(
        num_scalar_prefetch=1, grid=(t // tm, ni),
        in_specs=[pl.BlockSpec((tm, d), lambda i, it, io: (i, 0)),
                  pl.BlockSpec((tm, LANES), lambda i, it, io: (i, 0)),
                  pl.BlockSpec((None, 1, LANES), lambda i, it, io: (i, 0, 0)),
                  pl.BlockSpec(memory_space=pl.ANY), pl.BlockSpec(memory_space=pl.ANY)],
        out_specs=[pl.BlockSpec((rb, d), lambda i, it, io: (io[i * ni + it], 0)),
                   pl.BlockSpec((rb, LANES), lambda i, it, io: (io[i * ni + it], 0)),
                   pl.BlockSpec((tm, LANES), lambda i, it, io: (i, 0)),
                   pl.BlockSpec((tm, LANES), lambda i, it, io: (i, 0))],
        scratch_shapes=[pltpu.VMEM((2, LANES, tm), F32), pltpu.VMEM((2, tm, LANES), F32)])
    return pl.pallas_call(
        functools.partial(_moe_gather_kernel, tm=tm, rb=rb), grid_spec=grid_spec,
        out_shape=[jax.ShapeDtypeStruct((nrows, d), BF16), jax.ShapeDtypeStruct((nrows, LANES), F32),
                   jax.ShapeDtypeStruct((t, LANES), F32), jax.ShapeDtypeStruct((t, LANES), F32)],
        input_output_aliases={4: 0, 5: 1},
        compiler_params=_cparams(("parallel", "arbitrary")), name="moe_gather",
    )(tab['iout'], h, comb, tab['off_row'], jnp.zeros((nrows, d), BF16), jnp.zeros((nrows, LANES), F32))


def _moe_grouped_kernel(*refs, nk, dual):
    if dual:
        mexp_ref, nv_ref, a_ref, w1_ref, w3_ref, ws_ref, out_ref, *accs = refs
    else:
        mexp_ref, nv_ref, a_ref, w1_ref, out_ref, *accs = refs
    del mexp_ref
    k = pl.program_id(2)

    def part():
        a = a_ref[...]
        if dual:
            return _dot(a, w1_ref[...]), _dot(a, w3_ref[...])
        return (_dot(a, w1_ref[...]),)

    def finish(r1, r3=None):
        if dual:
            out_ref[...] = (_silu(r1) * r3 * ws_ref[:, 0:1]).astype(BF16)
        else:
            out_ref[...] = r1.astype(BF16)

    @pl.when(pl.program_id(0) < nv_ref[0])
    def _():
        _accumulate(k, nk, part, accs, finish)


def _moe_grouped(a, ws, w_sorted, tab, *, tn, tk, name):
    rows, kdim = a.shape
    n = ws[0].shape[-1]
    tn, tk = _tile(n, tn), _tile(kdim, tk)
    nj, nk = n // tn, kdim // tk
    nmb = tab['nblk'] * MOE_RB // MOE_MB
    dual = len(ws) == 2

    def live(m, nv):
        return m < nv[0]

    def me(m, nv):
        return jnp.minimum(m, nv[0] - 1)

    a_spec = pl.BlockSpec((MOE_MB, tk), lambda m, j, k, ex, nv: (me(m, nv), jnp.where(live(m, nv), k, nk - 1)))
    w_spec = pl.BlockSpec((None, tk, tn), lambda m, j, k, ex, nv: (
        ex[me(m, nv)], jnp.where(live(m, nv), k, nk - 1), jnp.where(live(m, nv), j, nj - 1)))
    o_spec = pl.BlockSpec((MOE_MB, tn), lambda m, j, k, ex, nv: (me(m, nv), jnp.where(live(m, nv), j, nj - 1)))
    in_specs = [a_spec, w_spec]
    args = [a, ws[0]]
    if dual:
        in_specs += [w_spec, pl.BlockSpec((MOE_MB, LANES), lambda m, j, k, ex, nv: (me(m, nv), 0))]
        args += [ws[1], w_sorted]
    scratch = [pltpu.VMEM((MOE_MB, tn), F32)] * (len(ws) if nk > 1 else 0)
    grid_spec = pltpu.PrefetchScalarGridSpec(
        num_scalar_prefetch=2, grid=(nmb, nj, nk), in_specs=in_specs, out_specs=o_spec,
        scratch_shapes=scratch)
    return pl.pallas_call(
        functools.partial(_moe_grouped_kernel, nk=nk, dual=dual), grid_spec=grid_spec,
        out_shape=jax.ShapeDtypeStruct((rows, n), BF16),
        compiler_params=_cparams(("arbitrary", "arbitrary", "arbitrary")), name=name,
    )(tab['mexp'], tab['nvalid'], *args)


def _moe_scatter_kernel(iblk_ref, l0_ref, l1_ref, *rest, rb, ni, seg):
    del iblk_ref
    y_refs = rest[:ni]
    x_ref, mod_ref, out_ref, pt_ref, ya_ref = rest[ni:]
    row = _mod_row(pl.program_id(0), *seg)

    @pl.when(pl.program_id(1) == 0)
    def _():
        l0, l1 = l0_ref[...], l1_ref[...]
        lane = lax.broadcasted_iota(jnp.int32, (1, rb), 1).astype(F32)
        for q in range(ni):
            s = lane + float(q * rb)
            pt_ref[:, q * rb:(q + 1) * rb] = jnp.where((l0 == s) | (l1 == s), 1.0, 0.0).astype(BF16)

    for q in range(ni):
        ya_ref[q * rb:(q + 1) * rb, :] = y_refs[q][...]
    out_ref[...] = x_ref[...] + mod_ref[pl.ds(row, 1), :] * _dot(pt_ref[...], ya_ref[...])


def _moe_scatter(l0, l1, ys, x, mod, gate_off, tab, *, tm, tn, seg_fn):
    t, d = x.shape
    rb, ni = MOE_RB, tab['ni']
    tn = _tile(d, tn)
    gblk = gate_off // tn
    lspec = pl.BlockSpec((tm, LANES), lambda i, j, ib: (i, 0))
    yspecs = [pl.BlockSpec((rb, tn), lambda i, j, ib, q=q: (ib[i * ni + q], j)) for q in range(ni)]
    grid_spec = pltpu.PrefetchScalarGridSpec(
        num_scalar_prefetch=1, grid=(t // tm, d // tn),
        in_specs=[lspec, lspec, *yspecs,
                  pl.BlockSpec((tm, tn), lambda i, j, ib: (i, j)),
                  pl.BlockSpec((MOD_ROWS, tn), lambda i, j, ib: (0, gblk + j))],
        out_specs=pl.BlockSpec((tm, tn), lambda i, j, ib: (i, j)),
        scratch_shapes=[pltpu.VMEM((tm, ni * rb), BF16), pltpu.VMEM((ni * rb, tn), BF16)])
    return pl.pallas_call(
        functools.partial(_moe_scatter_kernel, rb=rb, ni=ni, seg=seg_fn(tm)), grid_spec=grid_spec,
        out_shape=jax.ShapeDtypeStruct((t, d), F32),
        compiler_params=_cparams(("parallel", "arbitrary")), name="moe_scatter",
    )(tab['iblk'], l0, l1, *([ys] * ni), x, mod)


def kernel(x, c, ctx, c_ctx, mod_w, mod_b, norm1_w, norm2_w, in_w, conv_w, conv_b, hgrn_lb, hgrn_norm_w,
           mlstm_igate_b, mlstm_fgate_b, mlstm_norm_w, ssd_a_log, ssd_dt_bias, ssd_d, ssd_norm_w,
           gate_w, gate_b, branch_w, out_w, ffn_w1, ffn_w3, ffn_w2, router_w, moe_w1, moe_w3, moe_w2,
           final_norm_w):
    nb, seq, d = x.shape
    ctx_len = ctx.shape[1]
    depth = mod_w.shape[0]
    bw = d // 2
    hg_heads = bw // HGRN_DK
    ml_dv = bw // MLSTM_HEADS
    ml_dk = ml_dv // 2
    ss_heads = bw // SSD_HEADDIM
    rank = gate_w.shape[1]
    n_ml_qk = MLSTM_HEADS * ml_dk
    n_ss_bc = SSD_GROUPS * SSD_STATE
    off = {}
    pos = 0
    for nm, sz in (('ml_q', n_ml_qk), ('ml_k', n_ml_qk), ('ss_x', bw), ('ss_B', n_ss_bc), ('ss_C', n_ss_bc),
                   ('hg_q', bw), ('hg_f_fwd', bw), ('hg_f_bwd', bw), ('hg_i', bw), ('hg_g', bw),
                   ('ml_v', bw), ('ml_z', bw), ('ml_gates', 4 * MLSTM_HEADS), ('ss_z', bw),
                   ('ss_dt', 2 * ss_heads), ('merge', rank)):
        off[nm] = (pos, sz)
        pos += sz
    n_conv = off['hg_q'][0]
    n_main = off['ml_gates'][0]
    n_small = 4 * MLSTM_HEADS + 2 * ss_heads
    assert n_small <= LANES and ctx_len % GRID_W == 0 and seq % ctx_len == 0
    t_off_z, t_off_merge, t_off_small = 0, bw, bw + rank
    n_tail_raw = bw + rank + LANES
    n_tail = -(-n_tail_raw // 512) * 512

    tb = ctx_len
    ncb, nlb = 1, seq // tb
    n_ctx_rows = nb * ctx_len
    t = n_ctx_rows + nb * seq
    ctx_row = nb

    def seg_fn(tile):
        assert n_ctx_rows % tile == 0 and seq % tile == 0
        return (n_ctx_rows // tile, seq // tile, ctx_row)

    tm_big = _tile(math.gcd(n_ctx_rows, seq), 1024)

    c_all = jnp.zeros((MOD_ROWS, d), F32).at[:nb].set(c).at[ctx_row].set(c_ctx)

    in_w_bf = in_w.astype(BF16)
    lb_cum = jnp.cumsum(jax.nn.softmax(hgrn_lb.astype(F32), axis=0), axis=0)
    lower_bounds = lb_cum - lb_cum[0]

    for l in range(depth):
        mod = _mm(c_all, mod_w, w_lead=l, out_dtype=F32, tm=MOD_ROWS, tn=1024, tk=2048,
                  a_silu=True, bias=mod_b[l][None, :], name="mod")
        if l == 0:
            h, xs = _norm_mod_join(ctx.reshape(n_ctx_rows, d), x.reshape(nb * seq, d), norm1_w[l][None, :],
                                   mod, sh_off=0, sc_off=d, seg_fn=seg_fn, tr=tb)
        else:
            h = _norm_mod(xs, norm1_w[l][None, :], mod, sh_off=0, sc_off=d, seg_fn=seg_fn, tr=tb,
                          name="norm1")
        wl = in_w[l]
        w_tail = jnp.concatenate(
            [wl[:, off['ss_z'][0]:off['ss_z'][0] + bw],
             wl[:, off['merge'][0]:off['merge'][0] + rank],
             wl[:, off['ml_gates'][0]:off['ml_gates'][0] + 4 * MLSTM_HEADS],
             wl[:, off['ss_dt'][0]:off['ss_dt'][0] + 2 * ss_heads],
             jnp.zeros((d, n_tail - n_tail_raw + LANES - n_small), F32)], axis=1).astype(BF16)
        p_main = _mm(h, in_w_bf, w_lead=l, n=n_main, out_dtype=F32, tm=tm_big, tn=1024, tk=d,
                     name="in_proj_main")
        p_tail = _mm(h, w_tail, out_dtype=F32, tm=tm_big, tn=n_tail // 2, tk=2048, name="in_proj_tail")
        conv = _conv(p_main, conv_w[l].reshape(9, n_conv), conv_b[l][None, :], n_conv=n_conv, tc=tb,
                     ctx_len=ctx_len, n_ctx_tiles=n_ctx_rows // tb, tiles_per_img=seq // tb, cb=512)

        hgf, hgb = _hgrn_scan(p_main, lower_bounds[l][None, :], nb=nb, ncb=ncb, nlb=nlb, tb=tb,
                              heads=hg_heads, off_q=off['hg_q'][0], off_ff=off['hg_f_fwd'][0],
                              off_fb=off['hg_f_bwd'][0], off_i=off['hg_i'][0])
        ml_bias = jnp.zeros((1, LANES), F32).at[0, :4 * MLSTM_HEADS].set(
            jnp.stack([mlstm_igate_b[l, 0], mlstm_fgate_b[l, 0],
                       mlstm_igate_b[l, 1], mlstm_fgate_b[l, 1]]).reshape(-1))
        mlf, mlb = _mlstm_scan(conv, p_main, p_tail, ml_bias, nb=nb, ncb=ncb, nlb=nlb, tb=tb,
                               dk=ml_dk, dv=ml_dv, off_q=off['ml_q'][0], off_k=off['ml_k'][0],
                               off_v=off['ml_v'][0], off_small=t_off_small)
        lo = 4 * MLSTM_HEADS
        dtb_row = jnp.zeros((1, LANES), F32).at[0, lo:lo + 2 * ss_heads].set(ssd_dt_bias[l].reshape(-1))
        alog_row = jnp.zeros((1, LANES), F32).at[0, lo:lo + 2 * ss_heads].set(ssd_a_log[l].reshape(-1))
        ssf, ssb = _ssd_scan(conv, p_tail, dtb_row, alog_row, nb=nb, ncb=ncb, nlb=nlb, tb=tb,
                             heads=ss_heads, off_x=off['ss_x'][0], off_b=off['ss_B'][0],
                             off_c=off['ss_C'][0], off_small=t_off_small)

        y_hg = _headnorm(hgf, hgb, p_main, off['hg_g'][0], hgrn_norm_w[l][None, :], hd=HGRN_DK,
                         tr=2 * tb, cb=1024, name="hgrn_finish")
        y_ml = _headnorm(mlf, mlb, p_main, off['ml_z'][0], mlstm_norm_w[l][None, :], hd=ml_dv,
                         tr=2 * tb, cb=1024, name="mlstm_finish")
        d_row = jnp.repeat(ssd_d[l], SSD_HEADDIM)[None, :]
        y_ss = _ssd_finish(ssf, ssb, conv, off['ss_x'][0], p_tail, t_off_z, d_row,
                           ssd_norm_w[l][None, :], tr=tb)
        ym = _merge((y_hg, y_ml, y_ss), branch_w[l].astype(BF16), p_tail, t_off_merge, rank,
                    gate_w[l].astype(BF16), gate_b[l][None, :], tm=tm_big, tn=1024)
        xs = _mm(ym, out_w[l].astype(BF16), out_dtype=F32, tm=tm_big, tn=1024, tk=d,
                 resid=(xs, mod, 2 * d, seg_fn), name="out_proj")

        i = l // 2
        if l % 2 == 0:
            h2 = _norm_mod(xs, norm2_w[l][None, :], mod, sh_off=3 * d, sc_off=4 * d, seg_fn=seg_fn,
                           tr=tb, name="norm2")
            mid = _swiglu_up(h2, ffn_w1[i].astype(BF16), ffn_w3[i].astype(BF16),
                             tm=tm_big, tn=1024, tk=2048, name="ffn_up")
            xs = _mm(mid, ffn_w2[i].astype(BF16), out_dtype=F32, tm=tm_big, tn=1024, tk=2048,
                     resid=(xs, mod, 5 * d, seg_fn), name="ffn_down")
        else:
            rw = jnp.zeros((d, LANES), F32).at[:, :N_EXPERTS].set(router_w[i])
            h2, comb, cnt = _norm_mod(xs, norm2_w[l][None, :], mod, sh_off=3 * d, sc_off=4 * d,
                                      seg_fn=seg_fn, tr=tb, router_w=rw, name="norm2_router")
            cnt_tile = cnt.reshape(t // tm_big, tm_big // tb, SUBLANES, LANES)[:, :, 0, :N_EXPERTS]
            tab = _moe_tables(jnp.sum(cnt_tile, axis=1).astype(jnp.int32), tm_big)
            a_s, w_s, l0, l1 = _moe_gather(h2, comb, tab, tm=tm_big)
            mid = _moe_grouped(a_s, (moe_w1[i].astype(BF16), moe_w3[i].astype(BF16)), w_s, tab,
                               tn=1024, tk=d, name="moe_up")
            ys = _moe_grouped(mid, (moe_w2[i].astype(BF16),), None, tab, tn=1024, tk=mid.shape[1],
                              name="moe_down")
            xs = _moe_scatter(l0, l1, ys, xs, mod, 5 * d, tab, tm=tm_big, tn=1024, seg_fn=seg_fn)

    out = _final_norm(xs, final_norm_w[None, :], row0=n_ctx_rows, rows=nb * seq, tr=tb)
    return out.reshape(nb, seq, d)
```

```python
import functools
import math

import jax
import jax.numpy as jnp
from jax import lax
from jax.experimental import pallas as pl
from jax.experimental.pallas import tpu as pltpu

F32 = jnp.float32
BF16 = jnp.bfloat16

GRID_W = 64
EPS = 1e-6
HGRN_DK = 128
HGRN_HEADS_PER_STEP = 4
MLSTM_HEADS = 8
SSD_HEADDIM = 64
SSD_GROUPS = 8
SSD_STATE = 128
N_EXPERTS = 8
LANES = 128
SUBLANES = 8
LOG2E = 1.4426950408889634
MOD_ROWS = 8
VMEM_LIMIT = 56 * 1024 * 1024


def _cparams(sem):
    return pltpu.CompilerParams(dimension_semantics=sem, vmem_limit_bytes=VMEM_LIMIT)


def _tile(n, pref):
    t = min(n, pref)
    while n % t:
        t //= 2
    return t


def _sigmoid(x):
    return 1.0 / (1.0 + jnp.exp(-x))


def _silu(x):
    return x * _sigmoid(x)


def _log_sigmoid(x):
    return jnp.minimum(x, 0.0) - jnp.log1p(jnp.exp(-jnp.abs(x)))


def _softplus(x):
    return jnp.maximum(x, 0.0) + jnp.log1p(jnp.exp(-jnp.abs(x)))


def _dot(a, b):
    return jnp.dot(a, b, preferred_element_type=F32)


def _dot_nt(a, b):
    return lax.dot_general(a, b, (((1,), (1,)), ((), ())), preferred_element_type=F32)


def _split3(x):
    x1 = x.astype(BF16)
    r = x - x1.astype(F32)
    x2 = r.astype(BF16)
    x3 = (r - x2.astype(F32)).astype(BF16)
    return x1, x2, x3


def _dot_exact(m01, x):
    x1, x2, x3 = _split3(x)
    return _dot(m01, x1) + _dot(m01, x2) + _dot(m01, x3)


def _mod_row(tile, n_ctx_tiles, tiles_per_batch, ctx_row):
    return jnp.where(tile < n_ctx_tiles, ctx_row, (tile - n_ctx_tiles) // tiles_per_batch)


def _accumulate(k, nk, part, acc_refs, finish):
    if nk == 1:
        finish(*part())
        return

    @pl.when(k == 0)
    def _():
        for acc, p in zip(acc_refs, part()):
            acc[...] = p

    @pl.when((k > 0) & (k < nk - 1))
    def _():
        for acc, p in zip(acc_refs, part()):
            acc[...] += p

    @pl.when(k == nk - 1)
    def _():
        finish(*[acc[...] + p for acc, p in zip(acc_refs, part())])


def _mm_kernel(*refs, nk, a_silu, has_bias, resid, seg, tile0):
    it = iter(refs)
    a_ref, w_ref = next(it), next(it)
    bias_ref = next(it) if has_bias else None
    x_ref = next(it) if resid else None
    mod_ref = next(it) if resid else None
    out_ref = next(it)
    acc_ref = next(it) if nk > 1 else None
    k = pl.program_id(2)
    row = _mod_row(pl.program_id(0) + tile0, *seg) if resid else None

    def part():
        a = a_ref[...]
        if a_silu:
            a = _silu(a.astype(F32))
        return (_dot(a.astype(BF16), w_ref[...].astype(BF16)),)

    def finish(r):
        if has_bias:
            r = r + bias_ref[...]
        if resid:
            r = x_ref[...] + mod_ref[pl.ds(row, 1), :] * r
        out_ref[...] = r.astype(out_ref.dtype)

    _accumulate(k, nk, part, (acc_ref,), finish)


def _mm(a, w, *, out_dtype, tm, tn, tk, w_lead=None, n=None, a_silu=False, bias=None,
        resid=None, row0=0, name):
    m, kdim = a.shape
    n = w.shape[-1] if n is None else n
    tm, tn, tk = _tile(math.gcd(m, row0), tm), _tile(n, tn), _tile(kdim, tk)
    nk = kdim // tk
    o = row0 // tm
    if w_lead is None:
        w_spec = pl.BlockSpec((tk, tn), lambda i, j, k: (k, j))
    else:
        w_spec = pl.BlockSpec((None, tk, tn), lambda i, j, k: (w_lead, k, j))
    in_specs = [pl.BlockSpec((tm, tk), lambda i, j, k: (i + o, k)), w_spec]
    args = [a, w]
    if bias is not None:
        in_specs.append(pl.BlockSpec((1, tn), lambda i, j, k: (0, j)))
        args.append(bias)
    seg = None
    if resid is not None:
        x, mod, gate_off, seg_fn = resid
        seg = seg_fn(tm)
        gblk = gate_off // tn
        in_specs.append(pl.BlockSpec((tm, tn), lambda i, j, k: (i + o, j)))
        in_specs.append(pl.BlockSpec((MOD_ROWS, tn), lambda i, j, k: (0, gblk + j)))
        args += [x, mod]
    return pl.pallas_call(
        functools.partial(_mm_kernel, nk=nk, a_silu=a_silu, has_bias=bias is not None,
                          resid=resid is not None, seg=seg, tile0=o),
        grid=((m - row0) // tm, n // tn, nk),
        in_specs=in_specs,
        out_specs=pl.BlockSpec((tm, tn), lambda i, j, k: (i + o, j)),
        out_shape=jax.ShapeDtypeStruct((m, n), out_dtype),
        scratch_shapes=[pltpu.VMEM((tm, tn), F32)] if nk > 1 else [],
        compiler_params=_cparams(("parallel", "parallel", "arbitrary")),
        name=name,
    )(*args)


def _norm_mod_kernel(*refs, d, sh_off, sc_off, seg, router, tile0):
    if router:
        x_ref, nw_ref, mod_ref, rw_ref, out_ref, comb_ref, cnt_ref = refs
    else:
        x_ref, nw_ref, mod_ref, out_ref = refs
    row = _mod_row(pl.program_id(0) + tile0, *seg)
    x = x_ref[...]
    y = x * lax.rsqrt(jnp.mean(x * x, axis=-1, keepdims=True) + EPS) * nw_ref[...]
    sc = mod_ref[pl.ds(row, 1), sc_off:sc_off + d]
    sh = mod_ref[pl.ds(row, 1), sh_off:sh_off + d]
    h = y * (1.0 + sc) + sh
    out_ref[...] = h.astype(BF16)
    if router:
        h1, h2, h3 = _split3(h)
        r1, r2, r3 = _split3(rw_ref[...])
        logits = (_dot(h1, r1) + _dot(h1, r2) + _dot(h2, r1)
                  + _dot(h2, r2) + _dot(h1, r3) + _dot(h3, r1))
        lane = lax.broadcasted_iota(jnp.int32, logits.shape, 1).astype(F32)
        valid = lane < N_EXPERTS
        logits = jnp.where(valid, logits, -jnp.inf)
        mx = jnp.max(logits, axis=-1, keepdims=True)
        e = jnp.exp(logits - mx)
        probs = e / jnp.sum(e, axis=-1, keepdims=True)
        p1 = jnp.max(probs, axis=-1, keepdims=True)
        i1 = jnp.min(jnp.where((probs == p1) & valid, lane, float(LANES)), axis=-1, keepdims=True)
        rest = jnp.where((lane == i1) | (lane >= N_EXPERTS), -1.0, probs)
        p2 = jnp.max(rest, axis=-1, keepdims=True)
        i2 = jnp.min(jnp.where(rest == p2, lane, float(LANES)), axis=-1, keepdims=True)
        tot = p1 + p2
        comb = jnp.where(lane == i1, p1 / tot, jnp.where(lane == i2, p2 / tot, 0.0))
        comb_ref[...] = comb
        cnt = jnp.sum(jnp.where(comb > 0.0, 1.0, 0.0), axis=0, keepdims=True)
        cnt_ref[...] = jnp.broadcast_to(cnt, cnt_ref.shape)


def _norm_mod(x, nw, mod, *, sh_off, sc_off, seg_fn, tr, router_w=None, row0=0, name):
    t, d = x.shape
    tr = _tile(math.gcd(t, row0), tr)
    o = row0 // tr
    router = router_w is not None
    in_specs = [pl.BlockSpec((tr, d), lambda i: (i + o, 0)),
                pl.BlockSpec((1, d), lambda i: (0, 0)),
                pl.BlockSpec(mod.shape, lambda i: (0, 0))]
    args = [x, nw, mod]
    out_specs = [pl.BlockSpec((tr, d), lambda i: (i + o, 0))]
    out_shape = [jax.ShapeDtypeStruct((t, d), BF16)]
    if router:
        in_specs.append(pl.BlockSpec(router_w.shape, lambda i: (0, 0)))
        args.append(router_w)
        out_specs.append(pl.BlockSpec((tr, LANES), lambda i: (i + o, 0)))
        out_shape.append(jax.ShapeDtypeStruct((t, LANES), F32))
        out_specs.append(pl.BlockSpec((SUBLANES, LANES), lambda i: (i + o, 0)))
        out_shape.append(jax.ShapeDtypeStruct((t // tr * SUBLANES, LANES), F32))
    res = pl.pallas_call(
        functools.partial(_norm_mod_kernel, d=d, sh_off=sh_off, sc_off=sc_off, seg=seg_fn(tr),
                          router=router, tile0=o),
        grid=((t - row0) // tr,), in_specs=in_specs, out_specs=out_specs, out_shape=out_shape,
        compiler_params=_cparams(("parallel",)), name=name,
    )(*args)
    return res if router else res[0]


def _norm_mod_join_kernel(ctx_ref, lat_ref, nw_ref, mod_ref, out_ref, xs_ref, *, d, sh_off, sc_off, seg):
    i = pl.program_id(0)
    row = _mod_row(i, *seg)
    x = jnp.where(i < seg[0], ctx_ref[...], lat_ref[...])
    xs_ref[...] = x
    y = x * lax.rsqrt(jnp.mean(x * x, axis=-1, keepdims=True) + EPS) * nw_ref[...]
    sc = mod_ref[pl.ds(row, 1), sc_off:sc_off + d]
    sh = mod_ref[pl.ds(row, 1), sh_off:sh_off + d]
    out_ref[...] = (y * (1.0 + sc) + sh).astype(BF16)


def _norm_mod_join(ctx2d, lat2d, nw, mod, *, sh_off, sc_off, seg_fn, tr):
    d = ctx2d.shape[1]
    t = ctx2d.shape[0] + lat2d.shape[0]
    seg = seg_fn(tr)
    nct = seg[0]
    return pl.pallas_call(
        functools.partial(_norm_mod_join_kernel, d=d, sh_off=sh_off, sc_off=sc_off, seg=seg),
        grid=(t // tr,),
        in_specs=[pl.BlockSpec((tr, d), lambda i: (jnp.minimum(i, nct - 1), 0)),
                  pl.BlockSpec((tr, d), lambda i: (jnp.maximum(i - nct, 0), 0)),
                  pl.BlockSpec((1, d), lambda i: (0, 0)),
                  pl.BlockSpec(mod.shape, lambda i: (0, 0))],
        out_specs=[pl.BlockSpec((tr, d), lambda i: (i, 0)), pl.BlockSpec((tr, d), lambda i: (i, 0))],
        out_shape=[jax.ShapeDtypeStruct((t, d), BF16), jax.ShapeDtypeStruct((t, d), F32)],
        compiler_params=_cparams(("arbitrary",)), name="norm1_join",
    )(ctx2d, lat2d, nw, mod)


def _final_norm_kernel(x_ref, w_ref, out_ref):
    x = x_ref[...]
    out_ref[...] = x * lax.rsqrt(jnp.mean(x * x, axis=-1, keepdims=True) + EPS) * w_ref[...]


def _final_norm(x, w, *, row0, rows, tr):
    d = x.shape[1]
    tr = _tile(math.gcd(row0, rows), tr)
    off = row0 // tr
    return pl.pallas_call(
        _final_norm_kernel, grid=(rows // tr,),
        in_specs=[pl.BlockSpec((tr, d), lambda i: (i + off, 0)),
                  pl.BlockSpec((1, d), lambda i: (0, 0))],
        out_specs=pl.BlockSpec((tr, d), lambda i: (i, 0)),
        out_shape=jax.ShapeDtypeStruct((rows, d), F32),
        compiler_params=_cparams(("parallel",)), name="final_norm",
    )(x, w)


def _conv_kernel(main_ref, prev_ref, next_ref, w_ref, b_ref, out_ref, z_ref, *,
                 tc, ctx_len, n_ctx_tiles, tiles_per_img):
    i = pl.program_id(0)
    is_ctx = i < n_ctx_tiles
    li = i - n_ctx_tiles
    first = (li % tiles_per_img) == 0
    last = (li % tiles_per_img) == tiles_per_img - 1
    z_ref[0:GRID_W, :] = jnp.where(is_ctx | first, 0.0, prev_ref[...])
    z_ref[GRID_W:GRID_W + tc, :] = main_ref[...]
    z_ref[GRID_W + tc:, :] = jnp.where(is_ctx | last, 0.0, next_ref[...])
    w = w_ref[...]

    def column_sum(dc):
        s = w[3 + dc:4 + dc, :] * z_ref[GRID_W:GRID_W + tc, :]
        for dr in (0, 2):
            wt = jnp.where(is_ctx, 0.0, w[3 * dr + dc:3 * dr + dc + 1, :])
            s = s + wt * z_ref[GRID_W * dr:GRID_W * dr + tc, :]
        return s

    pos = lax.broadcasted_iota(jnp.int32, (tc, 1), 0)
    col = jnp.where(is_ctx, pos % ctx_len, pos % GRID_W)
    width = jnp.where(is_ctx, ctx_len, GRID_W)
    left = jnp.where(col != 0, pltpu.roll(column_sum(0), 1, 0), 0.0)
    right = jnp.where(col != width - 1, pltpu.roll(column_sum(2), tc - 1, 0), 0.0)
    out_ref[...] = _silu(b_ref[...] + column_sum(1) + left + right)


def _conv(p_main, conv_w9, conv_b, *, n_conv, tc, ctx_len, n_ctx_tiles, tiles_per_img, cb):
    t = p_main.shape[0]
    cb = _tile(n_conv, cb)
    rpt = tc // GRID_W
    nrow = t // GRID_W
    return pl.pallas_call(
        functools.partial(_conv_kernel, tc=tc, ctx_len=ctx_len, n_ctx_tiles=n_ctx_tiles,
                          tiles_per_img=tiles_per_img),
        grid=(t // tc, n_conv // cb),
        in_specs=[pl.BlockSpec((tc, cb), lambda i, j: (i, j)),
                  pl.BlockSpec((GRID_W, cb), lambda i, j: (jnp.maximum(i * rpt - 1, 0), j)),
                  pl.BlockSpec((GRID_W, cb), lambda i, j: (jnp.minimum((i + 1) * rpt, nrow - 1), j)),
                  pl.BlockSpec((9, cb), lambda i, j: (0, j)),
                  pl.BlockSpec((1, cb), lambda i, j: (0, j))],
        out_specs=pl.BlockSpec((tc, cb), lambda i, j: (i, j)),
        out_shape=jax.ShapeDtypeStruct((t, n_conv), F32),
        scratch_shapes=[pltpu.VMEM((tc + 2 * GRID_W, cb), F32)],
        compiler_params=_cparams(("parallel", "parallel")), name="conv_silu",
    )(p_main, p_main, p_main, conv_w9, conv_b)


def _rowblk(b, s, rev, ncb, nlb, nb):
    if rev:
        ctx = b * ncb + (ncb - 1 - s)
        lat = nb * ncb + b * nlb + (nlb - 1 - (s - ncb))
    else:
        ctx = b * ncb + s
        lat = nb * ncb + b * nlb + (s - ncb)
    return jnp.where(s < ncb, ctx, lat)


def _tri(n, rev):
    r = lax.broadcasted_iota(jnp.int32, (n, n), 0)
    c = lax.broadcasted_iota(jnp.int32, (n, n), 1)
    return (c >= r) if rev else (c <= r)


def _level_ref(bl, half, rev):
    tb, dk = bl.shape
    blk = 2 * half
    idx = half if rev else half - 1
    if blk == tb:
        return bl[idx:idx + 1, :]
    if blk >= SUBLANES:
        b3 = bl.reshape(tb // blk, blk, dk)
        return jnp.broadcast_to(b3[:, idx:idx + 1, :], b3.shape).reshape(tb, dk)
    tmod = lax.broadcasted_iota(jnp.int32, (tb, 1), 0) % blk
    r = bl
    for m in range(blk):
        if idx != m:
            r = jnp.where(tmod == m, pltpu.roll(bl, (m - idx) % tb, 0), r)
    return r


def _hgrn_dir(q_ref, u_ref, v_ref, lb_ref, o_ref, st_ref, d, rev, tb, hps):
    ri = lax.broadcasted_iota(jnp.int32, (tb, tb), 0)
    cj = lax.broadcasted_iota(jnp.int32, (tb, tb), 1)
    keep = jnp.where((cj >= ri) if rev else (cj <= ri), 1.0, 0.0).astype(BF16)
    half = tb // 2
    ri = lax.broadcasted_iota(jnp.int32, (half, half), 0)
    cj = lax.broadcasted_iota(jnp.int32, (half, half), 1)
    level = jnp.where((cj > ri) if rev else (cj < ri), 31 - lax.clz(ri ^ cj), -1)
    for h in range(hps):
        _hgrn_head(q_ref, u_ref, v_ref, lb_ref, o_ref, st_ref, d, h, rev, tb, keep, level)


def _hgrn_head(q_ref, u_ref, v_ref, lb_ref, o_ref, st_ref, d, h, rev, tb, keep, level):
    dk = HGRN_DK
    cols = slice(h * dk, (h + 1) * dk)
    lbv = lb_ref[:, cols]
    qraw = q_ref[:, cols]
    u = u_ref[:, cols]
    v = v_ref[:, cols]
    q = _silu(qraw) * dk ** -0.5
    la = jnp.log(lbv)
    lc = jnp.log1p(-lbv) + _log_sigmoid(u)
    mx = jnp.maximum(la, lc)
    mn = jnp.minimum(la, lc)
    logf = mx + jnp.log1p(jnp.exp(mn - mx))
    k = (1.0 - lbv) * _sigmoid(-u)

    bl = _dot_exact(keep, logf * LOG2E)
    b_end = bl[0:1] if rev else bl[tb - 1:tb]
    half = tb // 2
    halves = (slice(0, half), slice(half, tb))
    scores = [jnp.zeros((half, half), F32), jnp.zeros((half, half), F32)]
    for lv in range(half.bit_length() - 1):
        dq = bl - _level_ref(bl, 1 << lv, rev)
        qt = (q * jnp.exp2(dq)).astype(BF16)
        kt = (k * jnp.exp2(-dq)).astype(BF16)
        for hb, rows in enumerate(halves):
            scores[hb] = jnp.where(level == lv, _dot_nt(qt[rows], kt[rows]), scores[hb])
    isl, jsl = (halves[0], halves[1]) if rev else (halves[1], halves[0])
    dq = bl - _level_ref(bl, half, rev)
    cross = _dot_nt((q[isl] * jnp.exp2(dq[isl])).astype(BF16), (k[jsl] * jnp.exp2(-dq[jsl])).astype(BF16))
    vb = v.astype(BF16)
    st = st_ref[d, h]
    rest = (jnp.sum(q * k, axis=-1, keepdims=True) * v
            + _dot_nt((q * jnp.exp2(bl)).astype(BF16), st.astype(BF16)))
    for hb, rows in enumerate(halves):
        o = _dot(scores[hb].astype(BF16), vb[rows]) + rest[rows]
        if rows == isl:
            o = o + _dot(cross.astype(BF16), vb[jsl])
        o_ref[rows, cols] = o
    st_ref[d, h] = st * jnp.exp2(b_end) + _dot(v.T.astype(BF16), (k * jnp.exp2(b_end - bl)).astype(BF16))


def _hgrn_kernel(qf, uf, vf, qb, ub, vb, lb_ref, of, ob, st_ref, *, tb, hps):
    @pl.when(pl.program_id(2) == 0)
    def _():
        st_ref[...] = jnp.zeros_like(st_ref)
    _hgrn_dir(qf, uf, vf, lb_ref, of, st_ref, 0, False, tb, hps)
    _hgrn_dir(qb, ub, vb, lb_ref, ob, st_ref, 1, True, tb, hps)


def _hgrn_scan(p_main, lb_row, *, nb, ncb, nlb, tb, heads, off_q, off_ff, off_fb, off_i):
    t = p_main.shape[0]
    dk = HGRN_DK

    hps = HGRN_HEADS_PER_STEP
    wd = hps * dk

    def spec(off, rev):
        return pl.BlockSpec((tb, wd), lambda b, h, s: (_rowblk(b, s, rev, ncb, nlb, nb), off // wd + h))

    def ospec(rev):
        return pl.BlockSpec((tb, wd), lambda b, h, s: (_rowblk(b, s, rev, ncb, nlb, nb), h))

    out = jax.ShapeDtypeStruct((t, heads * dk), F32)
    return pl.pallas_call(
        functools.partial(_hgrn_kernel, tb=tb, hps=hps),
        grid=(nb, heads // hps, ncb + nlb),
        in_specs=[spec(off_q, False), spec(off_ff, False), spec(off_i, False),
                  spec(off_q, True), spec(off_fb, True), spec(off_i, True),
                  pl.BlockSpec((1, wd), lambda b, h, s: (0, h))],
        out_specs=[ospec(False), ospec(True)],
        out_shape=[out, out],
        scratch_shapes=[pltpu.VMEM((2, hps, dk, dk), F32)],
        compiler_params=_cparams(("parallel", "parallel", "arbitrary")), name="hgrn_scan",
    )(p_main, p_main, p_main, p_main, p_main, p_main, lb_row)


def _mlstm_dir(q_ref, k_ref, v_ref, g_ref, bias_ref, o_ref, c_ref, n_ref, m_ref, d, rev, tb, dk, dv):
    g = g_ref[...] + bias_ref[...]
    lf = _log_sigmoid(g)
    keep = _tri(tb, rev)
    b = _dot_exact(jnp.where(keep, 1.0, 0.0).astype(BF16), lf)
    bt, gt = b.T, g.T
    for h in range(MLSTM_HEADS):
        ci = 2 * MLSTM_HEADS * d + h
        cf = ci + MLSTM_HEADS
        q = q_ref[:, h * dk:(h + 1) * dk]
        ks = k_ref[:, h * dk:(h + 1) * dk] * dk ** -0.5
        v = v_ref[:, h * dv:(h + 1) * dv]
        b_col, ic_col = b[:, cf:cf + 1], g[:, ci:ci + 1]
        b_row, ic_row = bt[cf:cf + 1, :], gt[ci:ci + 1, :]
        b_end = b_col[0:1] if rev else b_col[tb - 1:tb]
        m_prev = m_ref[d, h][:, 0:1]
        w_in = jnp.where(keep, b_col - b_row + ic_row, -jnp.inf)
        w_st = b_col + m_prev
        m_row = jnp.maximum(jnp.max(w_in, axis=1, keepdims=True), w_st)
        qb = q.astype(BF16)
        vb = v.astype(BF16)
        p = jnp.exp(w_in - m_row) * _dot_nt(qb, ks.astype(BF16))
        e_st = jnp.exp(w_st - m_row)
        num = _dot(p.astype(BF16), vb) + e_st * _dot(qb, c_ref[d, h].astype(BF16))
        nrm = (jnp.sum(p, axis=1, keepdims=True)
               + e_st * jnp.sum(q * n_ref[d, h], axis=1, keepdims=True))
        o_ref[:, h * dv:(h + 1) * dv] = num / jnp.maximum(jnp.abs(nrm), jnp.exp(-m_row))
        w_end = b_end - b_col + ic_col
        m_new = jnp.maximum(b_end + m_prev, jnp.max(w_end, axis=0, keepdims=True))
        s_old = jnp.exp(b_end + m_prev - m_new)
        kt = ks * jnp.exp(w_end - m_new)
        c_ref[d, h] = s_old * c_ref[d, h] + _dot(kt.T.astype(BF16), vb)
        n_ref[d, h] = s_old * n_ref[d, h] + jnp.sum(kt, axis=0, keepdims=True)
        m_ref[d, h] = jnp.broadcast_to(m_new, m_ref.shape[2:])


def _mlstm_kernel(qf, kf, vf, gf, qb, kb, vb, gb, bias_ref, of, ob, c_ref, n_ref, m_ref, *, tb, dk, dv):
    @pl.when(pl.program_id(1) == 0)
    def _():
        c_ref[...] = jnp.zeros_like(c_ref)
        n_ref[...] = jnp.zeros_like(n_ref)
        m_ref[...] = jnp.zeros_like(m_ref)
    _mlstm_dir(qf, kf, vf, gf, bias_ref, of, c_ref, n_ref, m_ref, 0, False, tb, dk, dv)
    _mlstm_dir(qb, kb, vb, gb, bias_ref, ob, c_ref, n_ref, m_ref, 1, True, tb, dk, dv)


def _mlstm_scan(conv, p_main, p_tail, bias_row, *, nb, ncb, nlb, tb, dk, dv,
                off_q, off_k, off_v, off_small):
    t = conv.shape[0]

    nh = MLSTM_HEADS

    def spec(width, off, rev):
        return pl.BlockSpec((tb, width), lambda b, s: (_rowblk(b, s, rev, ncb, nlb, nb), off // width))

    def ospec(rev):
        return pl.BlockSpec((tb, nh * dv), lambda b, s: (_rowblk(b, s, rev, ncb, nlb, nb), 0))

    out = jax.ShapeDtypeStruct((t, nh * dv), F32)
    return pl.pallas_call(
        functools.partial(_mlstm_kernel, tb=tb, dk=dk, dv=dv),
        grid=(nb, ncb + nlb),
        in_specs=[spec(nh * dk, off_q, False), spec(nh * dk, off_k, False), spec(nh * dv, off_v, False),
                  spec(LANES, off_small, False),
                  spec(nh * dk, off_q, True), spec(nh * dk, off_k, True), spec(nh * dv, off_v, True),
                  spec(LANES, off_small, True),
                  pl.BlockSpec((1, LANES), lambda b, s: (0, 0))],
        out_specs=[ospec(False), ospec(True)],
        out_shape=[out, out],
        scratch_shapes=[pltpu.VMEM((2, nh, dk, dv), F32), pltpu.VMEM((2, nh, 1, dk), F32),
                        pltpu.VMEM((2, nh, 1, LANES), F32)],
        compiler_params=_cparams(("parallel", "arbitrary")), name="mlstm_scan",
    )(conv, conv, p_main, p_tail, conv, conv, p_main, p_tail, bias_row)


def _ssd_dir(c_ref, bm_ref, x_ref, g_ref, dtb_ref, alog_ref, o_ref, s_ref, d, rev, tb, heads, hpg):
    dt = _softplus(g_ref[...] + dtb_ref[...])
    la = -jnp.exp(alog_ref[...]) * dt
    keep = _tri(tb, rev)
    b = _dot_exact(jnp.where(keep, 1.0, 0.0).astype(BF16), la * LOG2E)
    bt = b.T
    width = hpg * SSD_HEADDIM
    hl = lax.broadcasted_iota(jnp.int32, (1, width), 1) // SSD_HEADDIM
    for grp in range(SSD_GROUPS):
        lane0 = MLSTM_HEADS * 4 + heads * d + hpg * grp
        cm = c_ref[:, grp * SSD_STATE:(grp + 1) * SSD_STATE].astype(BF16)
        bm = bm_ref[:, grp * SSD_STATE:(grp + 1) * SSD_STATE]
        x = x_ref[:, grp * width:(grp + 1) * width]
        b_all = jnp.zeros((tb, width), F32)
        dt_all = jnp.zeros((tb, width), F32)
        for i in range(hpg):
            b_all = jnp.where(hl == i, b[:, lane0 + i:lane0 + i + 1], b_all)
            dt_all = jnp.where(hl == i, dt[:, lane0 + i:lane0 + i + 1], dt_all)
        b_end = b_all[0:1] if rev else b_all[tb - 1:tb]
        v_all = x * dt_all
        v_bf = v_all.astype(BF16)
        gmat = _dot_nt(cm, bm.astype(BF16))
        s_old = s_ref[d, grp]
        inter = jnp.exp2(b_all) * _dot(cm, s_old.astype(BF16))
        outs = []
        for i in range(hpg):
            c = lane0 + i
            dec = jnp.where(keep, jnp.exp2(b[:, c:c + 1] - bt[c:c + 1, :]), 0.0)
            outs.append(_dot((gmat * dec).astype(BF16), v_bf[:, i * SSD_HEADDIM:(i + 1) * SSD_HEADDIM]))
        o_ref[:, grp * width:(grp + 1) * width] = jnp.concatenate(outs, axis=1) + inter
        sv = (v_all * jnp.exp2(b_end - b_all)).astype(BF16)
        s_ref[d, grp] = jnp.exp2(b_end) * s_old + _dot(bm.T.astype(BF16), sv)


def _ssd_kernel(cf, bf, xf, gf, cb, bb, xb, gb, dtb_ref, alog_ref, of, ob, s_ref, *, tb, heads, hpg):
    @pl.when(pl.program_id(1) == 0)
    def _():
        s_ref[...] = jnp.zeros_like(s_ref)
    _ssd_dir(cf, bf, xf, gf, dtb_ref, alog_ref, of, s_ref, 0, False, tb, heads, hpg)
    _ssd_dir(cb, bb, xb, gb, dtb_ref, alog_ref, ob, s_ref, 1, True, tb, heads, hpg)


def _ssd_scan(conv, p_tail, dtb_row, alog_row, *, nb, ncb, nlb, tb, heads, off_x, off_b, off_c, off_small):
    t = conv.shape[0]
    hpg = heads // SSD_GROUPS
    width = hpg * SSD_HEADDIM

    ng = SSD_GROUPS

    def spec(w, off, rev):
        return pl.BlockSpec((tb, w), lambda b, s: (_rowblk(b, s, rev, ncb, nlb, nb), off // w))

    def ospec(rev):
        return pl.BlockSpec((tb, ng * width), lambda b, s: (_rowblk(b, s, rev, ncb, nlb, nb), 0))

    row = pl.BlockSpec((1, LANES), lambda b, s: (0, 0))
    out = jax.ShapeDtypeStruct((t, heads * SSD_HEADDIM), F32)
    return pl.pallas_call(
        functools.partial(_ssd_kernel, tb=tb, heads=heads, hpg=hpg),
        grid=(nb, ncb + nlb),
        in_specs=[spec(ng * SSD_STATE, off_c, False), spec(ng * SSD_STATE, off_b, False),
                  spec(ng * width, off_x, False), spec(LANES, off_small, False),
                  spec(ng * SSD_STATE, off_c, True), spec(ng * SSD_STATE, off_b, True),
                  spec(ng * width, off_x, True), spec(LANES, off_small, True), row, row],
        out_specs=[ospec(False), ospec(True)],
        out_shape=[out, out],
        scratch_shapes=[pltpu.VMEM((2, ng, SSD_STATE, width), F32)],
        compiler_params=_cparams(("parallel", "arbitrary")), name="ssd_scan",
    )(conv, conv, conv, p_tail, conv, conv, conv, p_tail, dtb_row, alog_row)


def _headnorm_kernel(of_ref, ob_ref, g_ref, w_ref, out_ref, *, hd):
    cb = of_ref.shape[1]
    for i in range(cb // hd):
        sl = slice(i * hd, (i + 1) * hd)
        o = of_ref[:, sl] + ob_ref[:, sl]
        y = o * lax.rsqrt(jnp.mean(o * o, axis=-1, keepdims=True) + EPS) * w_ref[:, sl]
        out_ref[:, sl] = (y * _silu(g_ref[:, sl])).astype(BF16)


def _headnorm(of, ob, gsrc, goff, w, *, hd, tr, cb, name):
    t, wd = of.shape
    tr, cb = _tile(t, tr), _tile(wd, cb)
    gblk = goff // cb
    return pl.pallas_call(
        functools.partial(_headnorm_kernel, hd=hd), grid=(t // tr, wd // cb),
        in_specs=[pl.BlockSpec((tr, cb), lambda i, j: (i, j)),
                  pl.BlockSpec((tr, cb), lambda i, j: (i, j)),
                  pl.BlockSpec((tr, cb), lambda i, j: (i, gblk + j)),
                  pl.BlockSpec((1, cb), lambda i, j: (0, j))],
        out_specs=pl.BlockSpec((tr, cb), lambda i, j: (i, j)),
        out_shape=jax.ShapeDtypeStruct((t, wd), BF16),
        compiler_params=_cparams(("parallel", "parallel")), name=name,
    )(of, ob, gsrc, w)


def _ssd_finish_kernel(of_ref, ob_ref, x_ref, z_ref, d_ref, w_ref, out_ref):
    y = of_ref[...] + ob_ref[...] + d_ref[...] * x_ref[...]
    tt = y * _silu(z_ref[...])
    out_ref[...] = (tt * lax.rsqrt(jnp.mean(tt * tt, axis=-1, keepdims=True) + EPS)
                    * w_ref[...]).astype(BF16)


def _ssd_finish(of, ob, conv, off_x, p_tail, off_z, d_row, w, *, tr):
    t, wd = of.shape
    tr = _tile(t, tr)
    row = pl.BlockSpec((1, wd), lambda i: (0, 0))
    return pl.pallas_call(
        _ssd_finish_kernel, grid=(t // tr,),
        in_specs=[pl.BlockSpec((tr, wd), lambda i: (i, 0)),
                  pl.BlockSpec((tr, wd), lambda i: (i, 0)),
                  pl.BlockSpec((tr, wd), lambda i: (i, off_x // wd)),
                  pl.BlockSpec((tr, wd), lambda i: (i, off_z // wd)), row, row],
        out_specs=pl.BlockSpec((tr, wd), lambda i: (i, 0)),
        out_shape=jax.ShapeDtypeStruct((t, wd), BF16),
        compiler_params=_cparams(("parallel",)), name="ssd_finish",
    )(of, ob, conv, p_tail, d_row, w)


def _merge_kernel(y0, y1, y2, bw_ref, mg_ref, gw_ref, gb_ref, out_ref, acc_ref):
    k = pl.program_id(2)
    gate = _sigmoid(_dot(mg_ref[...].astype(BF16), gw_ref[...]) + gb_ref[...])
    for idx, y_ref in enumerate((y0, y1, y2)):
        @pl.when(k == idx)
        def _(y_ref=y_ref, idx=idx):
            r = gate * _dot(y_ref[...], bw_ref[...])
            if idx == 0:
                acc_ref[...] = r
            elif idx == 1:
                acc_ref[...] += r
            else:
                out_ref[...] = (acc_ref[...] + r).astype(BF16)


def _merge(ys, bw, p_tail, off_merge, rank, gw, gb, *, tm, tn, row0):
    t, bwid = ys[0].shape
    d = bw.shape[-1]
    tm, tn = _tile(math.gcd(t, row0), tm), _tile(d, tn)
    nj = d // tn
    o = row0 // tm
    yspec = pl.BlockSpec((tm, bwid), lambda i, j, k: (i + o, 0))
    return pl.pallas_call(
        _merge_kernel, grid=((t - row0) // tm, nj, 3),
        in_specs=[yspec, yspec, yspec,
                  pl.BlockSpec((None, bwid, tn), lambda i, j, k: (k, 0, j)),
                  pl.BlockSpec((tm, rank), lambda i, j, k: (i + o, off_merge // rank)),
                  pl.BlockSpec((rank, tn), lambda i, j, k: (0, k * nj + j)),
                  pl.BlockSpec((1, tn), lambda i, j, k: (0, k * nj + j))],
        out_specs=pl.BlockSpec((tm, tn), lambda i, j, k: (i + o, j)),
        out_shape=jax.ShapeDtypeStruct((t, d), BF16),
        scratch_shapes=[pltpu.VMEM((tm, tn), F32)],
        compiler_params=_cparams(("parallel", "parallel", "arbitrary")), name="branch_merge",
    )(ys[0], ys[1], ys[2], bw, p_tail, gw, gb)


def _swiglu_kernel(a_ref, w1_ref, w3_ref, out_ref, *accs, nk):
    def part():
        a = a_ref[...]
        return _dot(a, w1_ref[...]), _dot(a, w3_ref[...])

    def finish(r1, r3):
        out_ref[...] = (_silu(r1) * r3).astype(BF16)

    _accumulate(pl.program_id(2), nk, part, accs, finish)


def _swiglu_up(a, w1, w3, *, tm, tn, tk, row0, name):
    t, kdim = a.shape
    n = w1.shape[1]
    tm, tn, tk = _tile(math.gcd(t, row0), tm), _tile(n, tn), _tile(kdim, tk)
    nk = kdim // tk
    o = row0 // tm
    wspec = pl.BlockSpec((tk, tn), lambda i, j, k: (k, j))
    return pl.pallas_call(
        functools.partial(_swiglu_kernel, nk=nk),
        grid=((t - row0) // tm, n // tn, nk),
        in_specs=[pl.BlockSpec((tm, tk), lambda i, j, k: (i + o, k)), wspec, wspec],
        out_specs=pl.BlockSpec((tm, tn), lambda i, j, k: (i + o, j)),
        out_shape=jax.ShapeDtypeStruct((t, n), BF16),
        scratch_shapes=[pltpu.VMEM((tm, tn), F32)] * (2 if nk > 1 else 0),
        compiler_params=_cparams(("parallel", "parallel", "arbitrary")), name=name,
    )(a, w1, w3)


MOE_RB = LANES
MOE_MB = 512


def _moe_tables(cnt_tile, tm):
    nt, ne = cnt_tile.shape
    rb, bpm = MOE_RB, MOE_MB // MOE_RB
    ni = 2 * tm // rb + ne
    nblk = -(-(nt * ni + ne * (bpm - 1)) // bpm) * bpm
    nbk = (cnt_tile + (rb - 1)) // rb
    reg_e = (jnp.sum(nbk, axis=0) + (bpm - 1)) // bpm * bpm
    end_e = jnp.cumsum(reg_e)
    start_e = end_e - reg_e
    pre_ie = jnp.cumsum(nbk, axis=0) - nbk
    ends_ie = jnp.cumsum(nbk, axis=1)
    off_ie = ends_ie - nbk
    n_items = ends_ie[:, -1]
    it = jnp.arange(ni, dtype=jnp.int32)[None, :]
    itc = jnp.minimum(it, n_items[:, None] - 1)
    e_idx = jnp.sum((itc[:, :, None] >= ends_ie[:, None, :]).astype(jnp.int32), axis=-1)
    e_idx = jnp.minimum(e_idx, ne - 1)
    chunk = itc - jnp.take_along_axis(off_ie, e_idx, axis=1)
    gblk = start_e[e_idx] + jnp.take_along_axis(pre_ie, e_idx, axis=1) + chunk
    iout = jnp.where(it < n_items[:, None], gblk, nblk)
    m = jnp.arange(nblk // bpm, dtype=jnp.int32)
    mexp = jnp.sum((m[:, None] * bpm >= end_e[None, :]).astype(jnp.int32), axis=-1)
    off_row = jnp.zeros((nt, 1, LANES), F32).at[:, 0, :ne].set((off_ie * rb).astype(F32))
    return dict(ni=ni, nblk=nblk, iout=iout.reshape(-1).astype(jnp.int32),
                iblk=gblk.reshape(-1).astype(jnp.int32), n_items=n_items.astype(jnp.int32),
                mexp=jnp.minimum(mexp, ne - 1).astype(jnp.int32),
                nvalid=(end_e[-1:] // bpm).astype(jnp.int32), off_row=off_row)


def _moe_gather_kernel(iout_ref, h_ref, comb_ref, off_ref, za_ref, zw_ref,
                       a_ref, w_ref, l0_ref, l1_ref, lt_ref, wt_ref, *, tm, rb):
    del iout_ref, za_ref, zw_ref
    it = pl.program_id(1)

    @pl.when(it == 0)
    def _():
        comb = comb_ref[...]
        pick = comb > 0.0
        r = lax.broadcasted_iota(jnp.int32, (tm, tm), 0)
        c = lax.broadcasted_iota(jnp.int32, (tm, tm), 1)
        rank = _dot(jnp.where(c < r, 1.0, 0.0).astype(BF16), jnp.where(pick, 1.0, 0.0).astype(BF16))
        loc = off_ref[...] + rank
        lane = lax.broadcasted_iota(jnp.int32, comb.shape, 1).astype(F32)
        m1 = jnp.min(jnp.where(pick, lane, float(LANES)), axis=1, keepdims=True)
        first = pick & (lane == m1)
        second = pick & (lane != m1)
        l0 = jnp.sum(jnp.where(first, loc, 0.0), axis=1, keepdims=True)
        l1 = jnp.sum(jnp.where(second, loc + 1.0, 0.0), axis=1, keepdims=True) - 1.0
        l0b = jnp.broadcast_to(l0, comb.shape)
        l1b = jnp.broadcast_to(l1, comb.shape)
        l0_ref[...] = l0b
        l1_ref[...] = l1b
        lt_ref[0] = l0b.T
        lt_ref[1] = l1b.T
        wt_ref[0] = jnp.broadcast_to(jnp.sum(jnp.where(first, comb, 0.0), axis=1, keepdims=True), comb.shape)
        wt_ref[1] = jnp.broadcast_to(jnp.sum(jnp.where(second, comb, 0.0), axis=1, keepdims=True), comb.shape)

    s = (it * rb + lax.broadcasted_iota(jnp.int32, (rb, 1), 0)).astype(F32)
    p0 = jnp.where(lt_ref[0, 0:1, :] == s, 1.0, 0.0).astype(BF16)
    p1 = jnp.where(lt_ref[1, 0:1, :] == s, 1.0, 0.0).astype(BF16)
    a_ref[...] = _dot(p0 + p1, h_ref[...]).astype(BF16)
    w_ref[...] = _dot_exact(p0, wt_ref[0]) + _dot_exact(p1, wt_ref[1])


def _moe_gather(h, comb, tab, *, tm, row0):
    d = h.shape[1]
    t = h.shape[0] - row0
    o = row0 // tm
    rb, ni = MOE_RB, tab['ni']
    nrows = tab['nblk'] * rb + MOE_MB
    grid_spec = pltpu.PrefetchScalarGridSpec(
        num_scalar_prefetch=1, grid=(t // tm, ni),
        in_specs=[pl.BlockSpec((tm, d), lambda i, it, io: (i + o, 0)),
                  pl.BlockSpec((tm, LANES), lambda i, it, io: (i + o, 0)),
                  pl.BlockSpec((None, 1, LANES), lambda i, it, io: (i, 0, 0)),
                  pl.BlockSpec(memory_space=pl.ANY), pl.BlockSpec(memory_space=pl.ANY)],
        out_specs=[pl.BlockSpec((rb, d), lambda i, it, io: (io[i * ni + it], 0)),
                   pl.BlockSpec((rb, LANES), lambda i, it, io: (io[i * ni + it], 0)),
                   pl.BlockSpec((tm, LANES), lambda i, it, io: (i, 0)),
                   pl.BlockSpec((tm, LANES), lambda i, it, io: (i, 0))],
        scratch_shapes=[pltpu.VMEM((2, LANES, tm), F32), pltpu.VMEM((2, tm, LANES), F32)])
    return pl.pallas_call(
        functools.partial(_moe_gather_kernel, tm=tm, rb=rb), grid_spec=grid_spec,
        out_shape=[jax.ShapeDtypeStruct((nrows, d), BF16), jax.ShapeDtypeStruct((nrows, LANES), F32),
                   jax.ShapeDtypeStruct((t, LANES), F32), jax.ShapeDtypeStruct((t, LANES), F32)],
        input_output_aliases={4: 0, 5: 1},
        compiler_params=_cparams(("parallel", "arbitrary")), name="moe_gather",
    )(tab['iout'], h, comb, tab['off_row'], jnp.zeros((nrows, d), BF16), jnp.zeros((nrows, LANES), F32))


def _moe_grouped_kernel(*refs, nk, dual):
    if dual:
        mexp_ref, nv_ref, a_ref, w1_ref, w3_ref, ws_ref, out_ref, *accs = refs
    else:
        mexp_ref, nv_ref, a_ref, w1_ref, out_ref, *accs = refs
    del mexp_ref
    k = pl.program_id(2)

    def part():
        a = a_ref[...]
        if dual:
            return _dot(a, w1_ref[...]), _dot(a, w3_ref[...])
        return (_dot(a, w1_ref[...]),)

    def finish(r1, r3=None):
        if dual:
            out_ref[...] = (_silu(r1) * r3 * ws_ref[:, 0:1]).astype(BF16)
        else:
            out_ref[...] = r1.astype(BF16)

    @pl.when(pl.program_id(0) < nv_ref[0])
    def _():
        _accumulate(k, nk, part, accs, finish)


def _moe_grouped(a, ws, w_sorted, tab, *, tn, tk, name):
    rows, kdim = a.shape
    n = ws[0].shape[-1]
    tn, tk = _tile(n, tn), _tile(kdim, tk)
    nj, nk = n // tn, kdim // tk
    nmb = tab['nblk'] * MOE_RB // MOE_MB
    dual = len(ws) == 2

    def live(m, nv):
        return m < nv[0]

    def me(m, nv):
        return jnp.minimum(m, nv[0] - 1)

    a_spec = pl.BlockSpec((MOE_MB, tk), lambda m, j, k, ex, nv: (me(m, nv), jnp.where(live(m, nv), k, nk - 1)))
    w_spec = pl.BlockSpec((None, tk, tn), lambda m, j, k, ex, nv: (
        ex[me(m, nv)], jnp.where(live(m, nv), k, nk - 1), jnp.where(live(m, nv), j, nj - 1)))
    o_spec = pl.BlockSpec((MOE_MB, tn), lambda m, j, k, ex, nv: (me(m, nv), jnp.where(live(m, nv), j, nj - 1)))
    in_specs = [a_spec, w_spec]
    args = [a, ws[0]]
    if dual:
        in_specs += [w_spec, pl.BlockSpec((MOE_MB, LANES), lambda m, j, k, ex, nv: (me(m, nv), 0))]
        args += [ws[1], w_sorted]
    scratch = [pltpu.VMEM((MOE_MB, tn), F32)] * (len(ws) if nk > 1 else 0)
    grid_spec = pltpu.PrefetchScalarGridSpec(
        num_scalar_prefetch=2, grid=(nmb, nj, nk), in_specs=in_specs, out_specs=o_spec,
        scratch_shapes=scratch)
    return pl.pallas_call(
        functools.partial(_moe_grouped_kernel, nk=nk, dual=dual), grid_spec=grid_spec,
        out_shape=jax.ShapeDtypeStruct((rows, n), BF16),
        compiler_params=_cparams(("arbitrary", "arbitrary", "arbitrary")), name=name,
    )(tab['mexp'], tab['nvalid'], *args)


def _moe_scatter_kernel(iblk_ref, l0_ref, l1_ref, *rest, rb, ni, seg, tile0):
    del iblk_ref
    y_refs = rest[:ni]
    x_ref, mod_ref, out_ref, pt_ref, ya_ref = rest[ni:]
    row = _mod_row(pl.program_id(0) + tile0, *seg)

    @pl.when(pl.program_id(1) == 0)
    def _():
        l0, l1 = l0_ref[...], l1_ref[...]
        lane = lax.broadcasted_iota(jnp.int32, (1, rb), 1).astype(F32)
        for q in range(ni):
            s = lane + float(q * rb)
            pt_ref[:, q * rb:(q + 1) * rb] = jnp.where((l0 == s) | (l1 == s), 1.0, 0.0).astype(BF16)

    for q in range(ni):
        ya_ref[q * rb:(q + 1) * rb, :] = y_refs[q][...]
    out_ref[...] = x_ref[...] + mod_ref[pl.ds(row, 1), :] * _dot(pt_ref[...], ya_ref[...])


def _moe_scatter(l0, l1, ys, x, mod, gate_off, tab, *, tm, tn, seg_fn, row0):
    t, d = x.shape
    rb, ni = MOE_RB, tab['ni']
    tn = _tile(d, tn)
    gblk = gate_off // tn
    o = row0 // tm
    lspec = pl.BlockSpec((tm, LANES), lambda i, j, ib: (i, 0))
    yspecs = [pl.BlockSpec((rb, tn), lambda i, j, ib, q=q: (ib[i * ni + q], j)) for q in range(ni)]
    grid_spec = pltpu.PrefetchScalarGridSpec(
        num_scalar_prefetch=1, grid=((t - row0) // tm, d // tn),
        in_specs=[lspec, lspec, *yspecs,
                  pl.BlockSpec((tm, tn), lambda i, j, ib: (i + o, j)),
                  pl.BlockSpec((MOD_ROWS, tn), lambda i, j, ib: (0, gblk + j))],
        out_specs=pl.BlockSpec((tm, tn), lambda i, j, ib: (i + o, j)),
        scratch_shapes=[pltpu.VMEM((tm, ni * rb), BF16), pltpu.VMEM((ni * rb, tn), BF16)])
    return pl.pallas_call(
        functools.partial(_moe_scatter_kernel, rb=rb, ni=ni, seg=seg_fn(tm), tile0=o), grid_spec=grid_spec,
        out_shape=jax.ShapeDtypeStruct((t, d), F32),
        compiler_params=_cparams(("parallel", "arbitrary")), name="moe_scatter",
    )(tab['iblk'], l0, l1, *([ys] * ni), x, mod)


def kernel(x, c, ctx, c_ctx, mod_w, mod_b, norm1_w, norm2_w, in_w, conv_w, conv_b, hgrn_lb, hgrn_norm_w,
           mlstm_igate_b, mlstm_fgate_b, mlstm_norm_w, ssd_a_log, ssd_dt_bias, ssd_d, ssd_norm_w,
           gate_w, gate_b, branch_w, out_w, ffn_w1, ffn_w3, ffn_w2, router_w, moe_w1, moe_w3, moe_w2,
           final_norm_w):
    nb, seq, d = x.shape
    ctx_len = ctx.shape[1]
    depth = mod_w.shape[0]
    bw = d // 2
    hg_heads = bw // HGRN_DK
    ml_dv = bw // MLSTM_HEADS
    ml_dk = ml_dv // 2
    ss_heads = bw // SSD_HEADDIM
    rank = gate_w.shape[1]
    n_ml_qk = MLSTM_HEADS * ml_dk
    n_ss_bc = SSD_GROUPS * SSD_STATE
    off = {}
    pos = 0
    for nm, sz in (('ml_q', n_ml_qk), ('ml_k', n_ml_qk), ('ss_x', bw), ('ss_B', n_ss_bc), ('ss_C', n_ss_bc),
                   ('hg_q', bw), ('hg_f_fwd', bw), ('hg_f_bwd', bw), ('hg_i', bw), ('hg_g', bw),
                   ('ml_v', bw), ('ml_z', bw), ('ml_gates', 4 * MLSTM_HEADS), ('ss_z', bw),
                   ('ss_dt', 2 * ss_heads), ('merge', rank)):
        off[nm] = (pos, sz)
        pos += sz
    n_conv = off['hg_q'][0]
    n_main = off['ml_gates'][0]
    n_small = 4 * MLSTM_HEADS + 2 * ss_heads
    assert n_small <= LANES and ctx_len % GRID_W == 0 and seq % ctx_len == 0
    t_off_z, t_off_merge, t_off_small = 0, bw, bw + rank
    n_tail_raw = bw + rank + LANES
    n_tail = -(-n_tail_raw // 512) * 512

    tb = ctx_len
    ncb, nlb = 1, seq // tb
    n_ctx_rows = nb * ctx_len
    t = n_ctx_rows + nb * seq
    ctx_row = nb

    def seg_fn(tile):
        assert n_ctx_rows % tile == 0 and seq % tile == 0
        return (n_ctx_rows // tile, seq // tile, ctx_row)

    tm_big = _tile(math.gcd(n_ctx_rows, seq), 1024)

    c_all = jnp.zeros((MOD_ROWS, d), F32).at[:nb].set(c).at[ctx_row].set(c_ctx)

    in_w_bf = in_w.astype(BF16)
    w_tail = jnp.concatenate(
        [lax.slice_in_dim(in_w, off[nm][0], off[nm][0] + off[nm][1], axis=2)
         for nm in ('ss_z', 'merge', 'ml_gates', 'ss_dt')]
        + [jnp.zeros((depth, d, n_tail - n_tail_raw + LANES - n_small), F32)], axis=2).astype(BF16)
    lb_cum = jnp.cumsum(jax.nn.softmax(hgrn_lb.astype(F32), axis=0), axis=0)
    lower_bounds = lb_cum - lb_cum[0]

    for l in range(depth):
        r0 = n_ctx_rows if l == depth - 1 else 0
        mod = _mm(c_all, mod_w, w_lead=l, out_dtype=F32, tm=MOD_ROWS, tn=1024, tk=2048,
                  a_silu=True, bias=mod_b[l][None, :], name="mod")
        if l == 0:
            h, xs = _norm_mod_join(ctx.reshape(n_ctx_rows, d), x.reshape(nb * seq, d), norm1_w[l][None, :],
                                   mod, sh_off=0, sc_off=d, seg_fn=seg_fn, tr=tb)
        else:
            h = _norm_mod(xs, norm1_w[l][None, :], mod, sh_off=0, sc_off=d, seg_fn=seg_fn, tr=tb,
                          name="norm1")
        p_main = _mm(h, in_w_bf, w_lead=l, n=n_main, out_dtype=F32, tm=tm_big, tn=1024, tk=d,
                     name="in_proj_main")
        p_tail = _mm(h, w_tail, w_lead=l, out_dtype=F32, tm=tm_big, tn=n_tail // 2, tk=2048,
                     name="in_proj_tail")
        conv = _conv(p_main, conv_w[l].reshape(9, n_conv), conv_b[l][None, :], n_conv=n_conv, tc=tb,
                     ctx_len=ctx_len, n_ctx_tiles=n_ctx_rows // tb, tiles_per_img=seq // tb, cb=512)

        hgf, hgb = _hgrn_scan(p_main, lower_bounds[l][None, :], nb=nb, ncb=ncb, nlb=nlb, tb=tb,
                              heads=hg_heads, off_q=off['hg_q'][0], off_ff=off['hg_f_fwd'][0],
                              off_fb=off['hg_f_bwd'][0], off_i=off['hg_i'][0])
        ml_bias = jnp.zeros((1, LANES), F32).at[0, :4 * MLSTM_HEADS].set(
            jnp.stack([mlstm_igate_b[l, 0], mlstm_fgate_b[l, 0],
                       mlstm_igate_b[l, 1], mlstm_fgate_b[l, 1]]).reshape(-1))
        mlf, mlb = _mlstm_scan(conv, p_main, p_tail, ml_bias, nb=nb, ncb=ncb, nlb=nlb, tb=tb,
                               dk=ml_dk, dv=ml_dv, off_q=off['ml_q'][0], off_k=off['ml_k'][0],
                               off_v=off['ml_v'][0], off_small=t_off_small)
        lo = 4 * MLSTM_HEADS
        dtb_row = jnp.zeros((1, LANES), F32).at[0, lo:lo + 2 * ss_heads].set(ssd_dt_bias[l].reshape(-1))
        alog_row = jnp.zeros((1, LANES), F32).at[0, lo:lo + 2 * ss_heads].set(ssd_a_log[l].reshape(-1))
        ssf, ssb = _ssd_scan(conv, p_tail, dtb_row, alog_row, nb=nb, ncb=ncb, nlb=nlb, tb=tb,
                             heads=ss_heads, off_x=off['ss_x'][0], off_b=off['ss_B'][0],
                             off_c=off['ss_C'][0], off_small=t_off_small)

        y_hg = _headnorm(hgf, hgb, p_main, off['hg_g'][0], hgrn_norm_w[l][None, :], hd=HGRN_DK,
                         tr=2 * tb, cb=1024, name="hgrn_finish")
        y_ml = _headnorm(mlf, mlb, p_main, off['ml_z'][0], mlstm_norm_w[l][None, :], hd=ml_dv,
                         tr=2 * tb, cb=1024, name="mlstm_finish")
        d_row = jnp.repeat(ssd_d[l], SSD_HEADDIM)[None, :]
        y_ss = _ssd_finish(ssf, ssb, conv, off['ss_x'][0], p_tail, t_off_z, d_row,
                           ssd_norm_w[l][None, :], tr=tb)
        ym = _merge((y_hg, y_ml, y_ss), branch_w[l].astype(BF16), p_tail, t_off_merge, rank,
                    gate_w[l].astype(BF16), gate_b[l][None, :], tm=tm_big, tn=1024, row0=r0)
        xs = _mm(ym, out_w[l].astype(BF16), out_dtype=F32, tm=tm_big, tn=1024, tk=d,
                 resid=(xs, mod, 2 * d, seg_fn), row0=r0, name="out_proj")

        i = l // 2
        if l % 2 == 0:
            h2 = _norm_mod(xs, norm2_w[l][None, :], mod, sh_off=3 * d, sc_off=4 * d, seg_fn=seg_fn,
                           tr=tb, row0=r0, name="norm2")
            mid = _swiglu_up(h2, ffn_w1[i].astype(BF16), ffn_w3[i].astype(BF16),
                             tm=tm_big, tn=1024, tk=2048, row0=r0, name="ffn_up")
            xs = _mm(mid, ffn_w2[i].astype(BF16), out_dtype=F32, tm=tm_big, tn=1024, tk=2048,
                     resid=(xs, mod, 5 * d, seg_fn), row0=r0, name="ffn_down")
        else:
            rw = jnp.zeros((d, LANES), F32).at[:, :N_EXPERTS].set(router_w[i])
            h2, comb, cnt = _norm_mod(xs, norm2_w[l][None, :], mod, sh_off=3 * d, sc_off=4 * d,
                                      seg_fn=seg_fn, tr=tb, router_w=rw, row0=r0, name="norm2_router")
            cnt_tile = cnt.reshape(t // tm_big, tm_big // tb, SUBLANES, LANES)[r0 // tm_big:, :, 0, :N_EXPERTS]
            tab = _moe_tables(jnp.sum(cnt_tile, axis=1).astype(jnp.int32), tm_big)
            a_s, w_s, l0, l1 = _moe_gather(h2, comb, tab, tm=tm_big, row0=r0)
            mid = _moe_grouped(a_s, (moe_w1[i].astype(BF16), moe_w3[i].astype(BF16)), w_s, tab,
                               tn=1024, tk=d, name="moe_up")
            ys = _moe_grouped(mid, (moe_w2[i].astype(BF16),), None, tab, tn=1024, tk=mid.shape[1],
                              name="moe_down")
            xs = _moe_scatter(l0, l1, ys, xs, mod, 5 * d, tab, tm=tm_big, tn=1024, seg_fn=seg_fn, row0=r0)

    out = _final_norm(xs, final_norm_w[None, :], row0=n_ctx_rows, rows=nb * seq, tr=tb)
    return out.reshape(nb, seq, d)
```

```python
import functools
import math

import jax
import jax.numpy as jnp
from jax import lax
from jax.experimental import pallas as pl
from jax.experimental.pallas import tpu as pltpu

F32 = jnp.float32
BF16 = jnp.bfloat16

GRID_W = 64
EPS = 1e-6
HGRN_DK = 128
HGRN_HEADS_PER_STEP = 8
MLSTM_HEADS = 8
SSD_HEADDIM = 64
SSD_GROUPS = 8
SSD_STATE = 128
N_EXPERTS = 8
LANES = 128
SUBLANES = 8
LOG2E = 1.4426950408889634
MOD_ROWS = 8
VMEM_LIMIT = 56 * 1024 * 1024


def _cparams(sem):
    return pltpu.CompilerParams(dimension_semantics=sem, vmem_limit_bytes=VMEM_LIMIT)


def _tile(n, pref):
    t = min(n, pref)
    while n % t:
        t //= 2
    return t


def _sigmoid(x):
    return 1.0 / (1.0 + jnp.exp(-x))


def _silu(x):
    return x * _sigmoid(x)


def _log_sigmoid(x):
    return jnp.minimum(x, 0.0) - jnp.log1p(jnp.exp(-jnp.abs(x)))


def _softplus(x):
    return jnp.maximum(x, 0.0) + jnp.log1p(jnp.exp(-jnp.abs(x)))


def _dot(a, b):
    return jnp.dot(a, b, preferred_element_type=F32)


def _dot_nt(a, b):
    return lax.dot_general(a, b, (((1,), (1,)), ((), ())), preferred_element_type=F32)


def _split3(x):
    x1 = x.astype(BF16)
    r = x - x1.astype(F32)
    x2 = r.astype(BF16)
    x3 = (r - x2.astype(F32)).astype(BF16)
    return x1, x2, x3


def _dot_exact(m01, x):
    x1, x2, x3 = _split3(x)
    return _dot(m01, x1) + _dot(m01, x2) + _dot(m01, x3)


def _dot_exact_rhs(x, m01):
    x1, x2, x3 = _split3(x)
    return _dot(x1, m01) + _dot(x2, m01) + _dot(x3, m01)


def _mod_row(tile, n_ctx_tiles, tiles_per_batch, ctx_row):
    return jnp.where(tile < n_ctx_tiles, ctx_row, (tile - n_ctx_tiles) // tiles_per_batch)


def _accumulate(k, nk, part, acc_refs, finish):
    if nk == 1:
        finish(*part())
        return

    @pl.when(k == 0)
    def _():
        for acc, p in zip(acc_refs, part()):
            acc[...] = p

    @pl.when((k > 0) & (k < nk - 1))
    def _():
        for acc, p in zip(acc_refs, part()):
            acc[...] += p

    @pl.when(k == nk - 1)
    def _():
        finish(*[acc[...] + p for acc, p in zip(acc_refs, part())])


def _mm_kernel(*refs, nk, a_silu, has_bias, resid, seg, tile0):
    it = iter(refs)
    a_ref, w_ref = next(it), next(it)
    bias_ref = next(it) if has_bias else None
    x_ref = next(it) if resid else None
    mod_ref = next(it) if resid else None
    out_ref = next(it)
    acc_ref = next(it) if nk > 1 else None
    k = pl.program_id(2)
    row = _mod_row(pl.program_id(0) + tile0, *seg) if resid else None

    def part():
        a = a_ref[...]
        if a_silu:
            a = _silu(a.astype(F32))
        return (_dot(a.astype(BF16), w_ref[...].astype(BF16)),)

    def finish(r):
        if has_bias:
            r = r + bias_ref[...]
        if resid:
            r = x_ref[...] + mod_ref[pl.ds(row, 1), :] * r
        out_ref[...] = r.astype(out_ref.dtype)

    _accumulate(k, nk, part, (acc_ref,), finish)


def _mm(a, w, *, out_dtype, tm, tn, tk, w_lead=None, n=None, a_silu=False, bias=None,
        resid=None, row0=0, name):
    m, kdim = a.shape
    n = w.shape[-1] if n is None else n
    tm, tn, tk = _tile(math.gcd(m, row0), tm), _tile(n, tn), _tile(kdim, tk)
    nk = kdim // tk
    o = row0 // tm
    if w_lead is None:
        w_spec = pl.BlockSpec((tk, tn), lambda i, j, k: (k, j))
    else:
        w_spec = pl.BlockSpec((None, tk, tn), lambda i, j, k: (w_lead, k, j))
    in_specs = [pl.BlockSpec((tm, tk), lambda i, j, k: (i + o, k)), w_spec]
    args = [a, w]
    if bias is not None:
        in_specs.append(pl.BlockSpec((1, tn), lambda i, j, k: (0, j)))
        args.append(bias)
    seg = None
    if resid is not None:
        x, mod, gate_off, seg_fn = resid
        seg = seg_fn(tm)
        gblk = gate_off // tn
        in_specs.append(pl.BlockSpec((tm, tn), lambda i, j, k: (i + o, j)))
        in_specs.append(pl.BlockSpec((MOD_ROWS, tn), lambda i, j, k: (0, gblk + j)))
        args += [x, mod]
    return pl.pallas_call(
        functools.partial(_mm_kernel, nk=nk, a_silu=a_silu, has_bias=bias is not None,
                          resid=resid is not None, seg=seg, tile0=o),
        grid=((m - row0) // tm, n // tn, nk),
        in_specs=in_specs,
        out_specs=pl.BlockSpec((tm, tn), lambda i, j, k: (i + o, j)),
        out_shape=jax.ShapeDtypeStruct((m, n), out_dtype),
        scratch_shapes=[pltpu.VMEM((tm, tn), F32)] if nk > 1 else [],
        compiler_params=_cparams(("parallel", "parallel", "arbitrary")),
        name=name,
    )(*args)


def _norm_mod_kernel(*refs, d, sh_off, sc_off, seg, router, tile0):
    if router:
        x_ref, nw_ref, mod_ref, rw_ref, out_ref, comb_ref, cnt_ref = refs
    else:
        x_ref, nw_ref, mod_ref, out_ref = refs
    row = _mod_row(pl.program_id(0) + tile0, *seg)
    x = x_ref[...]
    y = x * lax.rsqrt(jnp.mean(x * x, axis=-1, keepdims=True) + EPS) * nw_ref[...]
    sc = mod_ref[pl.ds(row, 1), sc_off:sc_off + d]
    sh = mod_ref[pl.ds(row, 1), sh_off:sh_off + d]
    h = y * (1.0 + sc) + sh
    out_ref[...] = h.astype(BF16)
    if router:
        h1, h2, h3 = _split3(h)
        r1, r2, r3 = _split3(rw_ref[...])
        logits = (_dot(h1, r1) + _dot(h1, r2) + _dot(h2, r1)
                  + _dot(h2, r2) + _dot(h1, r3) + _dot(h3, r1))
        lane = lax.broadcasted_iota(jnp.int32, logits.shape, 1).astype(F32)
        valid = lane < N_EXPERTS
        logits = jnp.where(valid, logits, -jnp.inf)
        mx = jnp.max(logits, axis=-1, keepdims=True)
        e = jnp.exp(logits - mx)
        probs = e / jnp.sum(e, axis=-1, keepdims=True)
        p1 = jnp.max(probs, axis=-1, keepdims=True)
        i1 = jnp.min(jnp.where((probs == p1) & valid, lane, float(LANES)), axis=-1, keepdims=True)
        rest = jnp.where((lane == i1) | (lane >= N_EXPERTS), -1.0, probs)
        p2 = jnp.max(rest, axis=-1, keepdims=True)
        i2 = jnp.min(jnp.where(rest == p2, lane, float(LANES)), axis=-1, keepdims=True)
        tot = p1 + p2
        comb = jnp.where(lane == i1, p1 / tot, jnp.where(lane == i2, p2 / tot, 0.0))
        comb_ref[...] = comb
        cnt = jnp.sum(jnp.where(comb > 0.0, 1.0, 0.0), axis=0, keepdims=True)
        cnt_ref[...] = jnp.broadcast_to(cnt, cnt_ref.shape)


def _norm_mod(x, nw, mod, *, sh_off, sc_off, seg_fn, tr, router_w=None, row0=0, name):
    t, d = x.shape
    tr = _tile(math.gcd(t, row0), tr)
    o = row0 // tr
    router = router_w is not None
    in_specs = [pl.BlockSpec((tr, d), lambda i: (i + o, 0)),
                pl.BlockSpec((1, d), lambda i: (0, 0)),
                pl.BlockSpec(mod.shape, lambda i: (0, 0))]
    args = [x, nw, mod]
    out_specs = [pl.BlockSpec((tr, d), lambda i: (i + o, 0))]
    out_shape = [jax.ShapeDtypeStruct((t, d), BF16)]
    if router:
        in_specs.append(pl.BlockSpec(router_w.shape, lambda i: (0, 0)))
        args.append(router_w)
        out_specs.append(pl.BlockSpec((tr, LANES), lambda i: (i + o, 0)))
        out_shape.append(jax.ShapeDtypeStruct((t, LANES), F32))
        out_specs.append(pl.BlockSpec((SUBLANES, LANES), lambda i: (i + o, 0)))
        out_shape.append(jax.ShapeDtypeStruct((t // tr * SUBLANES, LANES), F32))
    res = pl.pallas_call(
        functools.partial(_norm_mod_kernel, d=d, sh_off=sh_off, sc_off=sc_off, seg=seg_fn(tr),
                          router=router, tile0=o),
        grid=((t - row0) // tr,), in_specs=in_specs, out_specs=out_specs, out_shape=out_shape,
        compiler_params=_cparams(("parallel",)), name=name,
    )(*args)
    return res if router else res[0]


def _norm_mod_join_kernel(ctx_ref, lat_ref, nw_ref, mod_ref, out_ref, xs_ref, *, d, sh_off, sc_off, seg):
    i = pl.program_id(0)
    row = _mod_row(i, *seg)
    x = jnp.where(i < seg[0], ctx_ref[...], lat_ref[...])
    xs_ref[...] = x
    y = x * lax.rsqrt(jnp.mean(x * x, axis=-1, keepdims=True) + EPS) * nw_ref[...]
    sc = mod_ref[pl.ds(row, 1), sc_off:sc_off + d]
    sh = mod_ref[pl.ds(row, 1), sh_off:sh_off + d]
    out_ref[...] = (y * (1.0 + sc) + sh).astype(BF16)


def _norm_mod_join(ctx2d, lat2d, nw, mod, *, sh_off, sc_off, seg_fn, tr):
    d = ctx2d.shape[1]
    t = ctx2d.shape[0] + lat2d.shape[0]
    seg = seg_fn(tr)
    nct = seg[0]
    return pl.pallas_call(
        functools.partial(_norm_mod_join_kernel, d=d, sh_off=sh_off, sc_off=sc_off, seg=seg),
        grid=(t // tr,),
        in_specs=[pl.BlockSpec((tr, d), lambda i: (jnp.minimum(i, nct - 1), 0)),
                  pl.BlockSpec((tr, d), lambda i: (jnp.maximum(i - nct, 0), 0)),
                  pl.BlockSpec((1, d), lambda i: (0, 0)),
                  pl.BlockSpec(mod.shape, lambda i: (0, 0))],
        out_specs=[pl.BlockSpec((tr, d), lambda i: (i, 0)), pl.BlockSpec((tr, d), lambda i: (i, 0))],
        out_shape=[jax.ShapeDtypeStruct((t, d), BF16), jax.ShapeDtypeStruct((t, d), F32)],
        compiler_params=_cparams(("arbitrary",)), name="norm1_join",
    )(ctx2d, lat2d, nw, mod)


def _final_norm_kernel(x_ref, w_ref, out_ref):
    x = x_ref[...]
    out_ref[...] = x * lax.rsqrt(jnp.mean(x * x, axis=-1, keepdims=True) + EPS) * w_ref[...]


def _final_norm(x, w, *, row0, rows, tr):
    d = x.shape[1]
    tr = _tile(math.gcd(row0, rows), tr)
    off = row0 // tr
    return pl.pallas_call(
        _final_norm_kernel, grid=(rows // tr,),
        in_specs=[pl.BlockSpec((tr, d), lambda i: (i + off, 0)),
                  pl.BlockSpec((1, d), lambda i: (0, 0))],
        out_specs=pl.BlockSpec((tr, d), lambda i: (i, 0)),
        out_shape=jax.ShapeDtypeStruct((rows, d), F32),
        compiler_params=_cparams(("parallel",)), name="final_norm",
    )(x, w)


def _conv_kernel(main_ref, prev_ref, next_ref, w_ref, b_ref, out_ref, z_ref, *,
                 tc, ctx_len, n_ctx_tiles, tiles_per_img):
    i = pl.program_id(0)
    is_ctx = i < n_ctx_tiles
    li = i - n_ctx_tiles
    first = (li % tiles_per_img) == 0
    last = (li % tiles_per_img) == tiles_per_img - 1
    z_ref[0:GRID_W, :] = jnp.where(is_ctx | first, 0.0, prev_ref[...])
    z_ref[GRID_W:GRID_W + tc, :] = main_ref[...]
    z_ref[GRID_W + tc:, :] = jnp.where(is_ctx | last, 0.0, next_ref[...])
    w = w_ref[...]

    def column_sum(dc):
        s = w[3 + dc:4 + dc, :] * z_ref[GRID_W:GRID_W + tc, :]
        for dr in (0, 2):
            wt = jnp.where(is_ctx, 0.0, w[3 * dr + dc:3 * dr + dc + 1, :])
            s = s + wt * z_ref[GRID_W * dr:GRID_W * dr + tc, :]
        return s

    pos = lax.broadcasted_iota(jnp.int32, (tc, 1), 0)
    col = jnp.where(is_ctx, pos % ctx_len, pos % GRID_W)
    width = jnp.where(is_ctx, ctx_len, GRID_W)
    left = jnp.where(col != 0, pltpu.roll(column_sum(0), 1, 0), 0.0)
    right = jnp.where(col != width - 1, pltpu.roll(column_sum(2), tc - 1, 0), 0.0)
    out_ref[...] = _silu(b_ref[...] + column_sum(1) + left + right)


def _conv(p_main, conv_w9, conv_b, *, n_conv, tc, ctx_len, n_ctx_tiles, tiles_per_img, cb):
    t = p_main.shape[0]
    cb = _tile(n_conv, cb)
    rpt = tc // GRID_W
    nrow = t // GRID_W
    return pl.pallas_call(
        functools.partial(_conv_kernel, tc=tc, ctx_len=ctx_len, n_ctx_tiles=n_ctx_tiles,
                          tiles_per_img=tiles_per_img),
        grid=(t // tc, n_conv // cb),
        in_specs=[pl.BlockSpec((tc, cb), lambda i, j: (i, j)),
                  pl.BlockSpec((GRID_W, cb), lambda i, j: (jnp.maximum(i * rpt - 1, 0), j)),
                  pl.BlockSpec((GRID_W, cb), lambda i, j: (jnp.minimum((i + 1) * rpt, nrow - 1), j)),
                  pl.BlockSpec((9, cb), lambda i, j: (0, j)),
                  pl.BlockSpec((1, cb), lambda i, j: (0, j))],
        out_specs=pl.BlockSpec((tc, cb), lambda i, j: (i, j)),
        out_shape=jax.ShapeDtypeStruct((t, n_conv), F32),
        scratch_shapes=[pltpu.VMEM((tc + 2 * GRID_W, cb), F32)],
        compiler_params=_cparams(("parallel", "parallel")), name="conv_silu",
    )(p_main, p_main, p_main, conv_w9, conv_b)


def _rowblk(b, s, rev, ncb, nlb, nb):
    if rev:
        ctx = b * ncb + (ncb - 1 - s)
        lat = nb * ncb + b * nlb + (nlb - 1 - (s - ncb))
    else:
        ctx = b * ncb + s
        lat = nb * ncb + b * nlb + (s - ncb)
    return jnp.where(s < ncb, ctx, lat)


def _tri(n, rev):
    r = lax.broadcasted_iota(jnp.int32, (n, n), 0)
    c = lax.broadcasted_iota(jnp.int32, (n, n), 1)
    return (c >= r) if rev else (c <= r)


def _level_ref(bl, half, rev):
    tb, dk = bl.shape
    blk = 2 * half
    idx = half if rev else half - 1
    if blk == tb:
        return bl[idx:idx + 1, :]
    if blk >= SUBLANES:
        b3 = bl.reshape(tb // blk, blk, dk)
        return jnp.broadcast_to(b3[:, idx:idx + 1, :], b3.shape).reshape(tb, dk)
    tmod = lax.broadcasted_iota(jnp.int32, (tb, 1), 0) % blk
    r = bl
    for m in range(blk):
        if idx != m:
            r = jnp.where(tmod == m, pltpu.roll(bl, (m - idx) % tb, 0), r)
    return r


def _hgrn_dir(q_ref, u_ref, v_ref, lb_ref, o_ref, st_ref, d, rev, tb, hps):
    ri = lax.broadcasted_iota(jnp.int32, (tb, tb), 0)
    cj = lax.broadcasted_iota(jnp.int32, (tb, tb), 1)
    keep = jnp.where((cj >= ri) if rev else (cj <= ri), 1.0, 0.0).astype(BF16)
    half = tb // 2
    ri = lax.broadcasted_iota(jnp.int32, (half, half), 0)
    cj = lax.broadcasted_iota(jnp.int32, (half, half), 1)
    level = jnp.where((cj > ri) if rev else (cj < ri), 31 - lax.clz(ri ^ cj), -1)
    for h in range(hps):
        _hgrn_head(q_ref, u_ref, v_ref, lb_ref, o_ref, st_ref, d, h, rev, tb, keep, level)


def _hgrn_head(q_ref, u_ref, v_ref, lb_ref, o_ref, st_ref, d, h, rev, tb, keep, level):
    dk = HGRN_DK
    cols = slice(h * dk, (h + 1) * dk)
    lbv = lb_ref[:, cols]
    qraw = q_ref[:, cols]
    u = u_ref[:, cols]
    v = v_ref[:, cols]
    q = _silu(qraw) * dk ** -0.5
    la = jnp.log(lbv)
    lc = jnp.log1p(-lbv) + _log_sigmoid(u)
    mx = jnp.maximum(la, lc)
    mn = jnp.minimum(la, lc)
    logf = mx + jnp.log1p(jnp.exp(mn - mx))
    k = (1.0 - lbv) * _sigmoid(-u)

    bl = _dot_exact(keep, logf * LOG2E)
    b_end = bl[0:1] if rev else bl[tb - 1:tb]
    half = tb // 2
    halves = (slice(0, half), slice(half, tb))
    scores = [jnp.zeros((half, half), F32), jnp.zeros((half, half), F32)]
    for lv in range(half.bit_length() - 1):
        dq = bl - _level_ref(bl, 1 << lv, rev)
        qt = (q * jnp.exp2(dq)).astype(BF16)
        kt = (k * jnp.exp2(-dq)).astype(BF16)
        for hb, rows in enumerate(halves):
            scores[hb] = jnp.where(level == lv, _dot_nt(qt[rows], kt[rows]), scores[hb])
    isl, jsl = (halves[0], halves[1]) if rev else (halves[1], halves[0])
    dq = bl - _level_ref(bl, half, rev)
    cross = _dot_nt((q[isl] * jnp.exp2(dq[isl])).astype(BF16), (k[jsl] * jnp.exp2(-dq[jsl])).astype(BF16))
    vb = v.astype(BF16)
    st = st_ref[d, h]
    rest = (jnp.sum(q * k, axis=-1, keepdims=True) * v
            + _dot_nt((q * jnp.exp2(bl)).astype(BF16), st.astype(BF16)))
    for hb, rows in enumerate(halves):
        o = _dot(scores[hb].astype(BF16), vb[rows]) + rest[rows]
        if rows == isl:
            o = o + _dot(cross.astype(BF16), vb[jsl])
        o_ref[rows, cols] = o
    st_ref[d, h] = st * jnp.exp2(b_end) + _dot(v.T.astype(BF16), (k * jnp.exp2(b_end - bl)).astype(BF16))


def _hgrn_kernel(qf, uf, vf, qb, ub, vb, lb_ref, of, ob, st_ref, *, tb, hps):
    @pl.when(pl.program_id(2) == 0)
    def _():
        st_ref[...] = jnp.zeros_like(st_ref)
    _hgrn_dir(qf, uf, vf, lb_ref, of, st_ref, 0, False, tb, hps)
    _hgrn_dir(qb, ub, vb, lb_ref, ob, st_ref, 1, True, tb, hps)


def _hgrn_scan(p_main, lb_row, *, nb, ncb, nlb, tb, heads, off_q, off_ff, off_fb, off_i):
    t = p_main.shape[0]
    dk = HGRN_DK

    hps = HGRN_HEADS_PER_STEP
    wd = hps * dk

    def spec(off, rev):
        return pl.BlockSpec((tb, wd), lambda b, h, s: (_rowblk(b, s, rev, ncb, nlb, nb), off // wd + h))

    def ospec(rev):
        return pl.BlockSpec((tb, wd), lambda b, h, s: (_rowblk(b, s, rev, ncb, nlb, nb), h))

    out = jax.ShapeDtypeStruct((t, heads * dk), F32)
    return pl.pallas_call(
        functools.partial(_hgrn_kernel, tb=tb, hps=hps),
        grid=(nb, heads // hps, ncb + nlb),
        in_specs=[spec(off_q, False), spec(off_ff, False), spec(off_i, False),
                  spec(off_q, True), spec(off_fb, True), spec(off_i, True),
                  pl.BlockSpec((1, wd), lambda b, h, s: (0, h))],
        out_specs=[ospec(False), ospec(True)],
        out_shape=[out, out],
        scratch_shapes=[pltpu.VMEM((2, hps, dk, dk), F32)],
        compiler_params=_cparams(("parallel", "parallel", "arbitrary")), name="hgrn_scan",
    )(p_main, p_main, p_main, p_main, p_main, p_main, lb_row)


def _mlstm_dir(q_ref, k_ref, v_ref, g_ref, bias_ref, o_ref, c_ref, n_ref, m_ref, d, rev, tb, dk, dv):
    g = g_ref[...] + bias_ref[...]
    lf = _log_sigmoid(g) * LOG2E
    g = g * LOG2E
    keep = _tri(tb, rev)
    b = _dot_exact(jnp.where(keep, 1.0, 0.0).astype(BF16), lf)
    bt, gt = b.T, g.T
    for h in range(MLSTM_HEADS):
        ci = 2 * MLSTM_HEADS * d + h
        cf = ci + MLSTM_HEADS
        q = q_ref[:, h * dk:(h + 1) * dk]
        ks = k_ref[:, h * dk:(h + 1) * dk] * dk ** -0.5
        v = v_ref[:, h * dv:(h + 1) * dv]
        b_col, ic_col = b[:, cf:cf + 1], g[:, ci:ci + 1]
        b_row, ic_row = bt[cf:cf + 1, :], gt[ci:ci + 1, :]
        b_end = b_col[0:1] if rev else b_col[tb - 1:tb]
        m_prev = m_ref[d, h][:, 0:1]
        w_in = jnp.where(keep, b_col - b_row + ic_row, -jnp.inf)
        w_st = b_col + m_prev
        m_row = jnp.maximum(jnp.max(w_in, axis=1, keepdims=True), w_st)
        qb = q.astype(BF16)
        vb = v.astype(BF16)
        p = jnp.exp2(w_in - m_row) * _dot_nt(qb, ks.astype(BF16))
        e_st = jnp.exp2(w_st - m_row)
        num = _dot(p.astype(BF16), vb) + e_st * _dot(qb, c_ref[d, h].astype(BF16))
        nrm = (jnp.sum(p, axis=1, keepdims=True)
               + e_st * jnp.sum(q * n_ref[d, h], axis=1, keepdims=True))
        o_ref[:, h * dv:(h + 1) * dv] = num / jnp.maximum(jnp.abs(nrm), jnp.exp2(-m_row))
        w_end = b_end - b_col + ic_col
        m_new = jnp.maximum(b_end + m_prev, jnp.max(w_end, axis=0, keepdims=True))
        s_old = jnp.exp2(b_end + m_prev - m_new)
        kt = ks * jnp.exp2(w_end - m_new)
        c_ref[d, h] = s_old * c_ref[d, h] + _dot(kt.T.astype(BF16), vb)
        n_ref[d, h] = s_old * n_ref[d, h] + jnp.sum(kt, axis=0, keepdims=True)
        m_ref[d, h] = jnp.broadcast_to(m_new, m_ref.shape[2:])


def _mlstm_kernel(qf, kf, vf, gf, qb, kb, vb, gb, bias_ref, of, ob, c_ref, n_ref, m_ref, *, tb, dk, dv):
    @pl.when(pl.program_id(1) == 0)
    def _():
        c_ref[...] = jnp.zeros_like(c_ref)
        n_ref[...] = jnp.zeros_like(n_ref)
        m_ref[...] = jnp.zeros_like(m_ref)
    _mlstm_dir(qf, kf, vf, gf, bias_ref, of, c_ref, n_ref, m_ref, 0, False, tb, dk, dv)
    _mlstm_dir(qb, kb, vb, gb, bias_ref, ob, c_ref, n_ref, m_ref, 1, True, tb, dk, dv)


def _mlstm_scan(conv, p_main, p_tail, bias_row, *, nb, ncb, nlb, tb, dk, dv,
                off_q, off_k, off_v, off_small):
    t = conv.shape[0]

    nh = MLSTM_HEADS

    def spec(width, off, rev):
        return pl.BlockSpec((tb, width), lambda b, s: (_rowblk(b, s, rev, ncb, nlb, nb), off // width))

    def ospec(rev):
        return pl.BlockSpec((tb, nh * dv), lambda b, s: (_rowblk(b, s, rev, ncb, nlb, nb), 0))

    out = jax.ShapeDtypeStruct((t, nh * dv), F32)
    return pl.pallas_call(
        functools.partial(_mlstm_kernel, tb=tb, dk=dk, dv=dv),
        grid=(nb, ncb + nlb),
        in_specs=[spec(nh * dk, off_q, False), spec(nh * dk, off_k, False), spec(nh * dv, off_v, False),
                  spec(LANES, off_small, False),
                  spec(nh * dk, off_q, True), spec(nh * dk, off_k, True), spec(nh * dv, off_v, True),
                  spec(LANES, off_small, True),
                  pl.BlockSpec((1, LANES), lambda b, s: (0, 0))],
        out_specs=[ospec(False), ospec(True)],
        out_shape=[out, out],
        scratch_shapes=[pltpu.VMEM((2, nh, dk, dv), F32), pltpu.VMEM((2, nh, 1, dk), F32),
                        pltpu.VMEM((2, nh, 1, LANES), F32)],
        compiler_params=_cparams(("parallel", "arbitrary")), name="mlstm_scan",
    )(conv, conv, p_main, p_tail, conv, conv, p_main, p_tail, bias_row)


def _ssd_dir(c_ref, bm_ref, x_ref, g_ref, dtb_ref, alog_ref, o_ref, s_ref, d, rev, tb, heads, hpg):
    dt = _softplus(g_ref[...] + dtb_ref[...])
    la = -jnp.exp(alog_ref[...]) * dt
    keep = _tri(tb, rev)
    b = _dot_exact(jnp.where(keep, 1.0, 0.0).astype(BF16), la * LOG2E)
    bt = b.T
    width = hpg * SSD_HEADDIM
    head0 = MLSTM_HEADS * 4 + heads * d
    er = lax.broadcasted_iota(jnp.int32, (LANES, heads * SSD_HEADDIM), 0)
    ec = lax.broadcasted_iota(jnp.int32, (LANES, heads * SSD_HEADDIM), 1)
    spread = jnp.where(er == head0 + ec // SSD_HEADDIM, 1.0, 0.0).astype(BF16)
    b_wide = _dot_exact_rhs(b, spread)
    dt_wide = _dot_exact_rhs(dt, spread)
    for grp in range(SSD_GROUPS):
        lane0 = head0 + hpg * grp
        cm = c_ref[:, grp * SSD_STATE:(grp + 1) * SSD_STATE].astype(BF16)
        bm = bm_ref[:, grp * SSD_STATE:(grp + 1) * SSD_STATE]
        x = x_ref[:, grp * width:(grp + 1) * width]
        b_all = b_wide[:, grp * width:(grp + 1) * width]
        dt_all = dt_wide[:, grp * width:(grp + 1) * width]
        b_end = b_all[0:1] if rev else b_all[tb - 1:tb]
        v_all = x * dt_all
        v_bf = v_all.astype(BF16)
        gmat = _dot_nt(cm, bm.astype(BF16))
        s_old = s_ref[d, grp]
        inter = jnp.exp2(b_all) * _dot(cm, s_old.astype(BF16))
        outs = []
        for i in range(hpg):
            c = lane0 + i
            dec = jnp.where(keep, jnp.exp2(b[:, c:c + 1] - bt[c:c + 1, :]), 0.0)
            outs.append(_dot((gmat * dec).astype(BF16), v_bf[:, i * SSD_HEADDIM:(i + 1) * SSD_HEADDIM]))
        o_ref[:, grp * width:(grp + 1) * width] = jnp.concatenate(outs, axis=1) + inter
        sv = (v_all * jnp.exp2(b_end - b_all)).astype(BF16)
        s_ref[d, grp] = jnp.exp2(b_end) * s_old + _dot(bm.T.astype(BF16), sv)


def _ssd_kernel(cf, bf, xf, gf, cb, bb, xb, gb, dtb_ref, alog_ref, of, ob, s_ref, *, tb, heads, hpg):
    @pl.when(pl.program_id(1) == 0)
    def _():
        s_ref[...] = jnp.zeros_like(s_ref)
    _ssd_dir(cf, bf, xf, gf, dtb_ref, alog_ref, of, s_ref, 0, False, tb, heads, hpg)
    _ssd_dir(cb, bb, xb, gb, dtb_ref, alog_ref, ob, s_ref, 1, True, tb, heads, hpg)


def _ssd_scan(conv, p_tail, dtb_row, alog_row, *, nb, ncb, nlb, tb, heads, off_x, off_b, off_c, off_small):
    t = conv.shape[0]
    hpg = heads // SSD_GROUPS
    width = hpg * SSD_HEADDIM

    ng = SSD_GROUPS

    def spec(w, off, rev):
        return pl.BlockSpec((tb, w), lambda b, s: (_rowblk(b, s, rev, ncb, nlb, nb), off // w))

    def ospec(rev):
        return pl.BlockSpec((tb, ng * width), lambda b, s: (_rowblk(b, s, rev, ncb, nlb, nb), 0))

    row = pl.BlockSpec((1, LANES), lambda b, s: (0, 0))
    out = jax.ShapeDtypeStruct((t, heads * SSD_HEADDIM), F32)
    return pl.pallas_call(
        functools.partial(_ssd_kernel, tb=tb, heads=heads, hpg=hpg),
        grid=(nb, ncb + nlb),
        in_specs=[spec(ng * SSD_STATE, off_c, False), spec(ng * SSD_STATE, off_b, False),
                  spec(ng * width, off_x, False), spec(LANES, off_small, False),
                  spec(ng * SSD_STATE, off_c, True), spec(ng * SSD_STATE, off_b, True),
                  spec(ng * width, off_x, True), spec(LANES, off_small, True), row, row],
        out_specs=[ospec(False), ospec(True)],
        out_shape=[out, out],
        scratch_shapes=[pltpu.VMEM((2, ng, SSD_STATE, width), F32)],
        compiler_params=_cparams(("parallel", "arbitrary")), name="ssd_scan",
    )(conv, conv, conv, p_tail, conv, conv, conv, p_tail, dtb_row, alog_row)


def _headnorm_kernel(of_ref, ob_ref, g_ref, w_ref, out_ref, *, hd):
    cb = of_ref.shape[1]
    for i in range(cb // hd):
        sl = slice(i * hd, (i + 1) * hd)
        o = of_ref[:, sl] + ob_ref[:, sl]
        y = o * lax.rsqrt(jnp.mean(o * o, axis=-1, keepdims=True) + EPS) * w_ref[:, sl]
        out_ref[:, sl] = (y * _silu(g_ref[:, sl])).astype(BF16)


def _headnorm(of, ob, gsrc, goff, w, *, hd, tr, cb, name):
    t, wd = of.shape
    tr, cb = _tile(t, tr), _tile(wd, cb)
    gblk = goff // cb
    return pl.pallas_call(
        functools.partial(_headnorm_kernel, hd=hd), grid=(t // tr, wd // cb),
        in_specs=[pl.BlockSpec((tr, cb), lambda i, j: (i, j)),
                  pl.BlockSpec((tr, cb), lambda i, j: (i, j)),
                  pl.BlockSpec((tr, cb), lambda i, j: (i, gblk + j)),
                  pl.BlockSpec((1, cb), lambda i, j: (0, j))],
        out_specs=pl.BlockSpec((tr, cb), lambda i, j: (i, j)),
        out_shape=jax.ShapeDtypeStruct((t, wd), BF16),
        compiler_params=_cparams(("parallel", "parallel")), name=name,
    )(of, ob, gsrc, w)


def _ssd_finish_kernel(of_ref, ob_ref, x_ref, z_ref, d_ref, w_ref, out_ref):
    y = of_ref[...] + ob_ref[...] + d_ref[...] * x_ref[...]
    tt = y * _silu(z_ref[...])
    out_ref[...] = (tt * lax.rsqrt(jnp.mean(tt * tt, axis=-1, keepdims=True) + EPS)
                    * w_ref[...]).astype(BF16)


def _ssd_finish(of, ob, conv, off_x, p_tail, off_z, d_row, w, *, tr):
    t, wd = of.shape
    tr = _tile(t, tr)
    row = pl.BlockSpec((1, wd), lambda i: (0, 0))
    return pl.pallas_call(
        _ssd_finish_kernel, grid=(t // tr,),
        in_specs=[pl.BlockSpec((tr, wd), lambda i: (i, 0)),
                  pl.BlockSpec((tr, wd), lambda i: (i, 0)),
                  pl.BlockSpec((tr, wd), lambda i: (i, off_x // wd)),
                  pl.BlockSpec((tr, wd), lambda i: (i, off_z // wd)), row, row],
        out_specs=pl.BlockSpec((tr, wd), lambda i: (i, 0)),
        out_shape=jax.ShapeDtypeStruct((t, wd), BF16),
        compiler_params=_cparams(("parallel",)), name="ssd_finish",
    )(of, ob, conv, p_tail, d_row, w)


def _merge_kernel(y0, y1, y2, bw_ref, mg_ref, gw_ref, gb_ref, out_ref, acc_ref):
    k = pl.program_id(2)
    gate = _sigmoid(_dot(mg_ref[...].astype(BF16), gw_ref[...]) + gb_ref[...])
    for idx, y_ref in enumerate((y0, y1, y2)):
        @pl.when(k == idx)
        def _(y_ref=y_ref, idx=idx):
            r = gate * _dot(y_ref[...], bw_ref[...])
            if idx == 0:
                acc_ref[...] = r
            elif idx == 1:
                acc_ref[...] += r
            else:
                out_ref[...] = (acc_ref[...] + r).astype(BF16)


def _merge(ys, bw, p_tail, off_merge, rank, gw, gb, *, tm, tn, row0):
    t, bwid = ys[0].shape
    d = bw.shape[-1]
    tm, tn = _tile(math.gcd(t, row0), tm), _tile(d, tn)
    nj = d // tn
    o = row0 // tm
    yspec = pl.BlockSpec((tm, bwid), lambda i, j, k: (i + o, 0))
    return pl.pallas_call(
        _merge_kernel, grid=((t - row0) // tm, nj, 3),
        in_specs=[yspec, yspec, yspec,
                  pl.BlockSpec((None, bwid, tn), lambda i, j, k: (k, 0, j)),
                  pl.BlockSpec((tm, rank), lambda i, j, k: (i + o, off_merge // rank)),
                  pl.BlockSpec((rank, tn), lambda i, j, k: (0, k * nj + j)),
                  pl.BlockSpec((1, tn), lambda i, j, k: (0, k * nj + j))],
        out_specs=pl.BlockSpec((tm, tn), lambda i, j, k: (i + o, j)),
        out_shape=jax.ShapeDtypeStruct((t, d), BF16),
        scratch_shapes=[pltpu.VMEM((tm, tn), F32)],
        compiler_params=_cparams(("parallel", "parallel", "arbitrary")), name="branch_merge",
    )(ys[0], ys[1], ys[2], bw, p_tail, gw, gb)


def _swiglu_kernel(a_ref, w1_ref, w3_ref, out_ref, *accs, nk):
    def part():
        a = a_ref[...]
        return _dot(a, w1_ref[...]), _dot(a, w3_ref[...])

    def finish(r1, r3):
        out_ref[...] = (_silu(r1) * r3).astype(BF16)

    _accumulate(pl.program_id(2), nk, part, accs, finish)


def _swiglu_up(a, w1, w3, *, tm, tn, tk, row0, name):
    t, kdim = a.shape
    n = w1.shape[1]
    tm, tn, tk = _tile(math.gcd(t, row0), tm), _tile(n, tn), _tile(kdim, tk)
    nk = kdim // tk
    o = row0 // tm
    wspec = pl.BlockSpec((tk, tn), lambda i, j, k: (k, j))
    return pl.pallas_call(
        functools.partial(_swiglu_kernel, nk=nk),
        grid=((t - row0) // tm, n // tn, nk),
        in_specs=[pl.BlockSpec((tm, tk), lambda i, j, k: (i + o, k)), wspec, wspec],
        out_specs=pl.BlockSpec((tm, tn), lambda i, j, k: (i + o, j)),
        out_shape=jax.ShapeDtypeStruct((t, n), BF16),
        scratch_shapes=[pltpu.VMEM((tm, tn), F32)] * (2 if nk > 1 else 0),
        compiler_params=_cparams(("parallel", "parallel", "arbitrary")), name=name,
    )(a, w1, w3)


MOE_RB = LANES
MOE_MB = 512


def _moe_tables(cnt_tile, tm):
    nt, ne = cnt_tile.shape
    rb, bpm = MOE_RB, MOE_MB // MOE_RB
    ni = 2 * tm // rb + ne
    nblk = -(-(nt * ni + ne * (bpm - 1)) // bpm) * bpm
    nbk = (cnt_tile + (rb - 1)) // rb
    reg_e = (jnp.sum(nbk, axis=0) + (bpm - 1)) // bpm * bpm
    end_e = jnp.cumsum(reg_e)
    start_e = end_e - reg_e
    pre_ie = jnp.cumsum(nbk, axis=0) - nbk
    ends_ie = jnp.cumsum(nbk, axis=1)
    off_ie = ends_ie - nbk
    n_items = ends_ie[:, -1]
    it = jnp.arange(ni, dtype=jnp.int32)[None, :]
    itc = jnp.minimum(it, n_items[:, None] - 1)
    e_idx = jnp.sum((itc[:, :, None] >= ends_ie[:, None, :]).astype(jnp.int32), axis=-1)
    e_idx = jnp.minimum(e_idx, ne - 1)
    chunk = itc - jnp.take_along_axis(off_ie, e_idx, axis=1)
    gblk = start_e[e_idx] + jnp.take_along_axis(pre_ie, e_idx, axis=1) + chunk
    iout = jnp.where(it < n_items[:, None], gblk, nblk)
    m = jnp.arange(nblk // bpm, dtype=jnp.int32)
    mexp = jnp.sum((m[:, None] * bpm >= end_e[None, :]).astype(jnp.int32), axis=-1)
    off_row = jnp.zeros((nt, 1, LANES), F32).at[:, 0, :ne].set((off_ie * rb).astype(F32))
    return dict(ni=ni, nblk=nblk, iout=iout.reshape(-1).astype(jnp.int32),
                iblk=gblk.reshape(-1).astype(jnp.int32), n_items=n_items.astype(jnp.int32),
                mexp=jnp.minimum(mexp, ne - 1).astype(jnp.int32),
                nvalid=(end_e[-1:] // bpm).astype(jnp.int32), off_row=off_row)


def _moe_gather_kernel(iout_ref, h_ref, comb_ref, off_ref, za_ref, zw_ref,
                       a_ref, w_ref, l0_ref, l1_ref, lt_ref, wt_ref, *, tm, rb):
    del iout_ref, za_ref, zw_ref
    it = pl.program_id(1)

    @pl.when(it == 0)
    def _():
        comb = comb_ref[...]
        pick = comb > 0.0
        r = lax.broadcasted_iota(jnp.int32, (tm, tm), 0)
        c = lax.broadcasted_iota(jnp.int32, (tm, tm), 1)
        rank = _dot(jnp.where(c < r, 1.0, 0.0).astype(BF16), jnp.where(pick, 1.0, 0.0).astype(BF16))
        loc = off_ref[...] + rank
        lane = lax.broadcasted_iota(jnp.int32, comb.shape, 1).astype(F32)
        m1 = jnp.min(jnp.where(pick, lane, float(LANES)), axis=1, keepdims=True)
        first = pick & (lane == m1)
        second = pick & (lane != m1)
        l0 = jnp.sum(jnp.where(first, loc, 0.0), axis=1, keepdims=True)
        l1 = jnp.sum(jnp.where(second, loc + 1.0, 0.0), axis=1, keepdims=True) - 1.0
        l0b = jnp.broadcast_to(l0, comb.shape)
        l1b = jnp.broadcast_to(l1, comb.shape)
        l0_ref[...] = l0b
        l1_ref[...] = l1b
        lt_ref[0] = l0b.T
        lt_ref[1] = l1b.T
        wt_ref[0] = jnp.broadcast_to(jnp.sum(jnp.where(first, comb, 0.0), axis=1, keepdims=True), comb.shape)
        wt_ref[1] = jnp.broadcast_to(jnp.sum(jnp.where(second, comb, 0.0), axis=1, keepdims=True), comb.shape)

    s = (it * rb + lax.broadcasted_iota(jnp.int32, (rb, 1), 0)).astype(F32)
    p0 = jnp.where(lt_ref[0, 0:1, :] == s, 1.0, 0.0).astype(BF16)
    p1 = jnp.where(lt_ref[1, 0:1, :] == s, 1.0, 0.0).astype(BF16)
    a_ref[...] = _dot(p0 + p1, h_ref[...]).astype(BF16)
    w_ref[...] = _dot_exact(p0, wt_ref[0]) + _dot_exact(p1, wt_ref[1])


def _moe_gather(h, comb, tab, *, tm, row0):
    d = h.shape[1]
    t = h.shape[0] - row0
    o = row0 // tm
    rb, ni = MOE_RB, tab['ni']
    nrows = tab['nblk'] * rb + MOE_MB
    grid_spec = pltpu.PrefetchScalarGridSpec(
        num_scalar_prefetch=1, grid=(t // tm, ni),
        in_specs=[pl.BlockSpec((tm, d), lambda i, it, io: (i + o, 0)),
                  pl.BlockSpec((tm, LANES), lambda i, it, io: (i + o, 0)),
                  pl.BlockSpec((None, 1, LANES), lambda i, it, io: (i, 0, 0)),
                  pl.BlockSpec(memory_space=pl.ANY), pl.BlockSpec(memory_space=pl.ANY)],
        out_specs=[pl.BlockSpec((rb, d), lambda i, it, io: (io[i * ni + it], 0)),
                   pl.BlockSpec((rb, LANES), lambda i, it, io: (io[i * ni + it], 0)),
                   pl.BlockSpec((tm, LANES), lambda i, it, io: (i, 0)),
                   pl.BlockSpec((tm, LANES), lambda i, it, io: (i, 0))],
        scratch_shapes=[pltpu.VMEM((2, LANES, tm), F32), pltpu.VMEM((2, tm, LANES), F32)])
    return pl.pallas_call(
        functools.partial(_moe_gather_kernel, tm=tm, rb=rb), grid_spec=grid_spec,
        out_shape=[jax.ShapeDtypeStruct((nrows, d), BF16), jax.ShapeDtypeStruct((nrows, LANES), F32),
                   jax.ShapeDtypeStruct((t, LANES), F32), jax.ShapeDtypeStruct((t, LANES), F32)],
        input_output_aliases={4: 0, 5: 1},
        compiler_params=_cparams(("parallel", "arbitrary")), name="moe_gather",
    )(tab['iout'], h, comb, tab['off_row'], jnp.zeros((nrows, d), BF16), jnp.zeros((nrows, LANES), F32))


def _moe_grouped_kernel(*refs, nk, dual):
    if dual:
        mexp_ref, nv_ref, a_ref, w1_ref, w3_ref, ws_ref, out_ref, *accs = refs
    else:
        mexp_ref, nv_ref, a_ref, w1_ref, out_ref, *accs = refs
    del mexp_ref
    k = pl.program_id(2)

    def part():
        a = a_ref[...]
        if dual:
            return _dot(a, w1_ref[...]), _dot(a, w3_ref[...])
        return (_dot(a, w1_ref[...]),)

    def finish(r1, r3=None):
        if dual:
            out_ref[...] = (_silu(r1) * r3 * ws_ref[:, 0:1]).astype(BF16)
        else:
            out_ref[...] = r1.astype(BF16)

    @pl.when(pl.program_id(0) < nv_ref[0])
    def _():
        _accumulate(k, nk, part, accs, finish)


def _moe_grouped(a, ws, w_sorted, tab, *, tn, tk, name):
    rows, kdim = a.shape
    n = ws[0].shape[-1]
    tn, tk = _tile(n, tn), _tile(kdim, tk)
    nj, nk = n // tn, kdim // tk
    nmb = tab['nblk'] * MOE_RB // MOE_MB
    dual = len(ws) == 2

    def live(m, nv):
        return m < nv[0]

    def me(m, nv):
        return jnp.minimum(m, nv[0] - 1)

    a_spec = pl.BlockSpec((MOE_MB, tk), lambda m, j, k, ex, nv: (me(m, nv), jnp.where(live(m, nv), k, nk - 1)))
    w_spec = pl.BlockSpec((None, tk, tn), lambda m, j, k, ex, nv: (
        ex[me(m, nv)], jnp.where(live(m, nv), k, nk - 1), jnp.where(live(m, nv), j, nj - 1)))
    o_spec = pl.BlockSpec((MOE_MB, tn), lambda m, j, k, ex, nv: (me(m, nv), jnp.where(live(m, nv), j, nj - 1)))
    in_specs = [a_spec, w_spec]
    args = [a, ws[0]]
    if dual:
        in_specs += [w_spec, pl.BlockSpec((MOE_MB, LANES), lambda m, j, k, ex, nv: (me(m, nv), 0))]
        args += [ws[1], w_sorted]
    scratch = [pltpu.VMEM((MOE_MB, tn), F32)] * (len(ws) if nk > 1 else 0)
    grid_spec = pltpu.PrefetchScalarGridSpec(
        num_scalar_prefetch=2, grid=(nmb, nj, nk), in_specs=in_specs, out_specs=o_spec,
        scratch_shapes=scratch)
    return pl.pallas_call(
        functools.partial(_moe_grouped_kernel, nk=nk, dual=dual), grid_spec=grid_spec,
        out_shape=jax.ShapeDtypeStruct((rows, n), BF16),
        compiler_params=_cparams(("arbitrary", "arbitrary", "arbitrary")), name=name,
    )(tab['mexp'], tab['nvalid'], *args)


def _moe_scatter_kernel(iblk_ref, l0_ref, l1_ref, *rest, rb, ni, seg, tile0):
    del iblk_ref
    y_refs = rest[:ni]
    x_ref, mod_ref, out_ref, pt_ref, ya_ref = rest[ni:]
    row = _mod_row(pl.program_id(0) + tile0, *seg)

    @pl.when(pl.program_id(1) == 0)
    def _():
        l0, l1 = l0_ref[...], l1_ref[...]
        lane = lax.broadcasted_iota(jnp.int32, (1, rb), 1).astype(F32)
        for q in range(ni):
            s = lane + float(q * rb)
            pt_ref[:, q * rb:(q + 1) * rb] = jnp.where((l0 == s) | (l1 == s), 1.0, 0.0).astype(BF16)

    for q in range(ni):
        ya_ref[q * rb:(q + 1) * rb, :] = y_refs[q][...]
    out_ref[...] = x_ref[...] + mod_ref[pl.ds(row, 1), :] * _dot(pt_ref[...], ya_ref[...])


def _moe_scatter(l0, l1, ys, x, mod, gate_off, tab, *, tm, tn, seg_fn, row0):
    t, d = x.shape
    rb, ni = MOE_RB, tab['ni']
    tn = _tile(d, tn)
    gblk = gate_off // tn
    o = row0 // tm
    lspec = pl.BlockSpec((tm, LANES), lambda i, j, ib: (i, 0))
    yspecs = [pl.BlockSpec((rb, tn), lambda i, j, ib, q=q: (ib[i * ni + q], j)) for q in range(ni)]
    grid_spec = pltpu.PrefetchScalarGridSpec(
        num_scalar_prefetch=1, grid=((t - row0) // tm, d // tn),
        in_specs=[lspec, lspec, *yspecs,
                  pl.BlockSpec((tm, tn), lambda i, j, ib: (i + o, j)),
                  pl.BlockSpec((MOD_ROWS, tn), lambda i, j, ib: (0, gblk + j))],
        out_specs=pl.BlockSpec((tm, tn), lambda i, j, ib: (i + o, j)),
        scratch_shapes=[pltpu.VMEM((tm, ni * rb), BF16), pltpu.VMEM((ni * rb, tn), BF16)])
    return pl.pallas_call(
        functools.partial(_moe_scatter_kernel, rb=rb, ni=ni, seg=seg_fn(tm), tile0=o), grid_spec=grid_spec,
        out_shape=jax.ShapeDtypeStruct((t, d), F32),
        compiler_params=_cparams(("parallel", "arbitrary")), name="moe_scatter",
    )(tab['iblk'], l0, l1, *([ys] * ni), x, mod)


def kernel(x, c, ctx, c_ctx, mod_w, mod_b, norm1_w, norm2_w, in_w, conv_w, conv_b, hgrn_lb, hgrn_norm_w,
           mlstm_igate_b, mlstm_fgate_b, mlstm_norm_w, ssd_a_log, ssd_dt_bias, ssd_d, ssd_norm_w,
           gate_w, gate_b, branch_w, out_w, ffn_w1, ffn_w3, ffn_w2, router_w, moe_w1, moe_w3, moe_w2,
           final_norm_w):
    nb, seq, d = x.shape
    ctx_len = ctx.shape[1]
    depth = mod_w.shape[0]
    bw = d // 2
    hg_heads = bw // HGRN_DK
    ml_dv = bw // MLSTM_HEADS
    ml_dk = ml_dv // 2
    ss_heads = bw // SSD_HEADDIM
    rank = gate_w.shape[1]
    n_ml_qk = MLSTM_HEADS * ml_dk
    n_ss_bc = SSD_GROUPS * SSD_STATE
    off = {}
    pos = 0
    for nm, sz in (('ml_q', n_ml_qk), ('ml_k', n_ml_qk), ('ss_x', bw), ('ss_B', n_ss_bc), ('ss_C', n_ss_bc),
                   ('hg_q', bw), ('hg_f_fwd', bw), ('hg_f_bwd', bw), ('hg_i', bw), ('hg_g', bw),
                   ('ml_v', bw), ('ml_z', bw), ('ml_gates', 4 * MLSTM_HEADS), ('ss_z', bw),
                   ('ss_dt', 2 * ss_heads), ('merge', rank)):
        off[nm] = (pos, sz)
        pos += sz
    n_conv = off['hg_q'][0]
    n_main = off['ml_gates'][0]
    n_small = 4 * MLSTM_HEADS + 2 * ss_heads
    assert n_small <= LANES and ctx_len % GRID_W == 0 and seq % ctx_len == 0
    t_off_z, t_off_merge, t_off_small = 0, bw, bw + rank
    n_tail_raw = bw + rank + LANES
    n_tail = -(-n_tail_raw // 512) * 512

    tb = ctx_len
    ncb, nlb = 1, seq // tb
    n_ctx_rows = nb * ctx_len
    t = n_ctx_rows + nb * seq
    ctx_row = nb

    def seg_fn(tile):
        assert n_ctx_rows % tile == 0 and seq % tile == 0
        return (n_ctx_rows // tile, seq // tile, ctx_row)

    tm_big = _tile(math.gcd(n_ctx_rows, seq), 1024)

    c_all = jnp.zeros((MOD_ROWS, d), F32).at[:nb].set(c).at[ctx_row].set(c_ctx)

    in_w_bf = in_w.astype(BF16)
    w_tail = jnp.concatenate(
        [lax.slice_in_dim(in_w, off[nm][0], off[nm][0] + off[nm][1], axis=2)
         for nm in ('ss_z', 'merge', 'ml_gates', 'ss_dt')]
        + [jnp.zeros((depth, d, n_tail - n_tail_raw + LANES - n_small), F32)], axis=2).astype(BF16)
    lb_cum = jnp.cumsum(jax.nn.softmax(hgrn_lb.astype(F32), axis=0), axis=0)
    lower_bounds = lb_cum - lb_cum[0]

    for l in range(depth):
        r0 = n_ctx_rows if l == depth - 1 else 0
        mod = _mm(c_all, mod_w, w_lead=l, out_dtype=F32, tm=MOD_ROWS, tn=1024, tk=2048,
                  a_silu=True, bias=mod_b[l][None, :], name="mod")
        if l == 0:
            h, xs = _norm_mod_join(ctx.reshape(n_ctx_rows, d), x.reshape(nb * seq, d), norm1_w[l][None, :],
                                   mod, sh_off=0, sc_off=d, seg_fn=seg_fn, tr=tb)
        else:
            h = _norm_mod(xs, norm1_w[l][None, :], mod, sh_off=0, sc_off=d, seg_fn=seg_fn, tr=tb,
                          name="norm1")
        p_main = _mm(h, in_w_bf, w_lead=l, n=n_main, out_dtype=F32, tm=tm_big, tn=1024, tk=d,
                     name="in_proj_main")
        p_tail = _mm(h, w_tail, w_lead=l, out_dtype=F32, tm=tm_big, tn=n_tail // 2, tk=2048,
                     name="in_proj_tail")
        conv = _conv(p_main, conv_w[l].reshape(9, n_conv), conv_b[l][None, :], n_conv=n_conv, tc=tb,
                     ctx_len=ctx_len, n_ctx_tiles=n_ctx_rows // tb, tiles_per_img=seq // tb, cb=512)

        hgf, hgb = _hgrn_scan(p_main, lower_bounds[l][None, :], nb=nb, ncb=ncb, nlb=nlb, tb=tb,
                              heads=hg_heads, off_q=off['hg_q'][0], off_ff=off['hg_f_fwd'][0],
                              off_fb=off['hg_f_bwd'][0], off_i=off['hg_i'][0])
        ml_bias = jnp.zeros((1, LANES), F32).at[0, :4 * MLSTM_HEADS].set(
            jnp.stack([mlstm_igate_b[l, 0], mlstm_fgate_b[l, 0],
                       mlstm_igate_b[l, 1], mlstm_fgate_b[l, 1]]).reshape(-1))
        mlf, mlb = _mlstm_scan(conv, p_main, p_tail, ml_bias, nb=nb, ncb=ncb, nlb=nlb, tb=tb,
                               dk=ml_dk, dv=ml_dv, off_q=off['ml_q'][0], off_k=off['ml_k'][0],
                               off_v=off['ml_v'][0], off_small=t_off_small)
        lo = 4 * MLSTM_HEADS
        dtb_row = jnp.zeros((1, LANES), F32).at[0, lo:lo + 2 * ss_heads].set(ssd_dt_bias[l].reshape(-1))
        alog_row = jnp.zeros((1, LANES), F32).at[0, lo:lo + 2 * ss_heads].set(ssd_a_log[l].reshape(-1))
        ssf, ssb = _ssd_scan(conv, p_tail, dtb_row, alog_row, nb=nb, ncb=ncb, nlb=nlb, tb=tb,
                             heads=ss_heads, off_x=off['ss_x'][0], off_b=off['ss_B'][0],
                             off_c=off['ss_C'][0], off_small=t_off_small)

        y_hg = _headnorm(hgf, hgb, p_main, off['hg_g'][0], hgrn_norm_w[l][None, :], hd=HGRN_DK,
                         tr=2 * tb, cb=1024, name="hgrn_finish")
        y_ml = _headnorm(mlf, mlb, p_main, off['ml_z'][0], mlstm_norm_w[l][None, :], hd=ml_dv,
                         tr=2 * tb, cb=1024, name="mlstm_finish")
        d_row = jnp.repeat(ssd_d[l], SSD_HEADDIM)[None, :]
        y_ss = _ssd_finish(ssf, ssb, conv, off['ss_x'][0], p_tail, t_off_z, d_row,
                           ssd_norm_w[l][None, :], tr=tb)
        ym = _merge((y_hg, y_ml, y_ss), branch_w[l].astype(BF16), p_tail, t_off_merge, rank,
                    gate_w[l].astype(BF16), gate_b[l][None, :], tm=tm_big, tn=1024, row0=r0)
        xs = _mm(ym, out_w[l].astype(BF16), out_dtype=F32, tm=tm_big, tn=1024, tk=d,
                 resid=(xs, mod, 2 * d, seg_fn), row0=r0, name="out_proj")

        i = l // 2
        if l % 2 == 0:
            h2 = _norm_mod(xs, norm2_w[l][None, :], mod, sh_off=3 * d, sc_off=4 * d, seg_fn=seg_fn,
                           tr=tb, row0=r0, name="norm2")
            mid = _swiglu_up(h2, ffn_w1[i].astype(BF16), ffn_w3[i].astype(BF16),
                             tm=tm_big, tn=1024, tk=2048, row0=r0, name="ffn_up")
            xs = _mm(mid, ffn_w2[i].astype(BF16), out_dtype=F32, tm=tm_big, tn=1024, tk=2048,
                     resid=(xs, mod, 5 * d, seg_fn), row0=r0, name="ffn_down")
        else:
            rw = jnp.zeros((d, LANES), F32).at[:, :N_EXPERTS].set(router_w[i])
            h2, comb, cnt = _norm_mod(xs, norm2_w[l][None, :], mod, sh_off=3 * d, sc_off=4 * d,
                                      seg_fn=seg_fn, tr=tb, router_w=rw, row0=r0, name="norm2_router")
            cnt_tile = cnt.reshape(t // tm_big, tm_big // tb, SUBLANES, LANES)[r0 // tm_big:, :, 0, :N_EXPERTS]
            tab = _moe_tables(jnp.sum(cnt_tile, axis=1).astype(jnp.int32), tm_big)
            a_s, w_s, l0, l1 = _moe_gather(h2, comb, tab, tm=tm_big, row0=r0)
            mid = _moe_grouped(a_s, (moe_w1[i].astype(BF16), moe_w3[i].astype(BF16)), w_s, tab,
                               tn=1024, tk=d, name="moe_up")
            ys = _moe_grouped(mid, (moe_w2[i].astype(BF16),), None, tab, tn=1024, tk=mid.shape[1],
                              name="moe_down")
            xs = _moe_scatter(l0, l1, ys, xs, mod, 5 * d, tab, tm=tm_big, tn=1024, seg_fn=seg_fn, row0=r0)

    out = _final_norm(xs, final_norm_w[None, :], row0=n_ctx_rows, rows=nb * seq, tr=tb)
    return out.reshape(nb, seq, d)
```

```python
import functools
import math

import jax
import jax.numpy as jnp
from jax import lax
from jax.experimental import pallas as pl
from jax.experimental.pallas import tpu as pltpu

F32 = jnp.float32
BF16 = jnp.bfloat16

GRID_W = 64
EPS = 1e-6
HGRN_DK = 128
HGRN_HEADS_PER_STEP = 8
MLSTM_HEADS = 8
SSD_HEADDIM = 64
SSD_GROUPS = 8
SSD_STATE = 128
N_EXPERTS = 8
LANES = 128
SUBLANES = 8
LOG2E = 1.4426950408889634
MOD_ROWS = 8
VMEM_LIMIT = 56 * 1024 * 1024


def _cparams(sem):
    return pltpu.CompilerParams(dimension_semantics=sem, vmem_limit_bytes=VMEM_LIMIT)


def _tile(n, pref):
    t = min(n, pref)
    while n % t:
        t //= 2
    return t


def _sigmoid(x):
    return 1.0 / (1.0 + jnp.exp(-x))


def _silu(x):
    return x * _sigmoid(x)


def _log_sigmoid(x):
    return jnp.minimum(x, 0.0) - jnp.log1p(jnp.exp(-jnp.abs(x)))


def _softplus(x):
    return jnp.maximum(x, 0.0) + jnp.log1p(jnp.exp(-jnp.abs(x)))


def _dot(a, b):
    return jnp.dot(a, b, preferred_element_type=F32)


def _dot_nt(a, b):
    return lax.dot_general(a, b, (((1,), (1,)), ((), ())), preferred_element_type=F32)


def _split3(x):
    x1 = x.astype(BF16)
    r = x - x1.astype(F32)
    x2 = r.astype(BF16)
    x3 = (r - x2.astype(F32)).astype(BF16)
    return x1, x2, x3


def _dot_exact(m01, x):
    x1, x2, x3 = _split3(x)
    return _dot(m01, x1) + _dot(m01, x2) + _dot(m01, x3)


def _dot_exact_rhs(x, m01):
    x1, x2, x3 = _split3(x)
    return _dot(x1, m01) + _dot(x2, m01) + _dot(x3, m01)


def _mod_row(tile, n_ctx_tiles, tiles_per_batch, ctx_row):
    return jnp.where(tile < n_ctx_tiles, ctx_row, (tile - n_ctx_tiles) // tiles_per_batch)


def _accumulate(k, nk, part, acc_refs, finish):
    if nk == 1:
        finish(*part())
        return

    @pl.when(k == 0)
    def _():
        for acc, p in zip(acc_refs, part()):
            acc[...] = p

    @pl.when((k > 0) & (k < nk - 1))
    def _():
        for acc, p in zip(acc_refs, part()):
            acc[...] += p

    @pl.when(k == nk - 1)
    def _():
        finish(*[acc[...] + p for acc, p in zip(acc_refs, part())])


def _mm_kernel(*refs, nk, a_silu, has_bias, resid, seg, tile0):
    it = iter(refs)
    a_ref, w_ref = next(it), next(it)
    bias_ref = next(it) if has_bias else None
    x_ref = next(it) if resid else None
    mod_ref = next(it) if resid else None
    out_ref = next(it)
    acc_ref = next(it) if nk > 1 else None
    k = pl.program_id(2)
    row = _mod_row(pl.program_id(0) + tile0, *seg) if resid else None

    def part():
        a = a_ref[...]
        if a_silu:
            a = _silu(a.astype(F32))
        return (_dot(a.astype(BF16), w_ref[...].astype(BF16)),)

    def finish(r):
        if has_bias:
            r = r + bias_ref[...]
        if resid:
            r = x_ref[...] + mod_ref[pl.ds(row, 1), :] * r
        out_ref[...] = r.astype(out_ref.dtype)

    _accumulate(k, nk, part, (acc_ref,), finish)


def _mm(a, w, *, out_dtype, tm, tn, tk, w_lead=None, n=None, a_silu=False, bias=None,
        resid=None, row0=0, name):
    m, kdim = a.shape
    n = w.shape[-1] if n is None else n
    tm, tn, tk = _tile(math.gcd(m, row0), tm), _tile(n, tn), _tile(kdim, tk)
    nk = kdim // tk
    o = row0 // tm
    if w_lead is None:
        w_spec = pl.BlockSpec((tk, tn), lambda i, j, k: (k, j))
    else:
        w_spec = pl.BlockSpec((None, tk, tn), lambda i, j, k: (w_lead, k, j))
    in_specs = [pl.BlockSpec((tm, tk), lambda i, j, k: (i + o, k)), w_spec]
    args = [a, w]
    if bias is not None:
        in_specs.append(pl.BlockSpec((1, tn), lambda i, j, k: (0, j)))
        args.append(bias)
    seg = None
    if resid is not None:
        x, mod, gate_off, seg_fn = resid
        seg = seg_fn(tm)
        gblk = gate_off // tn
        in_specs.append(pl.BlockSpec((tm, tn), lambda i, j, k: (i + o, j)))
        in_specs.append(pl.BlockSpec((MOD_ROWS, tn), lambda i, j, k: (0, gblk + j)))
        args += [x, mod]
    return pl.pallas_call(
        functools.partial(_mm_kernel, nk=nk, a_silu=a_silu, has_bias=bias is not None,
                          resid=resid is not None, seg=seg, tile0=o),
        grid=((m - row0) // tm, n // tn, nk),
        in_specs=in_specs,
        out_specs=pl.BlockSpec((tm, tn), lambda i, j, k: (i + o, j)),
        out_shape=jax.ShapeDtypeStruct((m, n), out_dtype),
        scratch_shapes=[pltpu.VMEM((tm, tn), F32)] if nk > 1 else [],
        compiler_params=_cparams(("parallel", "parallel", "arbitrary")),
        name=name,
    )(*args)


def _norm_mod_kernel(*refs, d, sh_off, sc_off, seg, router, tile0):
    if router:
        x_ref, nw_ref, mod_ref, rw_ref, out_ref, comb_ref, cnt_ref = refs
    else:
        x_ref, nw_ref, mod_ref, out_ref = refs
    row = _mod_row(pl.program_id(0) + tile0, *seg)
    x = x_ref[...]
    y = x * lax.rsqrt(jnp.mean(x * x, axis=-1, keepdims=True) + EPS) * nw_ref[...]
    sc = mod_ref[pl.ds(row, 1), sc_off:sc_off + d]
    sh = mod_ref[pl.ds(row, 1), sh_off:sh_off + d]
    h = y * (1.0 + sc) + sh
    out_ref[...] = h.astype(BF16)
    if router:
        h1, h2, h3 = _split3(h)
        r1, r2, r3 = _split3(rw_ref[...])
        logits = (_dot(h1, r1) + _dot(h1, r2) + _dot(h2, r1)
                  + _dot(h2, r2) + _dot(h1, r3) + _dot(h3, r1))
        lane = lax.broadcasted_iota(jnp.int32, logits.shape, 1).astype(F32)
        valid = lane < N_EXPERTS
        logits = jnp.where(valid, logits, -jnp.inf)
        mx = jnp.max(logits, axis=-1, keepdims=True)
        e = jnp.exp(logits - mx)
        probs = e / jnp.sum(e, axis=-1, keepdims=True)
        p1 = jnp.max(probs, axis=-1, keepdims=True)
        i1 = jnp.min(jnp.where((probs == p1) & valid, lane, float(LANES)), axis=-1, keepdims=True)
        rest = jnp.where((lane == i1) | (lane >= N_EXPERTS), -1.0, probs)
        p2 = jnp.max(rest, axis=-1, keepdims=True)
        i2 = jnp.min(jnp.where(rest == p2, lane, float(LANES)), axis=-1, keepdims=True)
        tot = p1 + p2
        comb = jnp.where(lane == i1, p1 / tot, jnp.where(lane == i2, p2 / tot, 0.0))
        comb_ref[...] = comb
        cnt = jnp.sum(jnp.where(comb > 0.0, 1.0, 0.0), axis=0, keepdims=True)
        cnt_ref[...] = jnp.broadcast_to(cnt, cnt_ref.shape)


def _norm_mod(x, nw, mod, *, sh_off, sc_off, seg_fn, tr, router_w=None, row0=0, name):
    t, d = x.shape
    tr = _tile(math.gcd(t, row0), tr)
    o = row0 // tr
    router = router_w is not None
    in_specs = [pl.BlockSpec((tr, d), lambda i: (i + o, 0)),
                pl.BlockSpec((1, d), lambda i: (0, 0)),
                pl.BlockSpec(mod.shape, lambda i: (0, 0))]
    args = [x, nw, mod]
    out_specs = [pl.BlockSpec((tr, d), lambda i: (i + o, 0))]
    out_shape = [jax.ShapeDtypeStruct((t, d), BF16)]
    if router:
        in_specs.append(pl.BlockSpec(router_w.shape, lambda i: (0, 0)))
        args.append(router_w)
        out_specs.append(pl.BlockSpec((tr, LANES), lambda i: (i + o, 0)))
        out_shape.append(jax.ShapeDtypeStruct((t, LANES), F32))
        out_specs.append(pl.BlockSpec((SUBLANES, LANES), lambda i: (i + o, 0)))
        out_shape.append(jax.ShapeDtypeStruct((t // tr * SUBLANES, LANES), F32))
    res = pl.pallas_call(
        functools.partial(_norm_mod_kernel, d=d, sh_off=sh_off, sc_off=sc_off, seg=seg_fn(tr),
                          router=router, tile0=o),
        grid=((t - row0) // tr,), in_specs=in_specs, out_specs=out_specs, out_shape=out_shape,
        compiler_params=_cparams(("parallel",)), name=name,
    )(*args)
    return res if router else res[0]


def _norm_mod_join_kernel(ctx_ref, lat_ref, nw_ref, mod_ref, out_ref, xs_ref, *, d, sh_off, sc_off, seg):
    i = pl.program_id(0)
    row = _mod_row(i, *seg)
    x = jnp.where(i < seg[0], ctx_ref[...], lat_ref[...])
    xs_ref[...] = x
    y = x * lax.rsqrt(jnp.mean(x * x, axis=-1, keepdims=True) + EPS) * nw_ref[...]
    sc = mod_ref[pl.ds(row, 1), sc_off:sc_off + d]
    sh = mod_ref[pl.ds(row, 1), sh_off:sh_off + d]
    out_ref[...] = (y * (1.0 + sc) + sh).astype(BF16)


def _norm_mod_join(ctx2d, lat2d, nw, mod, *, sh_off, sc_off, seg_fn, tr):
    d = ctx2d.shape[1]
    t = ctx2d.shape[0] + lat2d.shape[0]
    seg = seg_fn(tr)
    nct = seg[0]
    return pl.pallas_call(
        functools.partial(_norm_mod_join_kernel, d=d, sh_off=sh_off, sc_off=sc_off, seg=seg),
        grid=(t // tr,),
        in_specs=[pl.BlockSpec((tr, d), lambda i: (jnp.minimum(i, nct - 1), 0)),
                  pl.BlockSpec((tr, d), lambda i: (jnp.maximum(i - nct, 0), 0)),
                  pl.BlockSpec((1, d), lambda i: (0, 0)),
                  pl.BlockSpec(mod.shape, lambda i: (0, 0))],
        out_specs=[pl.BlockSpec((tr, d), lambda i: (i, 0)), pl.BlockSpec((tr, d), lambda i: (i, 0))],
        out_shape=[jax.ShapeDtypeStruct((t, d), BF16), jax.ShapeDtypeStruct((t, d), F32)],
        compiler_params=_cparams(("arbitrary",)), name="norm1_join",
    )(ctx2d, lat2d, nw, mod)


def _final_norm_kernel(x_ref, w_ref, out_ref):
    x = x_ref[...]
    out_ref[...] = x * lax.rsqrt(jnp.mean(x * x, axis=-1, keepdims=True) + EPS) * w_ref[...]


def _final_norm(x, w, *, row0, rows, tr):
    d = x.shape[1]
    tr = _tile(math.gcd(row0, rows), tr)
    off = row0 // tr
    return pl.pallas_call(
        _final_norm_kernel, grid=(rows // tr,),
        in_specs=[pl.BlockSpec((tr, d), lambda i: (i + off, 0)),
                  pl.BlockSpec((1, d), lambda i: (0, 0))],
        out_specs=pl.BlockSpec((tr, d), lambda i: (i, 0)),
        out_shape=jax.ShapeDtypeStruct((rows, d), F32),
        compiler_params=_cparams(("parallel",)), name="final_norm",
    )(x, w)


def _conv_kernel(main_ref, prev_ref, next_ref, w_ref, b_ref, out_ref, z_ref, *,
                 tc, ctx_len, n_ctx_tiles, tiles_per_img):
    i = pl.program_id(0)
    is_ctx = i < n_ctx_tiles
    li = i - n_ctx_tiles
    first = (li % tiles_per_img) == 0
    last = (li % tiles_per_img) == tiles_per_img - 1
    z_ref[0:GRID_W, :] = jnp.where(is_ctx | first, 0.0, prev_ref[...])
    z_ref[GRID_W:GRID_W + tc, :] = main_ref[...]
    z_ref[GRID_W + tc:, :] = jnp.where(is_ctx | last, 0.0, next_ref[...])
    w = w_ref[...]

    def column_sum(dc):
        s = w[3 + dc:4 + dc, :] * z_ref[GRID_W:GRID_W + tc, :]
        for dr in (0, 2):
            wt = jnp.where(is_ctx, 0.0, w[3 * dr + dc:3 * dr + dc + 1, :])
            s = s + wt * z_ref[GRID_W * dr:GRID_W * dr + tc, :]
        return s

    pos = lax.broadcasted_iota(jnp.int32, (tc, 1), 0)
    col = jnp.where(is_ctx, pos % ctx_len, pos % GRID_W)
    width = jnp.where(is_ctx, ctx_len, GRID_W)
    left = jnp.where(col != 0, pltpu.roll(column_sum(0), 1, 0), 0.0)
    right = jnp.where(col != width - 1, pltpu.roll(column_sum(2), tc - 1, 0), 0.0)
    out_ref[...] = _silu(b_ref[...] + column_sum(1) + left + right)


def _conv(p_main, conv_w9, conv_b, *, n_conv, tc, ctx_len, n_ctx_tiles, tiles_per_img, cb):
    t = p_main.shape[0]
    cb = _tile(n_conv, cb)
    rpt = tc // GRID_W
    nrow = t // GRID_W
    return pl.pallas_call(
        functools.partial(_conv_kernel, tc=tc, ctx_len=ctx_len, n_ctx_tiles=n_ctx_tiles,
                          tiles_per_img=tiles_per_img),
        grid=(t // tc, n_conv // cb),
        in_specs=[pl.BlockSpec((tc, cb), lambda i, j: (i, j)),
                  pl.BlockSpec((GRID_W, cb), lambda i, j: (jnp.maximum(i * rpt - 1, 0), j)),
                  pl.BlockSpec((GRID_W, cb), lambda i, j: (jnp.minimum((i + 1) * rpt, nrow - 1), j)),
                  pl.BlockSpec((9, cb), lambda i, j: (0, j)),
                  pl.BlockSpec((1, cb), lambda i, j: (0, j))],
        out_specs=pl.BlockSpec((tc, cb), lambda i, j: (i, j)),
        out_shape=jax.ShapeDtypeStruct((t, n_conv), F32),
        scratch_shapes=[pltpu.VMEM((tc + 2 * GRID_W, cb), F32)],
        compiler_params=_cparams(("parallel", "parallel")), name="conv_silu",
    )(p_main, p_main, p_main, conv_w9, conv_b)


def _rowblk(b, s, rev, ncb, nlb, nb):
    if rev:
        ctx = b * ncb + (ncb - 1 - s)
        lat = nb * ncb + b * nlb + (nlb - 1 - (s - ncb))
    else:
        ctx = b * ncb + s
        lat = nb * ncb + b * nlb + (s - ncb)
    return jnp.where(s < ncb, ctx, lat)


def _tri(n, rev):
    r = lax.broadcasted_iota(jnp.int32, (n, n), 0)
    c = lax.broadcasted_iota(jnp.int32, (n, n), 1)
    return (c >= r) if rev else (c <= r)


def _level_ref(bl, half, rev):
    tb, dk = bl.shape
    blk = 2 * half
    idx = half if rev else half - 1
    if blk == tb:
        return bl[idx:idx + 1, :]
    if blk >= SUBLANES:
        b3 = bl.reshape(tb // blk, blk, dk)
        return jnp.broadcast_to(b3[:, idx:idx + 1, :], b3.shape).reshape(tb, dk)
    tmod = lax.broadcasted_iota(jnp.int32, (tb, 1), 0) % blk
    r = bl
    for m in range(blk):
        if idx != m:
            r = jnp.where(tmod == m, pltpu.roll(bl, (m - idx) % tb, 0), r)
    return r


def _hgrn_dir(q_ref, u_ref, v_ref, lb_ref, o_ref, st_ref, d, rev, tb, hps):
    ri = lax.broadcasted_iota(jnp.int32, (tb, tb), 0)
    cj = lax.broadcasted_iota(jnp.int32, (tb, tb), 1)
    keep = jnp.where((cj >= ri) if rev else (cj <= ri), 1.0, 0.0).astype(BF16)
    half = tb // 2
    ri = lax.broadcasted_iota(jnp.int32, (half, half), 0)
    cj = lax.broadcasted_iota(jnp.int32, (half, half), 1)
    level = jnp.where((cj > ri) if rev else (cj < ri), 31 - lax.clz(ri ^ cj), -1)
    for h in range(hps):
        _hgrn_head(q_ref, u_ref, v_ref, lb_ref, o_ref, st_ref, d, h, rev, tb, keep, level)


def _hgrn_head(q_ref, u_ref, v_ref, lb_ref, o_ref, st_ref, d, h, rev, tb, keep, level):
    dk = HGRN_DK
    cols = slice(h * dk, (h + 1) * dk)
    lbv = lb_ref[:, cols]
    qraw = q_ref[:, cols]
    u = u_ref[:, cols]
    v = v_ref[:, cols]
    q = _silu(qraw) * dk ** -0.5
    la = jnp.log(lbv)
    lc = jnp.log1p(-lbv) + _log_sigmoid(u)
    mx = jnp.maximum(la, lc)
    mn = jnp.minimum(la, lc)
    logf = mx + jnp.log1p(jnp.exp(mn - mx))
    k = (1.0 - lbv) * _sigmoid(-u)

    bl = _dot_exact(keep, logf * LOG2E)
    b_end = bl[0:1] if rev else bl[tb - 1:tb]
    half = tb // 2
    halves = (slice(0, half), slice(half, tb))
    scores = [jnp.zeros((half, half), F32), jnp.zeros((half, half), F32)]
    for lv in range(half.bit_length() - 1):
        dq = bl - _level_ref(bl, 1 << lv, rev)
        qt = (q * jnp.exp2(dq)).astype(BF16)
        kt = (k * jnp.exp2(-dq)).astype(BF16)
        for hb, rows in enumerate(halves):
            scores[hb] = jnp.where(level == lv, _dot_nt(qt[rows], kt[rows]), scores[hb])
    isl, jsl = (halves[0], halves[1]) if rev else (halves[1], halves[0])
    dq = bl - _level_ref(bl, half, rev)
    cross = _dot_nt((q[isl] * jnp.exp2(dq[isl])).astype(BF16), (k[jsl] * jnp.exp2(-dq[jsl])).astype(BF16))
    vb = v.astype(BF16)
    st = st_ref[d, h]
    rest = (jnp.sum(q * k, axis=-1, keepdims=True) * v
            + _dot_nt((q * jnp.exp2(bl)).astype(BF16), st.astype(BF16)))
    for hb, rows in enumerate(halves):
        o = _dot(scores[hb].astype(BF16), vb[rows]) + rest[rows]
        if rows == isl:
            o = o + _dot(cross.astype(BF16), vb[jsl])
        o_ref[rows, cols] = o
    st_ref[d, h] = st * jnp.exp2(b_end) + _dot(v.T.astype(BF16), (k * jnp.exp2(b_end - bl)).astype(BF16))


def _hgrn_kernel(qf, uf, vf, qb, ub, vb, lb_ref, of, ob, st_ref, *, tb, hps):
    @pl.when(pl.program_id(2) == 0)
    def _():
        st_ref[...] = jnp.zeros_like(st_ref)
    _hgrn_dir(qf, uf, vf, lb_ref, of, st_ref, 0, False, tb, hps)
    _hgrn_dir(qb, ub, vb, lb_ref, ob, st_ref, 1, True, tb, hps)


def _hgrn_scan(p_main, lb_row, *, nb, ncb, nlb, tb, heads, off_q, off_ff, off_fb, off_i):
    t = p_main.shape[0]
    dk = HGRN_DK

    hps = HGRN_HEADS_PER_STEP
    wd = hps * dk

    def spec(off, rev):
        return pl.BlockSpec((tb, wd), lambda b, h, s: (_rowblk(b, s, rev, ncb, nlb, nb), off // wd + h))

    def ospec(rev):
        return pl.BlockSpec((tb, wd), lambda b, h, s: (_rowblk(b, s, rev, ncb, nlb, nb), h))

    out = jax.ShapeDtypeStruct((t, heads * dk), F32)
    return pl.pallas_call(
        functools.partial(_hgrn_kernel, tb=tb, hps=hps),
        grid=(nb, heads // hps, ncb + nlb),
        in_specs=[spec(off_q, False), spec(off_ff, False), spec(off_i, False),
                  spec(off_q, True), spec(off_fb, True), spec(off_i, True),
                  pl.BlockSpec((1, wd), lambda b, h, s: (0, h))],
        out_specs=[ospec(False), ospec(True)],
        out_shape=[out, out],
        scratch_shapes=[pltpu.VMEM((2, hps, dk, dk), F32)],
        compiler_params=_cparams(("parallel", "parallel", "arbitrary")), name="hgrn_scan",
    )(p_main, p_main, p_main, p_main, p_main, p_main, lb_row)


def _mlstm_dir(q_ref, k_ref, v_ref, g_ref, bias_ref, o_ref, c_ref, n_ref, m_ref, d, rev, tb, dk, dv):
    g = g_ref[...] + bias_ref[...]
    lf = _log_sigmoid(g) * LOG2E
    g = g * LOG2E
    keep = _tri(tb, rev)
    b = _dot_exact(jnp.where(keep, 1.0, 0.0).astype(BF16), lf)
    bt, gt = b.T, g.T
    for h in range(MLSTM_HEADS):
        ci = 2 * MLSTM_HEADS * d + h
        cf = ci + MLSTM_HEADS
        q = q_ref[:, h * dk:(h + 1) * dk]
        ks = k_ref[:, h * dk:(h + 1) * dk] * dk ** -0.5
        v = v_ref[:, h * dv:(h + 1) * dv]
        b_col, ic_col = b[:, cf:cf + 1], g[:, ci:ci + 1]
        b_row, ic_row = bt[cf:cf + 1, :], gt[ci:ci + 1, :]
        b_end = b_col[0:1] if rev else b_col[tb - 1:tb]
        m_prev = m_ref[d, h][:, 0:1]
        w_in = jnp.where(keep, b_col - b_row + ic_row, -jnp.inf)
        w_st = b_col + m_prev
        m_row = jnp.maximum(jnp.max(w_in, axis=1, keepdims=True), w_st)
        qb = q.astype(BF16)
        vb = v.astype(BF16)
        p = jnp.exp2(w_in - m_row) * _dot_nt(qb, ks.astype(BF16))
        e_st = jnp.exp2(w_st - m_row)
        num = _dot(p.astype(BF16), vb) + e_st * _dot(qb, c_ref[d, h].astype(BF16))
        nrm = (jnp.sum(p, axis=1, keepdims=True)
               + e_st * jnp.sum(q * n_ref[d, h], axis=1, keepdims=True))
        o_ref[:, h * dv:(h + 1) * dv] = num / jnp.maximum(jnp.abs(nrm), jnp.exp2(-m_row))
        w_end = b_end - b_col + ic_col
        m_new = jnp.maximum(b_end + m_prev, jnp.max(w_end, axis=0, keepdims=True))
        s_old = jnp.exp2(b_end + m_prev - m_new)
        kt = ks * jnp.exp2(w_end - m_new)
        c_ref[d, h] = s_old * c_ref[d, h] + _dot(kt.T.astype(BF16), vb)
        n_ref[d, h] = s_old * n_ref[d, h] + jnp.sum(kt, axis=0, keepdims=True)
        m_ref[d, h] = jnp.broadcast_to(m_new, m_ref.shape[2:])


def _mlstm_kernel(qf, kf, vf, gf, qb, kb, vb, gb, bias_ref, of, ob, c_ref, n_ref, m_ref, *, tb, dk, dv):
    @pl.when(pl.program_id(1) == 0)
    def _():
        c_ref[...] = jnp.zeros_like(c_ref)
        n_ref[...] = jnp.zeros_like(n_ref)
        m_ref[...] = jnp.zeros_like(m_ref)
    _mlstm_dir(qf, kf, vf, gf, bias_ref, of, c_ref, n_ref, m_ref, 0, False, tb, dk, dv)
    _mlstm_dir(qb, kb, vb, gb, bias_ref, ob, c_ref, n_ref, m_ref, 1, True, tb, dk, dv)


def _mlstm_scan(conv, p_main, p_tail, bias_row, *, nb, ncb, nlb, tb, dk, dv,
                off_q, off_k, off_v, off_small):
    t = conv.shape[0]

    nh = MLSTM_HEADS

    def spec(width, off, rev):
        return pl.BlockSpec((tb, width), lambda b, s: (_rowblk(b, s, rev, ncb, nlb, nb), off // width))

    def ospec(rev):
        return pl.BlockSpec((tb, nh * dv), lambda b, s: (_rowblk(b, s, rev, ncb, nlb, nb), 0))

    out = jax.ShapeDtypeStruct((t, nh * dv), F32)
    return pl.pallas_call(
        functools.partial(_mlstm_kernel, tb=tb, dk=dk, dv=dv),
        grid=(nb, ncb + nlb),
        in_specs=[spec(nh * dk, off_q, False), spec(nh * dk, off_k, False), spec(nh * dv, off_v, False),
                  spec(LANES, off_small, False),
                  spec(nh * dk, off_q, True), spec(nh * dk, off_k, True), spec(nh * dv, off_v, True),
                  spec(LANES, off_small, True),
                  pl.BlockSpec((1, LANES), lambda b, s: (0, 0))],
        out_specs=[ospec(False), ospec(True)],
        out_shape=[out, out],
        scratch_shapes=[pltpu.VMEM((2, nh, dk, dv), F32), pltpu.VMEM((2, nh, 1, dk), F32),
                        pltpu.VMEM((2, nh, 1, LANES), F32)],
        compiler_params=_cparams(("parallel", "arbitrary")), name="mlstm_scan",
    )(conv, conv, p_main, p_tail, conv, conv, p_main, p_tail, bias_row)


def _ssd_dir(c_ref, bm_ref, x_ref, g_ref, dtb_ref, alog_ref, o_ref, s_ref, d, rev, tb, heads, hpg):
    dt = _softplus(g_ref[...] + dtb_ref[...])
    la = -jnp.exp(alog_ref[...]) * dt
    keep = _tri(tb, rev)
    b = _dot_exact(jnp.where(keep, 1.0, 0.0).astype(BF16), la * LOG2E)
    bt = b.T
    width = hpg * SSD_HEADDIM
    head0 = MLSTM_HEADS * 4 + heads * d
    er = lax.broadcasted_iota(jnp.int32, (LANES, heads * SSD_HEADDIM), 0)
    ec = lax.broadcasted_iota(jnp.int32, (LANES, heads * SSD_HEADDIM), 1)
    spread = jnp.where(er == head0 + ec // SSD_HEADDIM, 1.0, 0.0).astype(BF16)
    b_wide = _dot_exact_rhs(b, spread)
    dt_wide = _dot_exact_rhs(dt, spread)
    for grp in range(SSD_GROUPS):
        lane0 = head0 + hpg * grp
        cm = c_ref[:, grp * SSD_STATE:(grp + 1) * SSD_STATE].astype(BF16)
        bm = bm_ref[:, grp * SSD_STATE:(grp + 1) * SSD_STATE]
        x = x_ref[:, grp * width:(grp + 1) * width]
        b_all = b_wide[:, grp * width:(grp + 1) * width]
        dt_all = dt_wide[:, grp * width:(grp + 1) * width]
        b_end = b_all[0:1] if rev else b_all[tb - 1:tb]
        v_all = x * dt_all
        v_bf = v_all.astype(BF16)
        gmat = _dot_nt(cm, bm.astype(BF16))
        s_old = s_ref[d, grp]
        inter = jnp.exp2(b_all) * _dot(cm, s_old.astype(BF16))
        outs = []
        for i in range(hpg):
            c = lane0 + i
            dec = jnp.where(keep, jnp.exp2(b[:, c:c + 1] - bt[c:c + 1, :]), 0.0)
            outs.append(_dot((gmat * dec).astype(BF16), v_bf[:, i * SSD_HEADDIM:(i + 1) * SSD_HEADDIM]))
        o_ref[:, grp * width:(grp + 1) * width] = jnp.concatenate(outs, axis=1) + inter
        sv = (v_all * jnp.exp2(b_end - b_all)).astype(BF16)
        s_ref[d, grp] = jnp.exp2(b_end) * s_old + _dot(bm.T.astype(BF16), sv)


def _ssd_kernel(cf, bf, xf, gf, cb, bb, xb, gb, dtb_ref, alog_ref, of, ob, s_ref, *, tb, heads, hpg):
    @pl.when(pl.program_id(1) == 0)
    def _():
        s_ref[...] = jnp.zeros_like(s_ref)
    _ssd_dir(cf, bf, xf, gf, dtb_ref, alog_ref, of, s_ref, 0, False, tb, heads, hpg)
    _ssd_dir(cb, bb, xb, gb, dtb_ref, alog_ref, ob, s_ref, 1, True, tb, heads, hpg)


def _ssd_scan(conv, p_tail, dtb_row, alog_row, *, nb, ncb, nlb, tb, heads, off_x, off_b, off_c, off_small):
    t = conv.shape[0]
    hpg = heads // SSD_GROUPS
    width = hpg * SSD_HEADDIM

    ng = SSD_GROUPS

    def spec(w, off, rev):
        return pl.BlockSpec((tb, w), lambda b, s: (_rowblk(b, s, rev, ncb, nlb, nb), off // w))

    def ospec(rev):
        return pl.BlockSpec((tb, ng * width), lambda b, s: (_rowblk(b, s, rev, ncb, nlb, nb), 0))

    row = pl.BlockSpec((1, LANES), lambda b, s: (0, 0))
    out = jax.ShapeDtypeStruct((t, heads * SSD_HEADDIM), F32)
    return pl.pallas_call(
        functools.partial(_ssd_kernel, tb=tb, heads=heads, hpg=hpg),
        grid=(nb, ncb + nlb),
        in_specs=[spec(ng * SSD_STATE, off_c, False), spec(ng * SSD_STATE, off_b, False),
                  spec(ng * width, off_x, False), spec(LANES, off_small, False),
                  spec(ng * SSD_STATE, off_c, True), spec(ng * SSD_STATE, off_b, True),
                  spec(ng * width, off_x, True), spec(LANES, off_small, True), row, row],
        out_specs=[ospec(False), ospec(True)],
        out_shape=[out, out],
        scratch_shapes=[pltpu.VMEM((2, ng, SSD_STATE, width), F32)],
        compiler_params=_cparams(("parallel", "arbitrary")), name="ssd_scan",
    )(conv, conv, conv, p_tail, conv, conv, conv, p_tail, dtb_row, alog_row)


def _headnorm_kernel(of_ref, ob_ref, g_ref, w_ref, out_ref, *, hd):
    cb = of_ref.shape[1]
    for i in range(cb // hd):
        sl = slice(i * hd, (i + 1) * hd)
        o = of_ref[:, sl] + ob_ref[:, sl]
        y = o * lax.rsqrt(jnp.mean(o * o, axis=-1, keepdims=True) + EPS) * w_ref[:, sl]
        out_ref[:, sl] = (y * _silu(g_ref[:, sl])).astype(BF16)


def _headnorm(of, ob, gsrc, goff, w, *, hd, tr, cb, name):
    t, wd = of.shape
    tr, cb = _tile(t, tr), _tile(wd, cb)
    gblk = goff // cb
    return pl.pallas_call(
        functools.partial(_headnorm_kernel, hd=hd), grid=(t // tr, wd // cb),
        in_specs=[pl.BlockSpec((tr, cb), lambda i, j: (i, j)),
                  pl.BlockSpec((tr, cb), lambda i, j: (i, j)),
                  pl.BlockSpec((tr, cb), lambda i, j: (i, gblk + j)),
                  pl.BlockSpec((1, cb), lambda i, j: (0, j))],
        out_specs=pl.BlockSpec((tr, cb), lambda i, j: (i, j)),
        out_shape=jax.ShapeDtypeStruct((t, wd), BF16),
        compiler_params=_cparams(("parallel", "parallel")), name=name,
    )(of, ob, gsrc, w)


def _ssd_finish_kernel(of_ref, ob_ref, x_ref, z_ref, d_ref, w_ref, out_ref):
    y = of_ref[...] + ob_ref[...] + d_ref[...] * x_ref[...]
    tt = y * _silu(z_ref[...])
    out_ref[...] = (tt * lax.rsqrt(jnp.mean(tt * tt, axis=-1, keepdims=True) + EPS)
                    * w_ref[...]).astype(BF16)


def _ssd_finish(of, ob, conv, off_x, p_tail, off_z, d_row, w, *, tr):
    t, wd = of.shape
    tr = _tile(t, tr)
    row = pl.BlockSpec((1, wd), lambda i: (0, 0))
    return pl.pallas_call(
        _ssd_finish_kernel, grid=(t // tr,),
        in_specs=[pl.BlockSpec((tr, wd), lambda i: (i, 0)),
                  pl.BlockSpec((tr, wd), lambda i: (i, 0)),
                  pl.BlockSpec((tr, wd), lambda i: (i, off_x // wd)),
                  pl.BlockSpec((tr, wd), lambda i: (i, off_z // wd)), row, row],
        out_specs=pl.BlockSpec((tr, wd), lambda i: (i, 0)),
        out_shape=jax.ShapeDtypeStruct((t, wd), BF16),
        compiler_params=_cparams(("parallel",)), name="ssd_finish",
    )(of, ob, conv, p_tail, d_row, w)


def _merge_kernel(y0, y1, y2, bw_ref, mg_ref, gw_ref, gb_ref, out_ref, acc_ref):
    k = pl.program_id(2)
    gate = _sigmoid(_dot(mg_ref[...].astype(BF16), gw_ref[...]) + gb_ref[...])
    for idx, y_ref in enumerate((y0, y1, y2)):
        @pl.when(k == idx)
        def _(y_ref=y_ref, idx=idx):
            r = gate * _dot(y_ref[...], bw_ref[...])
            if idx == 0:
                acc_ref[...] = r
            elif idx == 1:
                acc_ref[...] += r
            else:
                out_ref[...] = (acc_ref[...] + r).astype(BF16)


def _merge(ys, bw, p_tail, off_merge, rank, gw, gb, *, layer, tm, tn, row0):
    t, bwid = ys[0].shape
    d = bw.shape[-1]
    tm, tn = _tile(math.gcd(t, row0), tm), _tile(d, tn)
    nj = d // tn
    o = row0 // tm
    yspec = pl.BlockSpec((tm, bwid), lambda i, j, k: (i + o, 0))
    return pl.pallas_call(
        _merge_kernel, grid=((t - row0) // tm, nj, 3),
        in_specs=[yspec, yspec, yspec,
                  pl.BlockSpec((None, None, bwid, tn), lambda i, j, k: (layer, k, 0, j)),
                  pl.BlockSpec((tm, rank), lambda i, j, k: (i + o, off_merge // rank)),
                  pl.BlockSpec((None, rank, tn), lambda i, j, k: (layer, 0, k * nj + j)),
                  pl.BlockSpec((1, tn), lambda i, j, k: (0, k * nj + j))],
        out_specs=pl.BlockSpec((tm, tn), lambda i, j, k: (i + o, j)),
        out_shape=jax.ShapeDtypeStruct((t, d), BF16),
        scratch_shapes=[pltpu.VMEM((tm, tn), F32)],
        compiler_params=_cparams(("parallel", "parallel", "arbitrary")), name="branch_merge",
    )(ys[0], ys[1], ys[2], bw, p_tail, gw, gb)


def _swiglu_kernel(a_ref, w1_ref, w3_ref, out_ref, *accs, nk):
    def part():
        a = a_ref[...]
        return _dot(a, w1_ref[...]), _dot(a, w3_ref[...])

    def finish(r1, r3):
        out_ref[...] = (_silu(r1) * r3).astype(BF16)

    _accumulate(pl.program_id(2), nk, part, accs, finish)


def _swiglu_up(a, w1, w3, *, tm, tn, tk, row0, name):
    t, kdim = a.shape
    n = w1.shape[1]
    tm, tn, tk = _tile(math.gcd(t, row0), tm), _tile(n, tn), _tile(kdim, tk)
    nk = kdim // tk
    o = row0 // tm
    wspec = pl.BlockSpec((tk, tn), lambda i, j, k: (k, j))
    return pl.pallas_call(
        functools.partial(_swiglu_kernel, nk=nk),
        grid=((t - row0) // tm, n // tn, nk),
        in_specs=[pl.BlockSpec((tm, tk), lambda i, j, k: (i + o, k)), wspec, wspec],
        out_specs=pl.BlockSpec((tm, tn), lambda i, j, k: (i + o, j)),
        out_shape=jax.ShapeDtypeStruct((t, n), BF16),
        scratch_shapes=[pltpu.VMEM((tm, tn), F32)] * (2 if nk > 1 else 0),
        compiler_params=_cparams(("parallel", "parallel", "arbitrary")), name=name,
    )(a, w1, w3)


MOE_RB = LANES
MOE_MB = 512


def _moe_tables(cnt_tile, tm):
    nt, ne = cnt_tile.shape
    rb, bpm = MOE_RB, MOE_MB // MOE_RB
    ni = 2 * tm // rb + ne
    nblk = -(-(nt * ni + ne * (bpm - 1)) // bpm) * bpm
    nbk = (cnt_tile + (rb - 1)) // rb
    reg_e = (jnp.sum(nbk, axis=0) + (bpm - 1)) // bpm * bpm
    end_e = jnp.cumsum(reg_e)
    start_e = end_e - reg_e
    pre_ie = jnp.cumsum(nbk, axis=0) - nbk
    ends_ie = jnp.cumsum(nbk, axis=1)
    off_ie = ends_ie - nbk
    n_items = ends_ie[:, -1]
    it = jnp.arange(ni, dtype=jnp.int32)[None, :]
    itc = jnp.minimum(it, n_items[:, None] - 1)
    e_idx = jnp.sum((itc[:, :, None] >= ends_ie[:, None, :]).astype(jnp.int32), axis=-1)
    e_idx = jnp.minimum(e_idx, ne - 1)
    chunk = itc - jnp.take_along_axis(off_ie, e_idx, axis=1)
    gblk = start_e[e_idx] + jnp.take_along_axis(pre_ie, e_idx, axis=1) + chunk
    iout = jnp.where(it < n_items[:, None], gblk, nblk)
    m = jnp.arange(nblk // bpm, dtype=jnp.int32)
    mexp = jnp.sum((m[:, None] * bpm >= end_e[None, :]).astype(jnp.int32), axis=-1)
    off_row = jnp.zeros((nt, 1, LANES), F32).at[:, 0, :ne].set((off_ie * rb).astype(F32))
    return dict(ni=ni, nblk=nblk, iout=iout.reshape(-1).astype(jnp.int32),
                iblk=gblk.reshape(-1).astype(jnp.int32),
                mexp=jnp.minimum(mexp, ne - 1).astype(jnp.int32),
                nvalid=(end_e[-1:] // bpm).astype(jnp.int32), off_row=off_row)


def _moe_gather_kernel(iout_ref, h_ref, comb_ref, off_ref, za_ref, zw_ref,
                       a_ref, w_ref, l0_ref, l1_ref, lt_ref, wt_ref, *, tm, rb):
    del iout_ref, za_ref, zw_ref
    it = pl.program_id(1)

    @pl.when(it == 0)
    def _():
        comb = comb_ref[...]
        pick = comb > 0.0
        r = lax.broadcasted_iota(jnp.int32, (tm, tm), 0)
        c = lax.broadcasted_iota(jnp.int32, (tm, tm), 1)
        rank = _dot(jnp.where(c < r, 1.0, 0.0).astype(BF16), jnp.where(pick, 1.0, 0.0).astype(BF16))
        loc = off_ref[...] + rank
        lane = lax.broadcasted_iota(jnp.int32, comb.shape, 1).astype(F32)
        m1 = jnp.min(jnp.where(pick, lane, float(LANES)), axis=1, keepdims=True)
        first = pick & (lane == m1)
        second = pick & (lane != m1)
        l0 = jnp.sum(jnp.where(first, loc, 0.0), axis=1, keepdims=True)
        l1 = jnp.sum(jnp.where(second, loc + 1.0, 0.0), axis=1, keepdims=True) - 1.0
        l0b = jnp.broadcast_to(l0, comb.shape)
        l1b = jnp.broadcast_to(l1, comb.shape)
        l0_ref[...] = l0b
        l1_ref[...] = l1b
        lt_ref[0] = l0b.T
        lt_ref[1] = l1b.T
        wt_ref[0] = jnp.broadcast_to(jnp.sum(jnp.where(first, comb, 0.0), axis=1, keepdims=True), comb.shape)
        wt_ref[1] = jnp.broadcast_to(jnp.sum(jnp.where(second, comb, 0.0), axis=1, keepdims=True), comb.shape)

    s = (it * rb + lax.broadcasted_iota(jnp.int32, (rb, 1), 0)).astype(F32)
    p0 = jnp.where(lt_ref[0, 0:1, :] == s, 1.0, 0.0).astype(BF16)
    p1 = jnp.where(lt_ref[1, 0:1, :] == s, 1.0, 0.0).astype(BF16)
    a_ref[...] = _dot(p0 + p1, h_ref[...]).astype(BF16)
    w_ref[...] = _dot_exact(p0, wt_ref[0]) + _dot_exact(p1, wt_ref[1])


def _moe_gather(h, comb, tab, *, tm, row0):
    d = h.shape[1]
    t = h.shape[0] - row0
    o = row0 // tm
    rb, ni = MOE_RB, tab['ni']
    nrows = tab['nblk'] * rb + MOE_MB
    grid_spec = pltpu.PrefetchScalarGridSpec(
        num_scalar_prefetch=1, grid=(t // tm, ni),
        in_specs=[pl.BlockSpec((tm, d), lambda i, it, io: (i + o, 0)),
                  pl.BlockSpec((tm, LANES), lambda i, it, io: (i + o, 0)),
                  pl.BlockSpec((None, 1, LANES), lambda i, it, io: (i, 0, 0)),
                  pl.BlockSpec(memory_space=pl.ANY), pl.BlockSpec(memory_space=pl.ANY)],
        out_specs=[pl.BlockSpec((rb, d), lambda i, it, io: (io[i * ni + it], 0)),
                   pl.BlockSpec((rb, LANES), lambda i, it, io: (io[i * ni + it], 0)),
                   pl.BlockSpec((tm, LANES), lambda i, it, io: (i, 0)),
                   pl.BlockSpec((tm, LANES), lambda i, it, io: (i, 0))],
        scratch_shapes=[pltpu.VMEM((2, LANES, tm), F32), pltpu.VMEM((2, tm, LANES), F32)])
    return pl.pallas_call(
        functools.partial(_moe_gather_kernel, tm=tm, rb=rb), grid_spec=grid_spec,
        out_shape=[jax.ShapeDtypeStruct((nrows, d), BF16), jax.ShapeDtypeStruct((nrows, LANES), F32),
                   jax.ShapeDtypeStruct((t, LANES), F32), jax.ShapeDtypeStruct((t, LANES), F32)],
        input_output_aliases={4: 0, 5: 1},
        compiler_params=_cparams(("parallel", "arbitrary")), name="moe_gather",
    )(tab['iout'], h, comb, tab['off_row'], jnp.zeros((nrows, d), BF16), jnp.zeros((nrows, LANES), F32))


def _moe_grouped_kernel(*refs, nk, dual):
    if dual:
        mexp_ref, nv_ref, a_ref, w1_ref, w3_ref, ws_ref, out_ref, *accs = refs
    else:
        mexp_ref, nv_ref, a_ref, w1_ref, out_ref, *accs = refs
    del mexp_ref
    k = pl.program_id(2)

    def part():
        a = a_ref[...]
        if dual:
            return _dot(a, w1_ref[...]), _dot(a, w3_ref[...])
        return (_dot(a, w1_ref[...]),)

    def finish(r1, r3=None):
        if dual:
            out_ref[...] = (_silu(r1) * r3 * ws_ref[:, 0:1]).astype(BF16)
        else:
            out_ref[...] = r1.astype(BF16)

    @pl.when(pl.program_id(0) < nv_ref[0])
    def _():
        _accumulate(k, nk, part, accs, finish)


def _moe_grouped(a, ws, w_sorted, tab, *, tn, tk, name):
    rows, kdim = a.shape
    n = ws[0].shape[-1]
    tn, tk = _tile(n, tn), _tile(kdim, tk)
    nj, nk = n // tn, kdim // tk
    nmb = tab['nblk'] * MOE_RB // MOE_MB
    dual = len(ws) == 2

    def live(m, nv):
        return m < nv[0]

    def me(m, nv):
        return jnp.minimum(m, nv[0] - 1)

    a_spec = pl.BlockSpec((MOE_MB, tk), lambda m, j, k, ex, nv: (me(m, nv), jnp.where(live(m, nv), k, nk - 1)))
    w_spec = pl.BlockSpec((None, tk, tn), lambda m, j, k, ex, nv: (
        ex[me(m, nv)], jnp.where(live(m, nv), k, nk - 1), jnp.where(live(m, nv), j, nj - 1)))
    o_spec = pl.BlockSpec((MOE_MB, tn), lambda m, j, k, ex, nv: (me(m, nv), jnp.where(live(m, nv), j, nj - 1)))
    in_specs = [a_spec, w_spec]
    args = [a, ws[0]]
    if dual:
        in_specs += [w_spec, pl.BlockSpec((MOE_MB, LANES), lambda m, j, k, ex, nv: (me(m, nv), 0))]
        args += [ws[1], w_sorted]
    scratch = [pltpu.VMEM((MOE_MB, tn), F32)] * (len(ws) if nk > 1 else 0)
    grid_spec = pltpu.PrefetchScalarGridSpec(
        num_scalar_prefetch=2, grid=(nmb, nj, nk), in_specs=in_specs, out_specs=o_spec,
        scratch_shapes=scratch)
    return pl.pallas_call(
        functools.partial(_moe_grouped_kernel, nk=nk, dual=dual), grid_spec=grid_spec,
        out_shape=jax.ShapeDtypeStruct((rows, n), BF16),
        compiler_params=_cparams(("arbitrary", "arbitrary", "arbitrary")), name=name,
    )(tab['mexp'], tab['nvalid'], *args)


def _moe_scatter_kernel(iblk_ref, l0_ref, l1_ref, *rest, rb, ni, seg, tile0):
    del iblk_ref
    y_refs = rest[:ni]
    x_ref, mod_ref, out_ref, pt_ref, ya_ref = rest[ni:]
    row = _mod_row(pl.program_id(0) + tile0, *seg)

    @pl.when(pl.program_id(1) == 0)
    def _():
        l0, l1 = l0_ref[...], l1_ref[...]
        lane = lax.broadcasted_iota(jnp.int32, (1, rb), 1).astype(F32)
        for q in range(ni):
            s = lane + float(q * rb)
            pt_ref[:, q * rb:(q + 1) * rb] = jnp.where((l0 == s) | (l1 == s), 1.0, 0.0).astype(BF16)

    for q in range(ni):
        ya_ref[q * rb:(q + 1) * rb, :] = y_refs[q][...]
    out_ref[...] = x_ref[...] + mod_ref[pl.ds(row, 1), :] * _dot(pt_ref[...], ya_ref[...])


def _moe_scatter(l0, l1, ys, x, mod, gate_off, tab, *, tm, tn, seg_fn, row0):
    t, d = x.shape
    rb, ni = MOE_RB, tab['ni']
    tn = _tile(d, tn)
    gblk = gate_off // tn
    o = row0 // tm
    lspec = pl.BlockSpec((tm, LANES), lambda i, j, ib: (i, 0))
    yspecs = [pl.BlockSpec((rb, tn), lambda i, j, ib, q=q: (ib[i * ni + q], j)) for q in range(ni)]
    grid_spec = pltpu.PrefetchScalarGridSpec(
        num_scalar_prefetch=1, grid=((t - row0) // tm, d // tn),
        in_specs=[lspec, lspec, *yspecs,
                  pl.BlockSpec((tm, tn), lambda i, j, ib: (i + o, j)),
                  pl.BlockSpec((MOD_ROWS, tn), lambda i, j, ib: (0, gblk + j))],
        out_specs=pl.BlockSpec((tm, tn), lambda i, j, ib: (i + o, j)),
        scratch_shapes=[pltpu.VMEM((tm, ni * rb), BF16), pltpu.VMEM((ni * rb, tn), BF16)])
    return pl.pallas_call(
        functools.partial(_moe_scatter_kernel, rb=rb, ni=ni, seg=seg_fn(tm), tile0=o), grid_spec=grid_spec,
        out_shape=jax.ShapeDtypeStruct((t, d), F32),
        compiler_params=_cparams(("parallel", "arbitrary")), name="moe_scatter",
    )(tab['iblk'], l0, l1, *([ys] * ni), x, mod)


def kernel(x, c, ctx, c_ctx, mod_w, mod_b, norm1_w, norm2_w, in_w, conv_w, conv_b, hgrn_lb, hgrn_norm_w,
           mlstm_igate_b, mlstm_fgate_b, mlstm_norm_w, ssd_a_log, ssd_dt_bias, ssd_d, ssd_norm_w,
           gate_w, gate_b, branch_w, out_w, ffn_w1, ffn_w3, ffn_w2, router_w, moe_w1, moe_w3, moe_w2,
           final_norm_w):
    nb, seq, d = x.shape
    ctx_len = ctx.shape[1]
    depth = mod_w.shape[0]
    bw = d // 2
    hg_heads = bw // HGRN_DK
    ml_dv = bw // MLSTM_HEADS
    ml_dk = ml_dv // 2
    ss_heads = bw // SSD_HEADDIM
    rank = gate_w.shape[1]
    n_ml_qk = MLSTM_HEADS * ml_dk
    n_ss_bc = SSD_GROUPS * SSD_STATE
    off = {}
    pos = 0
    for nm, sz in (('ml_q', n_ml_qk), ('ml_k', n_ml_qk), ('ss_x', bw), ('ss_B', n_ss_bc), ('ss_C', n_ss_bc),
                   ('hg_q', bw), ('hg_f_fwd', bw), ('hg_f_bwd', bw), ('hg_i', bw), ('hg_g', bw),
                   ('ml_v', bw), ('ml_z', bw), ('ml_gates', 4 * MLSTM_HEADS), ('ss_z', bw),
                   ('ss_dt', 2 * ss_heads), ('merge', rank)):
        off[nm] = (pos, sz)
        pos += sz
    n_conv = off['hg_q'][0]
    n_main = off['ml_gates'][0]
    n_small = 4 * MLSTM_HEADS + 2 * ss_heads
    assert n_small <= LANES and ctx_len % GRID_W == 0 and seq % ctx_len == 0
    t_off_z, t_off_merge, t_off_small = 0, bw, bw + rank
    n_tail_raw = bw + rank + LANES
    n_tail = -(-n_tail_raw // 512) * 512

    tb = ctx_len
    ncb, nlb = 1, seq // tb
    n_ctx_rows = nb * ctx_len
    t = n_ctx_rows + nb * seq
    ctx_row = nb

    def seg_fn(tile):
        assert n_ctx_rows % tile == 0 and seq % tile == 0
        return (n_ctx_rows // tile, seq // tile, ctx_row)

    tm_big = _tile(math.gcd(n_ctx_rows, seq), 1024)

    c_all = jnp.zeros((MOD_ROWS, d), F32).at[:nb].set(c).at[ctx_row].set(c_ctx)

    in_w_bf = in_w.astype(BF16)
    branch_bf, gate_bf, out_bf = branch_w.astype(BF16), gate_w.astype(BF16), out_w.astype(BF16)
    w_tail = jnp.concatenate(
        [lax.slice_in_dim(in_w, off[nm][0], off[nm][0] + off[nm][1], axis=2)
         for nm in ('ss_z', 'merge', 'ml_gates', 'ss_dt')]
        + [jnp.zeros((depth, d, n_tail - n_tail_raw + LANES - n_small), F32)], axis=2).astype(BF16)
    lb_cum = jnp.cumsum(jax.nn.softmax(hgrn_lb.astype(F32), axis=0), axis=0)
    lower_bounds = lb_cum - lb_cum[0]

    for l in range(depth):
        r0 = n_ctx_rows if l == depth - 1 else 0
        mod = _mm(c_all, mod_w, w_lead=l, out_dtype=F32, tm=MOD_ROWS, tn=1024, tk=2048,
                  a_silu=True, bias=mod_b[l][None, :], name="mod")
        if l == 0:
            h, xs = _norm_mod_join(ctx.reshape(n_ctx_rows, d), x.reshape(nb * seq, d), norm1_w[l][None, :],
                                   mod, sh_off=0, sc_off=d, seg_fn=seg_fn, tr=tb)
        else:
            h = _norm_mod(xs, norm1_w[l][None, :], mod, sh_off=0, sc_off=d, seg_fn=seg_fn, tr=tb,
                          name="norm1")
        p_main = _mm(h, in_w_bf, w_lead=l, n=n_main, out_dtype=F32, tm=tm_big, tn=1024, tk=d,
                     name="in_proj_main")
        p_tail = _mm(h, w_tail, w_lead=l, out_dtype=F32, tm=tm_big, tn=n_tail // 2, tk=2048,
                     name="in_proj_tail")
        conv = _conv(p_main, conv_w[l].reshape(9, n_conv), conv_b[l][None, :], n_conv=n_conv, tc=tb,
                     ctx_len=ctx_len, n_ctx_tiles=n_ctx_rows // tb, tiles_per_img=seq // tb, cb=512)

        hgf, hgb = _hgrn_scan(p_main, lower_bounds[l][None, :], nb=nb, ncb=ncb, nlb=nlb, tb=tb,
                              heads=hg_heads, off_q=off['hg_q'][0], off_ff=off['hg_f_fwd'][0],
                              off_fb=off['hg_f_bwd'][0], off_i=off['hg_i'][0])
        ml_bias = jnp.zeros((1, LANES), F32).at[0, :4 * MLSTM_HEADS].set(
            jnp.stack([mlstm_igate_b[l, 0], mlstm_fgate_b[l, 0],
                       mlstm_igate_b[l, 1], mlstm_fgate_b[l, 1]]).reshape(-1))
        mlf, mlb = _mlstm_scan(conv, p_main, p_tail, ml_bias, nb=nb, ncb=ncb, nlb=nlb, tb=tb,
                               dk=ml_dk, dv=ml_dv, off_q=off['ml_q'][0], off_k=off['ml_k'][0],
                               off_v=off['ml_v'][0], off_small=t_off_small)
        lo = 4 * MLSTM_HEADS
        dtb_row = jnp.zeros((1, LANES), F32).at[0, lo:lo + 2 * ss_heads].set(ssd_dt_bias[l].reshape(-1))
        alog_row = jnp.zeros((1, LANES), F32).at[0, lo:lo + 2 * ss_heads].set(ssd_a_log[l].reshape(-1))
        ssf, ssb = _ssd_scan(conv, p_tail, dtb_row, alog_row, nb=nb, ncb=ncb, nlb=nlb, tb=tb,
                             heads=ss_heads, off_x=off['ss_x'][0], off_b=off['ss_B'][0],
                             off_c=off['ss_C'][0], off_small=t_off_small)

        y_hg = _headnorm(hgf, hgb, p_main, off['hg_g'][0], hgrn_norm_w[l][None, :], hd=HGRN_DK,
                         tr=2 * tb, cb=1024, name="hgrn_finish")
        y_ml = _headnorm(mlf, mlb, p_main, off['ml_z'][0], mlstm_norm_w[l][None, :], hd=ml_dv,
                         tr=2 * tb, cb=1024, name="mlstm_finish")
        d_row = jnp.repeat(ssd_d[l], SSD_HEADDIM)[None, :]
        y_ss = _ssd_finish(ssf, ssb, conv, off['ss_x'][0], p_tail, t_off_z, d_row,
                           ssd_norm_w[l][None, :], tr=tb)
        ym = _merge((y_hg, y_ml, y_ss), branch_bf, p_tail, t_off_merge, rank,
                    gate_bf, gate_b[l][None, :], layer=l, tm=tm_big, tn=1024, row0=r0)
        xs = _mm(ym, out_bf, w_lead=l, out_dtype=F32, tm=tm_big, tn=1024, tk=d,
                 resid=(xs, mod, 2 * d, seg_fn), row0=r0, name="out_proj")

        i = l // 2
        if l % 2 == 0:
            h2 = _norm_mod(xs, norm2_w[l][None, :], mod, sh_off=3 * d, sc_off=4 * d, seg_fn=seg_fn,
                           tr=tb, row0=r0, name="norm2")
            mid = _swiglu_up(h2, ffn_w1[i].astype(BF16), ffn_w3[i].astype(BF16),
                             tm=tm_big, tn=1024, tk=2048, row0=r0, name="ffn_up")
            xs = _mm(mid, ffn_w2[i].astype(BF16), out_dtype=F32, tm=tm_big, tn=1024, tk=2048,
                     resid=(xs, mod, 5 * d, seg_fn), row0=r0, name="ffn_down")
        else:
            rw = jnp.zeros((d, LANES), F32).at[:, :N_EXPERTS].set(router_w[i])
            h2, comb, cnt = _norm_mod(xs, norm2_w[l][None, :], mod, sh_off=3 * d, sc_off=4 * d,
                                      seg_fn=seg_fn, tr=tb, router_w=rw, row0=r0, name="norm2_router")
            cnt_tile = cnt.reshape(t // tm_big, tm_big // tb, SUBLANES, LANES)[r0 // tm_big:, :, 0, :N_EXPERTS]
            tab = _moe_tables(jnp.sum(cnt_tile, axis=1).astype(jnp.int32), tm_big)
            a_s, w_s, l0, l1 = _moe_gather(h2, comb, tab, tm=tm_big, row0=r0)
            mid = _moe_grouped(a_s, (moe_w1[i].astype(BF16), moe_w3[i].astype(BF16)), w_s, tab,
                               tn=1024, tk=d, name="moe_up")
            ys = _moe_grouped(mid, (moe_w2[i].astype(BF16),), None, tab, tn=1024, tk=mid.shape[1],
                              name="moe_down")
            xs = _moe_scatter(l0, l1, ys, xs, mod, 5 * d, tab, tm=tm_big, tn=1024, seg_fn=seg_fn, row0=r0)

    out = _final_norm(xs, final_norm_w[None, :], row0=n_ctx_rows, rows=nb * seq, tr=tb)
    return out.reshape(nb, seq, d)
```

```python
import functools
import math

import jax
import jax.numpy as jnp
from jax import lax
from jax.experimental import pallas as pl
from jax.experimental.pallas import tpu as pltpu

F32 = jnp.float32
BF16 = jnp.bfloat16

GRID_W = 64
EPS = 1e-6
HGRN_DK = 128
HGRN_HEADS_PER_STEP = 8
MLSTM_HEADS = 8
SSD_HEADDIM = 64
SSD_GROUPS = 8
SSD_STATE = 128
N_EXPERTS = 8
LANES = 128
SUBLANES = 8
LOG2E = 1.4426950408889634
MOD_ROWS = 8
VMEM_LIMIT = 56 * 1024 * 1024


def _cparams(sem):
    return pltpu.CompilerParams(dimension_semantics=sem, vmem_limit_bytes=VMEM_LIMIT)


def _tile(n, pref):
    t = min(n, pref)
    while n % t:
        t //= 2
    return t


def _sigmoid(x):
    return 1.0 / (1.0 + jnp.exp(-x))


def _silu(x):
    return x * _sigmoid(x)


def _log_sigmoid(x):
    return jnp.minimum(x, 0.0) - jnp.log1p(jnp.exp(-jnp.abs(x)))


def _softplus(x):
    return jnp.maximum(x, 0.0) + jnp.log1p(jnp.exp(-jnp.abs(x)))


def _dot(a, b):
    return jnp.dot(a, b, preferred_element_type=F32)


def _dot_nt(a, b):
    return lax.dot_general(a, b, (((1,), (1,)), ((), ())), preferred_element_type=F32)


def _split3(x):
    x1 = x.astype(BF16)
    r = x - x1.astype(F32)
    x2 = r.astype(BF16)
    x3 = (r - x2.astype(F32)).astype(BF16)
    return x1, x2, x3


def _dot_exact(m01, x):
    x1, x2, x3 = _split3(x)
    return _dot(m01, x1) + _dot(m01, x2) + _dot(m01, x3)


def _dot_exact_rhs(x, m01):
    x1, x2, x3 = _split3(x)
    return _dot(x1, m01) + _dot(x2, m01) + _dot(x3, m01)


def _mod_row(tile, n_ctx_tiles, tiles_per_batch, ctx_row):
    return jnp.where(tile < n_ctx_tiles, ctx_row, (tile - n_ctx_tiles) // tiles_per_batch)


def _accumulate(k, nk, part, acc_refs, finish):
    if nk == 1:
        finish(*part())
        return

    @pl.when(k == 0)
    def _():
        for acc, p in zip(acc_refs, part()):
            acc[...] = p

    @pl.when((k > 0) & (k < nk - 1))
    def _():
        for acc, p in zip(acc_refs, part()):
            acc[...] += p

    @pl.when(k == nk - 1)
    def _():
        finish(*[acc[...] + p for acc, p in zip(acc_refs, part())])


def _mm_kernel(*refs, nk, a_silu, has_bias, resid, seg, tile0):
    it = iter(refs)
    a_ref, w_ref = next(it), next(it)
    bias_ref = next(it) if has_bias else None
    x_ref = next(it) if resid else None
    mod_ref = next(it) if resid else None
    out_ref = next(it)
    acc_ref = next(it) if nk > 1 else None
    k = pl.program_id(2)
    row = _mod_row(pl.program_id(0) + tile0, *seg) if resid else None

    def part():
        a = a_ref[...]
        if a_silu:
            a = _silu(a.astype(F32))
        return (_dot(a.astype(BF16), w_ref[...].astype(BF16)),)

    def finish(r):
        if has_bias:
            r = r + bias_ref[...]
        if resid:
            r = x_ref[...] + mod_ref[pl.ds(row, 1), :] * r
        out_ref[...] = r.astype(out_ref.dtype)

    _accumulate(k, nk, part, (acc_ref,), finish)


def _mm(a, w, *, out_dtype, tm, tn, tk, w_lead=None, n=None, a_silu=False, bias=None,
        resid=None, row0=0, name):
    m, kdim = a.shape
    n = w.shape[-1] if n is None else n
    tm, tn, tk = _tile(math.gcd(m, row0), tm), _tile(n, tn), _tile(kdim, tk)
    nk = kdim // tk
    o = row0 // tm
    if w_lead is None:
        w_spec = pl.BlockSpec((tk, tn), lambda i, j, k: (k, j))
    else:
        w_spec = pl.BlockSpec((None, tk, tn), lambda i, j, k: (w_lead, k, j))
    in_specs = [pl.BlockSpec((tm, tk), lambda i, j, k: (i + o, k)), w_spec]
    args = [a, w]
    if bias is not None:
        in_specs.append(pl.BlockSpec((1, tn), lambda i, j, k: (0, j)))
        args.append(bias)
    seg = None
    if resid is not None:
        x, mod, gate_off, seg_fn = resid
        seg = seg_fn(tm)
        gblk = gate_off // tn
        in_specs.append(pl.BlockSpec((tm, tn), lambda i, j, k: (i + o, j)))
        in_specs.append(pl.BlockSpec((MOD_ROWS, tn), lambda i, j, k: (0, gblk + j)))
        args += [x, mod]
    return pl.pallas_call(
        functools.partial(_mm_kernel, nk=nk, a_silu=a_silu, has_bias=bias is not None,
                          resid=resid is not None, seg=seg, tile0=o),
        grid=((m - row0) // tm, n // tn, nk),
        in_specs=in_specs,
        out_specs=pl.BlockSpec((tm, tn), lambda i, j, k: (i + o, j)),
        out_shape=jax.ShapeDtypeStruct((m, n), out_dtype),
        scratch_shapes=[pltpu.VMEM((tm, tn), F32)] if nk > 1 else [],
        compiler_params=_cparams(("parallel", "parallel", "arbitrary")),
        name=name,
    )(*args)


def _norm_mod_kernel(*refs, d, sh_off, sc_off, seg, router, tile0):
    if router:
        x_ref, nw_ref, mod_ref, rw_ref, out_ref, comb_ref, cnt_ref = refs
    else:
        x_ref, nw_ref, mod_ref, out_ref = refs
    row = _mod_row(pl.program_id(0) + tile0, *seg)
    x = x_ref[...]
    y = x * lax.rsqrt(jnp.mean(x * x, axis=-1, keepdims=True) + EPS) * nw_ref[...]
    sc = mod_ref[pl.ds(row, 1), sc_off:sc_off + d]
    sh = mod_ref[pl.ds(row, 1), sh_off:sh_off + d]
    h = y * (1.0 + sc) + sh
    out_ref[...] = h.astype(BF16)
    if router:
        h1, h2, h3 = _split3(h)
        r1, r2, r3 = _split3(rw_ref[...])
        logits = (_dot(h1, r1) + _dot(h1, r2) + _dot(h2, r1)
                  + _dot(h2, r2) + _dot(h1, r3) + _dot(h3, r1))
        lane = lax.broadcasted_iota(jnp.int32, logits.shape, 1).astype(F32)
        valid = lane < N_EXPERTS
        logits = jnp.where(valid, logits, -jnp.inf)
        mx = jnp.max(logits, axis=-1, keepdims=True)
        e = jnp.exp(logits - mx)
        probs = e / jnp.sum(e, axis=-1, keepdims=True)
        p1 = jnp.max(probs, axis=-1, keepdims=True)
        i1 = jnp.min(jnp.where((probs == p1) & valid, lane, float(LANES)), axis=-1, keepdims=True)
        rest = jnp.where((lane == i1) | (lane >= N_EXPERTS), -1.0, probs)
        p2 = jnp.max(rest, axis=-1, keepdims=True)
        i2 = jnp.min(jnp.where(rest == p2, lane, float(LANES)), axis=-1, keepdims=True)
        tot = p1 + p2
        comb = jnp.where(lane == i1, p1 / tot, jnp.where(lane == i2, p2 / tot, 0.0))
        comb_ref[...] = comb
        cnt = jnp.sum(jnp.where(comb > 0.0, 1.0, 0.0), axis=0, keepdims=True)
        cnt_ref[...] = jnp.broadcast_to(cnt, cnt_ref.shape)


def _norm_mod(x, nw, mod, *, sh_off, sc_off, seg_fn, tr, router_w=None, row0=0, name):
    t, d = x.shape
    tr = _tile(math.gcd(t, row0), tr)
    o = row0 // tr
    router = router_w is not None
    in_specs = [pl.BlockSpec((tr, d), lambda i: (i + o, 0)),
                pl.BlockSpec((1, d), lambda i: (0, 0)),
                pl.BlockSpec(mod.shape, lambda i: (0, 0))]
    args = [x, nw, mod]
    out_specs = [pl.BlockSpec((tr, d), lambda i: (i + o, 0))]
    out_shape = [jax.ShapeDtypeStruct((t, d), BF16)]
    if router:
        in_specs.append(pl.BlockSpec(router_w.shape, lambda i: (0, 0)))
        args.append(router_w)
        out_specs.append(pl.BlockSpec((tr, LANES), lambda i: (i + o, 0)))
        out_shape.append(jax.ShapeDtypeStruct((t, LANES), F32))
        out_specs.append(pl.BlockSpec((SUBLANES, LANES), lambda i: (i + o, 0)))
        out_shape.append(jax.ShapeDtypeStruct((t // tr * SUBLANES, LANES), F32))
    res = pl.pallas_call(
        functools.partial(_norm_mod_kernel, d=d, sh_off=sh_off, sc_off=sc_off, seg=seg_fn(tr),
                          router=router, tile0=o),
        grid=((t - row0) // tr,), in_specs=in_specs, out_specs=out_specs, out_shape=out_shape,
        compiler_params=_cparams(("parallel",)), name=name,
    )(*args)
    return res if router else res[0]


def _norm_mod_join_kernel(ctx_ref, lat_ref, nw_ref, mod_ref, out_ref, xs_ref, *, d, sh_off, sc_off, seg):
    i = pl.program_id(0)
    row = _mod_row(i, *seg)
    x = jnp.where(i < seg[0], ctx_ref[...], lat_ref[...])
    xs_ref[...] = x
    y = x * lax.rsqrt(jnp.mean(x * x, axis=-1, keepdims=True) + EPS) * nw_ref[...]
    sc = mod_ref[pl.ds(row, 1), sc_off:sc_off + d]
    sh = mod_ref[pl.ds(row, 1), sh_off:sh_off + d]
    out_ref[...] = (y * (1.0 + sc) + sh).astype(BF16)


def _norm_mod_join(ctx2d, lat2d, nw, mod, *, sh_off, sc_off, seg_fn, tr):
    d = ctx2d.shape[1]
    t = ctx2d.shape[0] + lat2d.shape[0]
    seg = seg_fn(tr)
    nct = seg[0]
    return pl.pallas_call(
        functools.partial(_norm_mod_join_kernel, d=d, sh_off=sh_off, sc_off=sc_off, seg=seg),
        grid=(t // tr,),
        in_specs=[pl.BlockSpec((tr, d), lambda i: (jnp.minimum(i, nct - 1), 0)),
                  pl.BlockSpec((tr, d), lambda i: (jnp.maximum(i - nct, 0), 0)),
                  pl.BlockSpec((1, d), lambda i: (0, 0)),
                  pl.BlockSpec(mod.shape, lambda i: (0, 0))],
        out_specs=[pl.BlockSpec((tr, d), lambda i: (i, 0)), pl.BlockSpec((tr, d), lambda i: (i, 0))],
        out_shape=[jax.ShapeDtypeStruct((t, d), BF16), jax.ShapeDtypeStruct((t, d), F32)],
        compiler_params=_cparams(("arbitrary",)), name="norm1_join",
    )(ctx2d, lat2d, nw, mod)


def _final_norm_kernel(x_ref, w_ref, out_ref):
    x = x_ref[...]
    out_ref[...] = x * lax.rsqrt(jnp.mean(x * x, axis=-1, keepdims=True) + EPS) * w_ref[...]


def _final_norm(x, w, *, row0, rows, tr):
    d = x.shape[1]
    tr = _tile(math.gcd(row0, rows), tr)
    off = row0 // tr
    return pl.pallas_call(
        _final_norm_kernel, grid=(rows // tr,),
        in_specs=[pl.BlockSpec((tr, d), lambda i: (i + off, 0)),
                  pl.BlockSpec((1, d), lambda i: (0, 0))],
        out_specs=pl.BlockSpec((tr, d), lambda i: (i, 0)),
        out_shape=jax.ShapeDtypeStruct((rows, d), F32),
        compiler_params=_cparams(("parallel",)), name="final_norm",
    )(x, w)


def _conv_kernel(main_ref, prev_ref, next_ref, w_ref, b_ref, out_ref, z_ref, *,
                 tc, ctx_len, n_ctx_tiles, tiles_per_img):
    i = pl.program_id(0)
    is_ctx = i < n_ctx_tiles
    li = i - n_ctx_tiles
    first = (li % tiles_per_img) == 0
    last = (li % tiles_per_img) == tiles_per_img - 1
    z_ref[0:GRID_W, :] = jnp.where(is_ctx | first, 0.0, prev_ref[...])
    z_ref[GRID_W:GRID_W + tc, :] = main_ref[...]
    z_ref[GRID_W + tc:, :] = jnp.where(is_ctx | last, 0.0, next_ref[...])
    w = w_ref[...]

    def column_sum(dc):
        s = w[3 + dc:4 + dc, :] * z_ref[GRID_W:GRID_W + tc, :]
        for dr in (0, 2):
            wt = jnp.where(is_ctx, 0.0, w[3 * dr + dc:3 * dr + dc + 1, :])
            s = s + wt * z_ref[GRID_W * dr:GRID_W * dr + tc, :]
        return s

    pos = lax.broadcasted_iota(jnp.int32, (tc, 1), 0)
    col = jnp.where(is_ctx, pos % ctx_len, pos % GRID_W)
    width = jnp.where(is_ctx, ctx_len, GRID_W)
    left = jnp.where(col != 0, pltpu.roll(column_sum(0), 1, 0), 0.0)
    right = jnp.where(col != width - 1, pltpu.roll(column_sum(2), tc - 1, 0), 0.0)
    out_ref[...] = _silu(b_ref[...] + column_sum(1) + left + right)


def _conv(p_main, conv_w9, conv_b, *, n_conv, tc, ctx_len, n_ctx_tiles, tiles_per_img, cb):
    t = p_main.shape[0]
    cb = _tile(n_conv, cb)
    rpt = tc // GRID_W
    nrow = t // GRID_W
    return pl.pallas_call(
        functools.partial(_conv_kernel, tc=tc, ctx_len=ctx_len, n_ctx_tiles=n_ctx_tiles,
                          tiles_per_img=tiles_per_img),
        grid=(t // tc, n_conv // cb),
        in_specs=[pl.BlockSpec((tc, cb), lambda i, j: (i, j)),
                  pl.BlockSpec((GRID_W, cb), lambda i, j: (jnp.maximum(i * rpt - 1, 0), j)),
                  pl.BlockSpec((GRID_W, cb), lambda i, j: (jnp.minimum((i + 1) * rpt, nrow - 1), j)),
                  pl.BlockSpec((9, cb), lambda i, j: (0, j)),
                  pl.BlockSpec((1, cb), lambda i, j: (0, j))],
        out_specs=pl.BlockSpec((tc, cb), lambda i, j: (i, j)),
        out_shape=jax.ShapeDtypeStruct((t, n_conv), F32),
        scratch_shapes=[pltpu.VMEM((tc + 2 * GRID_W, cb), F32)],
        compiler_params=_cparams(("parallel", "parallel")), name="conv_silu",
    )(p_main, p_main, p_main, conv_w9, conv_b)


def _rowblk(b, s, rev, ncb, nlb, nb):
    if rev:
        ctx = b * ncb + (ncb - 1 - s)
        lat = nb * ncb + b * nlb + (nlb - 1 - (s - ncb))
    else:
        ctx = b * ncb + s
        lat = nb * ncb + b * nlb + (s - ncb)
    return jnp.where(s < ncb, ctx, lat)


def _tri(n, rev):
    r = lax.broadcasted_iota(jnp.int32, (n, n), 0)
    c = lax.broadcasted_iota(jnp.int32, (n, n), 1)
    return (c >= r) if rev else (c <= r)


def _level_ref(bl, half, rev):
    tb, dk = bl.shape
    blk = 2 * half
    idx = half if rev else half - 1
    if blk == tb:
        return bl[idx:idx + 1, :]
    if blk >= SUBLANES:
        b3 = bl.reshape(tb // blk, blk, dk)
        return jnp.broadcast_to(b3[:, idx:idx + 1, :], b3.shape).reshape(tb, dk)
    tmod = lax.broadcasted_iota(jnp.int32, (tb, 1), 0) % blk
    r = bl
    for m in range(blk):
        if idx != m:
            r = jnp.where(tmod == m, pltpu.roll(bl, (m - idx) % tb, 0), r)
    return r


def _hgrn_dir(q_ref, u_ref, v_ref, lb_ref, o_ref, st_ref, d, rev, tb, hps):
    ri = lax.broadcasted_iota(jnp.int32, (tb, tb), 0)
    cj = lax.broadcasted_iota(jnp.int32, (tb, tb), 1)
    keep = jnp.where((cj >= ri) if rev else (cj <= ri), 1.0, 0.0).astype(BF16)
    half = tb // 2
    ri = lax.broadcasted_iota(jnp.int32, (half, half), 0)
    cj = lax.broadcasted_iota(jnp.int32, (half, half), 1)
    level = jnp.where((cj > ri) if rev else (cj < ri), 31 - lax.clz(ri ^ cj), -1)
    for h in range(hps):
        _hgrn_head(q_ref, u_ref, v_ref, lb_ref, o_ref, st_ref, d, h, rev, tb, keep, level)


def _hgrn_head(q_ref, u_ref, v_ref, lb_ref, o_ref, st_ref, d, h, rev, tb, keep, level):
    dk = HGRN_DK
    cols = slice(h * dk, (h + 1) * dk)
    lbv = lb_ref[:, cols]
    qraw = q_ref[:, cols]
    u = u_ref[:, cols]
    v = v_ref[:, cols]
    q = _silu(qraw) * dk ** -0.5
    la = jnp.log(lbv)
    lc = jnp.log1p(-lbv) + _log_sigmoid(u)
    mx = jnp.maximum(la, lc)
    mn = jnp.minimum(la, lc)
    logf = mx + jnp.log1p(jnp.exp(mn - mx))
    k = (1.0 - lbv) * _sigmoid(-u)

    bl = _dot_exact(keep, logf * LOG2E)
    b_end = bl[0:1] if rev else bl[tb - 1:tb]
    half = tb // 2
    halves = (slice(0, half), slice(half, tb))
    scores = [jnp.zeros((half, half), F32), jnp.zeros((half, half), F32)]
    for lv in range(half.bit_length() - 1):
        dq = bl - _level_ref(bl, 1 << lv, rev)
        qt = (q * jnp.exp2(dq)).astype(BF16)
        kt = (k * jnp.exp2(-dq)).astype(BF16)
        for hb, rows in enumerate(halves):
            scores[hb] = jnp.where(level == lv, _dot_nt(qt[rows], kt[rows]), scores[hb])
    isl, jsl = (halves[0], halves[1]) if rev else (halves[1], halves[0])
    dq = bl - _level_ref(bl, half, rev)
    cross = _dot_nt((q[isl] * jnp.exp2(dq[isl])).astype(BF16), (k[jsl] * jnp.exp2(-dq[jsl])).astype(BF16))
    vb = v.astype(BF16)
    st = st_ref[d, h]
    rest = (jnp.sum(q * k, axis=-1, keepdims=True) * v
            + _dot_nt((q * jnp.exp2(bl)).astype(BF16), st.astype(BF16)))
    for hb, rows in enumerate(halves):
        o = _dot(scores[hb].astype(BF16), vb[rows]) + rest[rows]
        if rows == isl:
            o = o + _dot(cross.astype(BF16), vb[jsl])
        o_ref[rows, cols] = o
    st_ref[d, h] = st * jnp.exp2(b_end) + _dot(v.T.astype(BF16), (k * jnp.exp2(b_end - bl)).astype(BF16))


def _hgrn_kernel(qf, uf, vf, qb, ub, vb, lb_ref, of, ob, st_ref, *, tb, hps):
    @pl.when(pl.program_id(2) == 0)
    def _():
        st_ref[...] = jnp.zeros_like(st_ref)
    _hgrn_dir(qf, uf, vf, lb_ref, of, st_ref, 0, False, tb, hps)
    _hgrn_dir(qb, ub, vb, lb_ref, ob, st_ref, 1, True, tb, hps)


def _hgrn_scan(p_main, lb_row, *, nb, ncb, nlb, tb, heads, off_q, off_ff, off_fb, off_i):
    t = p_main.shape[0]
    dk = HGRN_DK

    hps = HGRN_HEADS_PER_STEP
    wd = hps * dk

    def spec(off, rev):
        return pl.BlockSpec((tb, wd), lambda b, h, s: (_rowblk(b, s, rev, ncb, nlb, nb), off // wd + h))

    def ospec(rev):
        return pl.BlockSpec((tb, wd), lambda b, h, s: (_rowblk(b, s, rev, ncb, nlb, nb), h))

    out = jax.ShapeDtypeStruct((t, heads * dk), F32)
    return pl.pallas_call(
        functools.partial(_hgrn_kernel, tb=tb, hps=hps),
        grid=(nb, heads // hps, ncb + nlb),
        in_specs=[spec(off_q, False), spec(off_ff, False), spec(off_i, False),
                  spec(off_q, True), spec(off_fb, True), spec(off_i, True),
                  pl.BlockSpec((1, wd), lambda b, h, s: (0, h))],
        out_specs=[ospec(False), ospec(True)],
        out_shape=[out, out],
        scratch_shapes=[pltpu.VMEM((2, hps, dk, dk), F32)],
        compiler_params=_cparams(("parallel", "parallel", "arbitrary")), name="hgrn_scan",
    )(p_main, p_main, p_main, p_main, p_main, p_main, lb_row)


def _mlstm_dir(q_ref, k_ref, v_ref, g_ref, bias_ref, o_ref, c_ref, n_ref, m_ref, d, rev, tb, dk, dv):
    g = g_ref[...] + bias_ref[...]
    lf = _log_sigmoid(g) * LOG2E
    g = g * LOG2E
    keep = _tri(tb, rev)
    b = _dot_exact(jnp.where(keep, 1.0, 0.0).astype(BF16), lf)
    bt, gt = b.T, g.T
    for h in range(MLSTM_HEADS):
        ci = 2 * MLSTM_HEADS * d + h
        cf = ci + MLSTM_HEADS
        q = q_ref[:, h * dk:(h + 1) * dk]
        ks = k_ref[:, h * dk:(h + 1) * dk] * dk ** -0.5
        v = v_ref[:, h * dv:(h + 1) * dv]
        b_col, ic_col = b[:, cf:cf + 1], g[:, ci:ci + 1]
        b_row, ic_row = bt[cf:cf + 1, :], gt[ci:ci + 1, :]
        b_end = b_col[0:1] if rev else b_col[tb - 1:tb]
        m_prev = m_ref[d, h][:, 0:1]
        w_in = jnp.where(keep, b_col - b_row + ic_row, -jnp.inf)
        w_st = b_col + m_prev
        m_row = jnp.maximum(jnp.max(w_in, axis=1, keepdims=True), w_st)
        qb = q.astype(BF16)
        vb = v.astype(BF16)
        p = jnp.exp2(w_in - m_row) * _dot_nt(qb, ks.astype(BF16))
        e_st = jnp.exp2(w_st - m_row)
        num = _dot(p.astype(BF16), vb) + e_st * _dot(qb, c_ref[d, h].astype(BF16))
        nrm = (jnp.sum(p, axis=1, keepdims=True)
               + e_st * jnp.sum(q * n_ref[d, h], axis=1, keepdims=True))
        o_ref[:, h * dv:(h + 1) * dv] = num / jnp.maximum(jnp.abs(nrm), jnp.exp2(-m_row))
        w_end = b_end - b_col + ic_col
        m_new = jnp.maximum(b_end + m_prev, jnp.max(w_end, axis=0, keepdims=True))
        s_old = jnp.exp2(b_end + m_prev - m_new)
        kt = ks * jnp.exp2(w_end - m_new)
        c_ref[d, h] = s_old * c_ref[d, h] + _dot(kt.T.astype(BF16), vb)
        n_ref[d, h] = s_old * n_ref[d, h] + jnp.sum(kt, axis=0, keepdims=True)
        m_ref[d, h] = jnp.broadcast_to(m_new, m_ref.shape[2:])


def _mlstm_kernel(qf, kf, vf, gf, qb, kb, vb, gb, bias_ref, of, ob, c_ref, n_ref, m_ref, *, tb, dk, dv):
    @pl.when(pl.program_id(1) == 0)
    def _():
        c_ref[...] = jnp.zeros_like(c_ref)
        n_ref[...] = jnp.zeros_like(n_ref)
        m_ref[...] = jnp.zeros_like(m_ref)
    _mlstm_dir(qf, kf, vf, gf, bias_ref, of, c_ref, n_ref, m_ref, 0, False, tb, dk, dv)
    _mlstm_dir(qb, kb, vb, gb, bias_ref, ob, c_ref, n_ref, m_ref, 1, True, tb, dk, dv)


def _mlstm_scan(conv, p_main, p_tail, bias_row, *, nb, ncb, nlb, tb, dk, dv,
                off_q, off_k, off_v, off_small):
    t = conv.shape[0]

    nh = MLSTM_HEADS

    def spec(width, off, rev):
        return pl.BlockSpec((tb, width), lambda b, s: (_rowblk(b, s, rev, ncb, nlb, nb), off // width))

    def ospec(rev):
        return pl.BlockSpec((tb, nh * dv), lambda b, s: (_rowblk(b, s, rev, ncb, nlb, nb), 0))

    out = jax.ShapeDtypeStruct((t, nh * dv), F32)
    return pl.pallas_call(
        functools.partial(_mlstm_kernel, tb=tb, dk=dk, dv=dv),
        grid=(nb, ncb + nlb),
        in_specs=[spec(nh * dk, off_q, False), spec(nh * dk, off_k, False), spec(nh * dv, off_v, False),
                  spec(LANES, off_small, False),
                  spec(nh * dk, off_q, True), spec(nh * dk, off_k, True), spec(nh * dv, off_v, True),
                  spec(LANES, off_small, True),
                  pl.BlockSpec((1, LANES), lambda b, s: (0, 0))],
        out_specs=[ospec(False), ospec(True)],
        out_shape=[out, out],
        scratch_shapes=[pltpu.VMEM((2, nh, dk, dv), F32), pltpu.VMEM((2, nh, 1, dk), F32),
                        pltpu.VMEM((2, nh, 1, LANES), F32)],
        compiler_params=_cparams(("parallel", "arbitrary")), name="mlstm_scan",
    )(conv, conv, p_main, p_tail, conv, conv, p_main, p_tail, bias_row)


def _ssd_dir(c_ref, bm_ref, x_ref, g_ref, dtb_ref, alog_ref, o_ref, s_ref, d, rev, tb, heads, hpg):
    dt = _softplus(g_ref[...] + dtb_ref[...])
    la = -jnp.exp(alog_ref[...]) * dt
    keep = _tri(tb, rev)
    b = _dot_exact(jnp.where(keep, 1.0, 0.0).astype(BF16), la * LOG2E)
    bt = b.T
    width = hpg * SSD_HEADDIM
    head0 = MLSTM_HEADS * 4 + heads * d
    er = lax.broadcasted_iota(jnp.int32, (LANES, heads * SSD_HEADDIM), 0)
    ec = lax.broadcasted_iota(jnp.int32, (LANES, heads * SSD_HEADDIM), 1)
    spread = jnp.where(er == head0 + ec // SSD_HEADDIM, 1.0, 0.0).astype(BF16)
    b_wide = _dot_exact_rhs(b, spread)
    dt_wide = _dot_exact_rhs(dt, spread)
    for grp in range(SSD_GROUPS):
        lane0 = head0 + hpg * grp
        cm = c_ref[:, grp * SSD_STATE:(grp + 1) * SSD_STATE].astype(BF16)
        bm = bm_ref[:, grp * SSD_STATE:(grp + 1) * SSD_STATE]
        x = x_ref[:, grp * width:(grp + 1) * width]
        b_all = b_wide[:, grp * width:(grp + 1) * width]
        dt_all = dt_wide[:, grp * width:(grp + 1) * width]
        b_end = b_all[0:1] if rev else b_all[tb - 1:tb]
        v_all = x * dt_all
        v_bf = v_all.astype(BF16)
        gmat = _dot_nt(cm, bm.astype(BF16))
        s_old = s_ref[d, grp]
        inter = jnp.exp2(b_all) * _dot(cm, s_old.astype(BF16))
        outs = []
        for i in range(hpg):
            c = lane0 + i
            dec = jnp.where(keep, jnp.exp2(b[:, c:c + 1] - bt[c:c + 1, :]), 0.0)
            outs.append(_dot((gmat * dec).astype(BF16), v_bf[:, i * SSD_HEADDIM:(i + 1) * SSD_HEADDIM]))
        o_ref[:, grp * width:(grp + 1) * width] = jnp.concatenate(outs, axis=1) + inter
        sv = (v_all * jnp.exp2(b_end - b_all)).astype(BF16)
        s_ref[d, grp] = jnp.exp2(b_end) * s_old + _dot(bm.T.astype(BF16), sv)


def _ssd_kernel(cf, bf, xf, gf, cb, bb, xb, gb, dtb_ref, alog_ref, of, ob, s_ref, *, tb, heads, hpg):
    @pl.when(pl.program_id(1) == 0)
    def _():
        s_ref[...] = jnp.zeros_like(s_ref)
    _ssd_dir(cf, bf, xf, gf, dtb_ref, alog_ref, of, s_ref, 0, False, tb, heads, hpg)
    _ssd_dir(cb, bb, xb, gb, dtb_ref, alog_ref, ob, s_ref, 1, True, tb, heads, hpg)


def _ssd_scan(conv, p_tail, dtb_row, alog_row, *, nb, ncb, nlb, tb, heads, off_x, off_b, off_c, off_small):
    t = conv.shape[0]
    hpg = heads // SSD_GROUPS
    width = hpg * SSD_HEADDIM

    ng = SSD_GROUPS

    def spec(w, off, rev):
        return pl.BlockSpec((tb, w), lambda b, s: (_rowblk(b, s, rev, ncb, nlb, nb), off // w))

    def ospec(rev):
        return pl.BlockSpec((tb, ng * width), lambda b, s: (_rowblk(b, s, rev, ncb, nlb, nb), 0))

    row = pl.BlockSpec((1, LANES), lambda b, s: (0, 0))
    out = jax.ShapeDtypeStruct((t, heads * SSD_HEADDIM), F32)
    return pl.pallas_call(
        functools.partial(_ssd_kernel, tb=tb, heads=heads, hpg=hpg),
        grid=(nb, ncb + nlb),
        in_specs=[spec(ng * SSD_STATE, off_c, False), spec(ng * SSD_STATE, off_b, False),
                  spec(ng * width, off_x, False), spec(LANES, off_small, False),
                  spec(ng * SSD_STATE, off_c, True), spec(ng * SSD_STATE, off_b, True),
                  spec(ng * width, off_x, True), spec(LANES, off_small, True), row, row],
        out_specs=[ospec(False), ospec(True)],
        out_shape=[out, out],
        scratch_shapes=[pltpu.VMEM((2, ng, SSD_STATE, width), F32)],
        compiler_params=_cparams(("parallel", "arbitrary")), name="ssd_scan",
    )(conv, conv, conv, p_tail, conv, conv, conv, p_tail, dtb_row, alog_row)


def _headnorm_kernel(of_ref, ob_ref, g_ref, w_ref, out_ref, *, hd):
    cb = of_ref.shape[1]
    for i in range(cb // hd):
        sl = slice(i * hd, (i + 1) * hd)
        o = of_ref[:, sl] + ob_ref[:, sl]
        y = o * lax.rsqrt(jnp.mean(o * o, axis=-1, keepdims=True) + EPS) * w_ref[:, sl]
        out_ref[:, sl] = (y * _silu(g_ref[:, sl])).astype(BF16)


def _headnorm(of, ob, gsrc, goff, w, *, hd, tr, cb, name):
    t, wd = of.shape
    tr, cb = _tile(t, tr), _tile(wd, cb)
    gblk = goff // cb
    return pl.pallas_call(
        functools.partial(_headnorm_kernel, hd=hd), grid=(t // tr, wd // cb),
        in_specs=[pl.BlockSpec((tr, cb), lambda i, j: (i, j)),
                  pl.BlockSpec((tr, cb), lambda i, j: (i, j)),
                  pl.BlockSpec((tr, cb), lambda i, j: (i, gblk + j)),
                  pl.BlockSpec((1, cb), lambda i, j: (0, j))],
        out_specs=pl.BlockSpec((tr, cb), lambda i, j: (i, j)),
        out_shape=jax.ShapeDtypeStruct((t, wd), BF16),
        compiler_params=_cparams(("parallel", "parallel")), name=name,
    )(of, ob, gsrc, w)


def _ssd_finish_kernel(of_ref, ob_ref, x_ref, z_ref, d_ref, w_ref, out_ref):
    y = of_ref[...] + ob_ref[...] + d_ref[...] * x_ref[...]
    tt = y * _silu(z_ref[...])
    out_ref[...] = (tt * lax.rsqrt(jnp.mean(tt * tt, axis=-1, keepdims=True) + EPS)
                    * w_ref[...]).astype(BF16)


def _ssd_finish(of, ob, conv, off_x, p_tail, off_z, d_row, w, *, tr):
    t, wd = of.shape
    tr = _tile(t, tr)
    row = pl.BlockSpec((1, wd), lambda i: (0, 0))
    return pl.pallas_call(
        _ssd_finish_kernel, grid=(t // tr,),
        in_specs=[pl.BlockSpec((tr, wd), lambda i: (i, 0)),
                  pl.BlockSpec((tr, wd), lambda i: (i, 0)),
                  pl.BlockSpec((tr, wd), lambda i: (i, off_x // wd)),
                  pl.BlockSpec((tr, wd), lambda i: (i, off_z // wd)), row, row],
        out_specs=pl.BlockSpec((tr, wd), lambda i: (i, 0)),
        out_shape=jax.ShapeDtypeStruct((t, wd), BF16),
        compiler_params=_cparams(("parallel",)), name="ssd_finish",
    )(of, ob, conv, p_tail, d_row, w)


def _merge_kernel(y0, y1, y2, bw_ref, mg_ref, gw_ref, gb_ref, out_ref, acc_ref):
    k = pl.program_id(2)
    gate = _sigmoid(_dot(mg_ref[...].astype(BF16), gw_ref[...]) + gb_ref[...])
    for idx, y_ref in enumerate((y0, y1, y2)):
        @pl.when(k == idx)
        def _(y_ref=y_ref, idx=idx):
            r = gate * _dot(y_ref[...], bw_ref[...])
            if idx == 0:
                acc_ref[...] = r
            elif idx == 1:
                acc_ref[...] += r
            else:
                out_ref[...] = (acc_ref[...] + r).astype(BF16)


def _merge(ys, bw, p_tail, off_merge, rank, gw, gb, *, layer, tm, tn, row0):
    t, bwid = ys[0].shape
    d = bw.shape[-1]
    tm, tn = _tile(math.gcd(t, row0), tm), _tile(d, tn)
    nj = d // tn
    o = row0 // tm
    yspec = pl.BlockSpec((tm, bwid), lambda i, j, k: (i + o, 0))
    return pl.pallas_call(
        _merge_kernel, grid=((t - row0) // tm, nj, 3),
        in_specs=[yspec, yspec, yspec,
                  pl.BlockSpec((None, None, bwid, tn), lambda i, j, k: (layer, k, 0, j)),
                  pl.BlockSpec((tm, rank), lambda i, j, k: (i + o, off_merge // rank)),
                  pl.BlockSpec((None, rank, tn), lambda i, j, k: (layer, 0, k * nj + j)),
                  pl.BlockSpec((1, tn), lambda i, j, k: (0, k * nj + j))],
        out_specs=pl.BlockSpec((tm, tn), lambda i, j, k: (i + o, j)),
        out_shape=jax.ShapeDtypeStruct((t, d), BF16),
        scratch_shapes=[pltpu.VMEM((tm, tn), F32)],
        compiler_params=_cparams(("parallel", "parallel", "arbitrary")), name="branch_merge",
    )(ys[0], ys[1], ys[2], bw, p_tail, gw, gb)


def _swiglu_kernel(a_ref, w1_ref, w3_ref, out_ref, *accs, nk):
    def part():
        a = a_ref[...]
        return _dot(a, w1_ref[...]), _dot(a, w3_ref[...])

    def finish(r1, r3):
        out_ref[...] = (_silu(r1) * r3).astype(BF16)

    _accumulate(pl.program_id(2), nk, part, accs, finish)


def _swiglu_up(a, w1, w3, *, tm, tn, tk, row0, name):
    t, kdim = a.shape
    n = w1.shape[1]
    tm, tn, tk = _tile(math.gcd(t, row0), tm), _tile(n, tn), _tile(kdim, tk)
    nk = kdim // tk
    o = row0 // tm
    wspec = pl.BlockSpec((tk, tn), lambda i, j, k: (k, j))
    return pl.pallas_call(
        functools.partial(_swiglu_kernel, nk=nk),
        grid=((t - row0) // tm, n // tn, nk),
        in_specs=[pl.BlockSpec((tm, tk), lambda i, j, k: (i + o, k)), wspec, wspec],
        out_specs=pl.BlockSpec((tm, tn), lambda i, j, k: (i + o, j)),
        out_shape=jax.ShapeDtypeStruct((t, n), BF16),
        scratch_shapes=[pltpu.VMEM((tm, tn), F32)] * (2 if nk > 1 else 0),
        compiler_params=_cparams(("parallel", "parallel", "arbitrary")), name=name,
    )(a, w1, w3)


MOE_RB = LANES
MOE_MB = 512


def _moe_tables(cnt_tile, tm):
    nt, ne = cnt_tile.shape
    rb, bpm = MOE_RB, MOE_MB // MOE_RB
    ni = 2 * tm // rb + ne
    nblk = -(-(nt * ni + ne * (bpm - 1)) // bpm) * bpm
    nbk = (cnt_tile + (rb - 1)) // rb
    reg_e = (jnp.sum(nbk, axis=0) + (bpm - 1)) // bpm * bpm
    end_e = jnp.cumsum(reg_e)
    start_e = end_e - reg_e
    pre_ie = jnp.cumsum(nbk, axis=0) - nbk
    ends_ie = jnp.cumsum(nbk, axis=1)
    off_ie = ends_ie - nbk
    n_items = ends_ie[:, -1]
    it = jnp.arange(ni, dtype=jnp.int32)[None, :]
    itc = jnp.minimum(it, n_items[:, None] - 1)
    e_idx = jnp.sum((itc[:, :, None] >= ends_ie[:, None, :]).astype(jnp.int32), axis=-1)
    e_idx = jnp.minimum(e_idx, ne - 1)
    chunk = itc - jnp.take_along_axis(off_ie, e_idx, axis=1)
    gblk = start_e[e_idx] + jnp.take_along_axis(pre_ie, e_idx, axis=1) + chunk
    iout = jnp.where(it < n_items[:, None], gblk, nblk)
    m = jnp.arange(nblk // bpm, dtype=jnp.int32)
    mexp = jnp.sum((m[:, None] * bpm >= end_e[None, :]).astype(jnp.int32), axis=-1)
    off_row = jnp.zeros((nt, 1, LANES), F32).at[:, 0, :ne].set((off_ie * rb).astype(F32))
    return dict(ni=ni, nblk=nblk, iout=iout.reshape(-1).astype(jnp.int32),
                iblk=gblk.reshape(-1).astype(jnp.int32),
                mexp=jnp.minimum(mexp, ne - 1).astype(jnp.int32),
                nvalid=(end_e[-1:] // bpm).astype(jnp.int32), off_row=off_row)


def _moe_gather_kernel(iout_ref, h_ref, comb_ref, off_ref, za_ref, zw_ref,
                       a_ref, w_ref, l0_ref, l1_ref, lt_ref, wt_ref, *, tm, rb):
    del iout_ref, za_ref, zw_ref
    it = pl.program_id(1)

    @pl.when(it == 0)
    def _():
        comb = comb_ref[...]
        pick = comb > 0.0
        r = lax.broadcasted_iota(jnp.int32, (tm, tm), 0)
        c = lax.broadcasted_iota(jnp.int32, (tm, tm), 1)
        rank = _dot(jnp.where(c < r, 1.0, 0.0).astype(BF16), jnp.where(pick, 1.0, 0.0).astype(BF16))
        loc = off_ref[...] + rank
        lane = lax.broadcasted_iota(jnp.int32, comb.shape, 1).astype(F32)
        m1 = jnp.min(jnp.where(pick, lane, float(LANES)), axis=1, keepdims=True)
        first = pick & (lane == m1)
        second = pick & (lane != m1)
        l0 = jnp.sum(jnp.where(first, loc, 0.0), axis=1, keepdims=True)
        l1 = jnp.sum(jnp.where(second, loc + 1.0, 0.0), axis=1, keepdims=True) - 1.0
        l0b = jnp.broadcast_to(l0, comb.shape)
        l1b = jnp.broadcast_to(l1, comb.shape)
        l0_ref[...] = l0b
        l1_ref[...] = l1b
        lt_ref[0] = l0b.T
        lt_ref[1] = l1b.T
        wt_ref[0] = jnp.broadcast_to(jnp.sum(jnp.where(first, comb, 0.0), axis=1, keepdims=True), comb.shape)
        wt_ref[1] = jnp.broadcast_to(jnp.sum(jnp.where(second, comb, 0.0), axis=1, keepdims=True), comb.shape)

    s = (it * rb + lax.broadcasted_iota(jnp.int32, (rb, 1), 0)).astype(F32)
    p0 = jnp.where(lt_ref[0, 0:1, :] == s, 1.0, 0.0).astype(BF16)
    p1 = jnp.where(lt_ref[1, 0:1, :] == s, 1.0, 0.0).astype(BF16)
    a_ref[...] = _dot(p0 + p1, h_ref[...]).astype(BF16)
    w_ref[...] = _dot_exact(p0, wt_ref[0]) + _dot_exact(p1, wt_ref[1])


def _moe_gather(h, comb, tab, *, tm, row0):
    d = h.shape[1]
    t = h.shape[0] - row0
    o = row0 // tm
    rb, ni = MOE_RB, tab['ni']
    nrows = tab['nblk'] * rb + MOE_MB
    grid_spec = pltpu.PrefetchScalarGridSpec(
        num_scalar_prefetch=1, grid=(t // tm, ni),
        in_specs=[pl.BlockSpec((tm, d), lambda i, it, io: (i + o, 0)),
                  pl.BlockSpec((tm, LANES), lambda i, it, io: (i + o, 0)),
                  pl.BlockSpec((None, 1, LANES), lambda i, it, io: (i, 0, 0)),
                  pl.BlockSpec(memory_space=pl.ANY), pl.BlockSpec(memory_space=pl.ANY)],
        out_specs=[pl.BlockSpec((rb, d), lambda i, it, io: (io[i * ni + it], 0)),
                   pl.BlockSpec((rb, LANES), lambda i, it, io: (io[i * ni + it], 0)),
                   pl.BlockSpec((tm, LANES), lambda i, it, io: (i, 0)),
                   pl.BlockSpec((tm, LANES), lambda i, it, io: (i, 0))],
        scratch_shapes=[pltpu.VMEM((2, LANES, tm), F32), pltpu.VMEM((2, tm, LANES), F32)])
    return pl.pallas_call(
        functools.partial(_moe_gather_kernel, tm=tm, rb=rb), grid_spec=grid_spec,
        out_shape=[jax.ShapeDtypeStruct((nrows, d), BF16), jax.ShapeDtypeStruct((nrows, LANES), F32),
                   jax.ShapeDtypeStruct((t, LANES), F32), jax.ShapeDtypeStruct((t, LANES), F32)],
        input_output_aliases={4: 0, 5: 1},
        compiler_params=_cparams(("parallel", "arbitrary")), name="moe_gather",
    )(tab['iout'], h, comb, tab['off_row'], jnp.zeros((nrows, d), BF16), jnp.zeros((nrows, LANES), F32))


def _moe_grouped_kernel(*refs, nk, dual):
    if dual:
        mexp_ref, nv_ref, a_ref, w1_ref, w3_ref, ws_ref, out_ref, *accs = refs
    else:
        mexp_ref, nv_ref, a_ref, w1_ref, out_ref, *accs = refs
    del mexp_ref
    k = pl.program_id(2)

    def part():
        a = a_ref[...]
        if dual:
            return _dot(a, w1_ref[...]), _dot(a, w3_ref[...])
        return (_dot(a, w1_ref[...]),)

    def finish(r1, r3=None):
        if dual:
            out_ref[...] = (_silu(r1) * r3 * ws_ref[:, 0:1]).astype(BF16)
        else:
            out_ref[...] = r1.astype(BF16)

    @pl.when(pl.program_id(0) < nv_ref[0])
    def _():
        _accumulate(k, nk, part, accs, finish)


def _moe_grouped(a, ws, w_sorted, tab, *, tn, tk, name):
    rows, kdim = a.shape
    n = ws[0].shape[-1]
    tn, tk = _tile(n, tn), _tile(kdim, tk)
    nj, nk = n // tn, kdim // tk
    nmb = tab['nblk'] * MOE_RB // MOE_MB
    dual = len(ws) == 2

    def live(m, nv):
        return m < nv[0]

    def me(m, nv):
        return jnp.minimum(m, nv[0] - 1)

    a_spec = pl.BlockSpec((MOE_MB, tk), lambda m, j, k, ex, nv: (me(m, nv), jnp.where(live(m, nv), k, nk - 1)))
    w_spec = pl.BlockSpec((None, tk, tn), lambda m, j, k, ex, nv: (
        ex[me(m, nv)], jnp.where(live(m, nv), k, nk - 1), jnp.where(live(m, nv), j, nj - 1)))
    o_spec = pl.BlockSpec((MOE_MB, tn), lambda m, j, k, ex, nv: (me(m, nv), jnp.where(live(m, nv), j, nj - 1)))
    in_specs = [a_spec, w_spec]
    args = [a, ws[0]]
    if dual:
        in_specs += [w_spec, pl.BlockSpec((MOE_MB, LANES), lambda m, j, k, ex, nv: (me(m, nv), 0))]
        args += [ws[1], w_sorted]
    scratch = [pltpu.VMEM((MOE_MB, tn), F32)] * (len(ws) if nk > 1 else 0)
    grid_spec = pltpu.PrefetchScalarGridSpec(
        num_scalar_prefetch=2, grid=(nmb, nj, nk), in_specs=in_specs, out_specs=o_spec,
        scratch_shapes=scratch)
    return pl.pallas_call(
        functools.partial(_moe_grouped_kernel, nk=nk, dual=dual), grid_spec=grid_spec,
        out_shape=jax.ShapeDtypeStruct((rows, n), BF16),
        compiler_params=_cparams(("arbitrary", "arbitrary", "arbitrary")), name=name,
    )(tab['mexp'], tab['nvalid'], *args)


def _moe_scatter_kernel(iblk_ref, l0_ref, l1_ref, *rest, rb, ni, seg, tile0):
    del iblk_ref
    y_refs = rest[:ni]
    x_ref, mod_ref, out_ref, pt_ref, ya_ref = rest[ni:]
    row = _mod_row(pl.program_id(0) + tile0, *seg)

    @pl.when(pl.program_id(1) == 0)
    def _():
        l0, l1 = l0_ref[...], l1_ref[...]
        lane = lax.broadcasted_iota(jnp.int32, (1, rb), 1).astype(F32)
        for q in range(ni):
            s = lane + float(q * rb)
            pt_ref[:, q * rb:(q + 1) * rb] = jnp.where((l0 == s) | (l1 == s), 1.0, 0.0).astype(BF16)

    for q in range(ni):
        ya_ref[q * rb:(q + 1) * rb, :] = y_refs[q][...]
    out_ref[...] = x_ref[...] + mod_ref[pl.ds(row, 1), :] * _dot(pt_ref[...], ya_ref[...])


def _moe_scatter(l0, l1, ys, x, mod, gate_off, tab, *, tm, tn, seg_fn, row0):
    t, d = x.shape
    rb, ni = MOE_RB, tab['ni']
    tn = _tile(d, tn)
    gblk = gate_off // tn
    o = row0 // tm
    lspec = pl.BlockSpec((tm, LANES), lambda i, j, ib: (i, 0))
    yspecs = [pl.BlockSpec((rb, tn), lambda i, j, ib, q=q: (ib[i * ni + q], j)) for q in range(ni)]
    grid_spec = pltpu.PrefetchScalarGridSpec(
        num_scalar_prefetch=1, grid=((t - row0) // tm, d // tn),
        in_specs=[lspec, lspec, *yspecs,
                  pl.BlockSpec((tm, tn), lambda i, j, ib: (i + o, j)),
                  pl.BlockSpec((MOD_ROWS, tn), lambda i, j, ib: (0, gblk + j))],
        out_specs=pl.BlockSpec((tm, tn), lambda i, j, ib: (i + o, j)),
        scratch_shapes=[pltpu.VMEM((tm, ni * rb), BF16), pltpu.VMEM((ni * rb, tn), BF16)])
    return pl.pallas_call(
        functools.partial(_moe_scatter_kernel, rb=rb, ni=ni, seg=seg_fn(tm), tile0=o), grid_spec=grid_spec,
        out_shape=jax.ShapeDtypeStruct((t, d), F32),
        compiler_params=_cparams(("parallel", "arbitrary")), name="moe_scatter",
    )(tab['iblk'], l0, l1, *([ys] * ni), x, mod)


def kernel(x, c, ctx, c_ctx, mod_w, mod_b, norm1_w, norm2_w, in_w, conv_w, conv_b, hgrn_lb, hgrn_norm_w,
           mlstm_igate_b, mlstm_fgate_b, mlstm_norm_w, ssd_a_log, ssd_dt_bias, ssd_d, ssd_norm_w,
           gate_w, gate_b, branch_w, out_w, ffn_w1, ffn_w3, ffn_w2, router_w, moe_w1, moe_w3, moe_w2,
           final_norm_w):
    nb, seq, d = x.shape
    ctx_len = ctx.shape[1]
    depth = mod_w.shape[0]
    bw = d // 2
    hg_heads = bw // HGRN_DK
    ml_dv = bw // MLSTM_HEADS
    ml_dk = ml_dv // 2
    ss_heads = bw // SSD_HEADDIM
    rank = gate_w.shape[1]
    n_ml_qk = MLSTM_HEADS * ml_dk
    n_ss_bc = SSD_GROUPS * SSD_STATE
    off = {}
    pos = 0
    for nm, sz in (('ml_q', n_ml_qk), ('ml_k', n_ml_qk), ('ss_x', bw), ('ss_B', n_ss_bc), ('ss_C', n_ss_bc),
                   ('hg_q', bw), ('hg_f_fwd', bw), ('hg_f_bwd', bw), ('hg_i', bw), ('hg_g', bw),
                   ('ml_v', bw), ('ml_z', bw), ('ml_gates', 4 * MLSTM_HEADS), ('ss_z', bw),
                   ('ss_dt', 2 * ss_heads), ('merge', rank)):
        off[nm] = (pos, sz)
        pos += sz
    n_conv = off['hg_q'][0]
    n_main = off['ml_gates'][0]
    n_small = 4 * MLSTM_HEADS + 2 * ss_heads
    assert n_small <= LANES and ctx_len % GRID_W == 0 and seq % ctx_len == 0
    t_off_z, t_off_merge, t_off_small = 0, bw, bw + rank
    n_tail_raw = bw + rank + LANES
    n_tail = -(-n_tail_raw // 512) * 512

    tb = ctx_len
    ncb, nlb = 1, seq // tb
    n_ctx_rows = nb * ctx_len
    t = n_ctx_rows + nb * seq
    ctx_row = nb

    def seg_fn(tile):
        assert n_ctx_rows % tile == 0 and seq % tile == 0
        return (n_ctx_rows // tile, seq // tile, ctx_row)

    tm_big = _tile(math.gcd(n_ctx_rows, seq), 1024)

    c_all = jnp.zeros((MOD_ROWS, d), F32).at[:nb].set(c).at[ctx_row].set(c_ctx)

    in_w_bf = in_w.astype(BF16)
    branch_bf, gate_bf, out_bf = branch_w.astype(BF16), gate_w.astype(BF16), out_w.astype(BF16)
    w_tail = jnp.concatenate(
        [lax.slice_in_dim(in_w, off[nm][0], off[nm][0] + off[nm][1], axis=2)
         for nm in ('ss_z', 'merge', 'ml_gates', 'ss_dt')]
        + [jnp.zeros((depth, d, n_tail - n_tail_raw + LANES - n_small), F32)], axis=2).astype(BF16)
    lb_cum = jnp.cumsum(jax.nn.softmax(hgrn_lb.astype(F32), axis=0), axis=0)
    lower_bounds = lb_cum - lb_cum[0]

    for l in range(depth):
        r0 = n_ctx_rows if l == depth - 1 else 0
        mod = _mm(c_all, mod_w, w_lead=l, out_dtype=F32, tm=MOD_ROWS, tn=1024, tk=2048,
                  a_silu=True, bias=mod_b[l][None, :], name="mod")
        if l == 0:
            h, xs = _norm_mod_join(ctx.reshape(n_ctx_rows, d), x.reshape(nb * seq, d), norm1_w[l][None, :],
                                   mod, sh_off=0, sc_off=d, seg_fn=seg_fn, tr=tb)
        else:
            h = _norm_mod(xs, norm1_w[l][None, :], mod, sh_off=0, sc_off=d, seg_fn=seg_fn, tr=tb,
                          name="norm1")
        p_main = _mm(h, in_w_bf, w_lead=l, n=n_main, out_dtype=F32, tm=tm_big, tn=1024, tk=d,
                     name="in_proj_main")
        p_tail = _mm(h, w_tail, w_lead=l, out_dtype=F32, tm=tm_big, tn=n_tail // 2, tk=2048,
                     name="in_proj_tail")
        conv = _conv(p_main, conv_w[l].reshape(9, n_conv), conv_b[l][None, :], n_conv=n_conv, tc=tb,
                     ctx_len=ctx_len, n_ctx_tiles=n_ctx_rows // tb, tiles_per_img=seq // tb, cb=2048)

        hgf, hgb = _hgrn_scan(p_main, lower_bounds[l][None, :], nb=nb, ncb=ncb, nlb=nlb, tb=tb,
                              heads=hg_heads, off_q=off['hg_q'][0], off_ff=off['hg_f_fwd'][0],
                              off_fb=off['hg_f_bwd'][0], off_i=off['hg_i'][0])
        ml_bias = jnp.zeros((1, LANES), F32).at[0, :4 * MLSTM_HEADS].set(
            jnp.stack([mlstm_igate_b[l, 0], mlstm_fgate_b[l, 0],
                       mlstm_igate_b[l, 1], mlstm_fgate_b[l, 1]]).reshape(-1))
        mlf, mlb = _mlstm_scan(conv, p_main, p_tail, ml_bias, nb=nb, ncb=ncb, nlb=nlb, tb=tb,
                               dk=ml_dk, dv=ml_dv, off_q=off['ml_q'][0], off_k=off['ml_k'][0],
                               off_v=off['ml_v'][0], off_small=t_off_small)
        lo = 4 * MLSTM_HEADS
        dtb_row = jnp.zeros((1, LANES), F32).at[0, lo:lo + 2 * ss_heads].set(ssd_dt_bias[l].reshape(-1))
        alog_row = jnp.zeros((1, LANES), F32).at[0, lo:lo + 2 * ss_heads].set(ssd_a_log[l].reshape(-1))
        ssf, ssb = _ssd_scan(conv, p_tail, dtb_row, alog_row, nb=nb, ncb=ncb, nlb=nlb, tb=tb,
                             heads=ss_heads, off_x=off['ss_x'][0], off_b=off['ss_B'][0],
                             off_c=off['ss_C'][0], off_small=t_off_small)

        y_hg = _headnorm(hgf, hgb, p_main, off['hg_g'][0], hgrn_norm_w[l][None, :], hd=HGRN_DK,
                         tr=2 * tb, cb=1024, name="hgrn_finish")
        y_ml = _headnorm(mlf, mlb, p_main, off['ml_z'][0], mlstm_norm_w[l][None, :], hd=ml_dv,
                         tr=2 * tb, cb=1024, name="mlstm_finish")
        d_row = jnp.repeat(ssd_d[l], SSD_HEADDIM)[None, :]
        y_ss = _ssd_finish(ssf, ssb, conv, off['ss_x'][0], p_tail, t_off_z, d_row,
                           ssd_norm_w[l][None, :], tr=tb)
        ym = _merge((y_hg, y_ml, y_ss), branch_bf, p_tail, t_off_merge, rank,
                    gate_bf, gate_b[l][None, :], layer=l, tm=tm_big, tn=1024, row0=r0)
        xs = _mm(ym, out_bf, w_lead=l, out_dtype=F32, tm=tm_big, tn=1024, tk=d,
                 resid=(xs, mod, 2 * d, seg_fn), row0=r0, name="out_proj")

        i = l // 2
        if l % 2 == 0:
            h2 = _norm_mod(xs, norm2_w[l][None, :], mod, sh_off=3 * d, sc_off=4 * d, seg_fn=seg_fn,
                           tr=tb, row0=r0, name="norm2")
            mid = _swiglu_up(h2, ffn_w1[i].astype(BF16), ffn_w3[i].astype(BF16),
                             tm=tm_big, tn=1024, tk=2048, row0=r0, name="ffn_up")
            xs = _mm(mid, ffn_w2[i].astype(BF16), out_dtype=F32, tm=tm_big, tn=1024, tk=2048,
                     resid=(xs, mod, 5 * d, seg_fn), row0=r0, name="ffn_down")
        else:
            rw = jnp.zeros((d, LANES), F32).at[:, :N_EXPERTS].set(router_w[i])
            h2, comb, cnt = _norm_mod(xs, norm2_w[l][None, :], mod, sh_off=3 * d, sc_off=4 * d,
                                      seg_fn=seg_fn, tr=tb, router_w=rw, row0=r0, name="norm2_router")
            cnt_tile = cnt.reshape(t // tm_big, tm_big // tb, SUBLANES, LANES)[r0 // tm_big:, :, 0, :N_EXPERTS]
            tab = _moe_tables(jnp.sum(cnt_tile, axis=1).astype(jnp.int32), tm_big)
            a_s, w_s, l0, l1 = _moe_gather(h2, comb, tab, tm=tm_big, row0=r0)
            mid = _moe_grouped(a_s, (moe_w1[i].astype(BF16), moe_w3[i].astype(BF16)), w_s, tab,
                               tn=1024, tk=d, name="moe_up")
            ys = _moe_grouped(mid, (moe_w2[i].astype(BF16),), None, tab, tn=1024, tk=mid.shape[1],
                              name="moe_down")
            xs = _moe_scatter(l0, l1, ys, xs, mod, 5 * d, tab, tm=tm_big, tn=1024, seg_fn=seg_fn, row0=r0)

    out = _final_norm(xs, final_norm_w[None, :], row0=n_ctx_rows, rows=nb * seq, tr=tb)
    return out.reshape(nb, seq, d)
```
